```python
import math
import jax, jax.numpy as jnp
from jax import lax
import numpy as np

D_MODEL = 2048
BATCH = 4
SEQ = 8192
DEPTH = 2

POOL_WINDOWS = (2, 4, 8, 16)
POOL_GROUPS = 4
POOL_GROUP_DIM = D_MODEL // 16
POOL_WIDTH = POOL_GROUPS * POOL_GROUP_DIM
SB_HEAD_DIM = 128
SB_HEADS = D_MODEL // 256
SB_WIDTH = SB_HEADS * SB_HEAD_DIM
SB_BLOCK = 128
GLA_HEADS = 4
GLA_DK = D_MODEL // 16
GLA_DV = D_MODEL // 16
GLA_KEY_WIDTH = GLA_HEADS * GLA_DK
GLA_WIDTH = GLA_HEADS * GLA_DV
GLA_RANK = 16
GLA_TAU = 16.0
GLA_CHUNK = 64
N_BRANCH = 3
D_FF = ((8 * D_MODEL // 3 + 255) // 256) * 256
RMS_EPS = 1e-6
IN_WIDTHS = (POOL_WIDTH, SB_WIDTH, SB_WIDTH, SB_WIDTH,
             GLA_KEY_WIDTH, GLA_KEY_WIDTH, GLA_WIDTH, GLA_WIDTH, GLA_RANK,
             N_BRANCH * D_MODEL)
IN_COLS = POOL_WIDTH + 3 * SB_WIDTH + 2 * GLA_KEY_WIDTH + 2 * GLA_WIDTH + GLA_RANK + N_BRANCH * D_MODEL

kernel_name = "hybrid_pool_stickbreak_gla_gated"


def _rmsnorm(x, gain, eps=RMS_EPS):
    xf = x.astype(jnp.float32)
    y = xf * lax.rsqrt(jnp.mean(xf * xf, axis=-1, keepdims=True) + eps)
    return (y * gain.astype(jnp.float32)).astype(x.dtype)


def _split_cols(t, widths):
    outs, start = [], 0
    for w in widths:
        outs.append(t[..., start:start + w])
        start += w
    return outs


def _pool_mixer(u, w_pool, pool_scale):
    b, s, _ = u.shape
    ug = u.astype(jnp.float32).reshape(b, s, POOL_GROUPS, POOL_GROUP_DIM)
    cs = jnp.cumsum(ug, axis=1)
    t1 = jnp.arange(1, s + 1, dtype=jnp.float32)[None, :, None]
    groups = []
    for g, w in enumerate(POOL_WINDOWS):
        csg = cs[:, :, g]
        cs_shift = jnp.pad(csg, ((0, 0), (w, 0), (0, 0)))[:, :s]
        count = jnp.minimum(t1, float(w))
        groups.append((csg - cs_shift) / count - ug[:, :, g])
    pooled = jnp.stack(groups, axis=2)
    y = jnp.einsum('bsgc,gcd->bsgd', pooled.astype(u.dtype), w_pool) * pool_scale
    return y.reshape(b, s, POOL_WIDTH)


def _stick_breaking(q, k, v, q_gain, k_gain):
    b, s, _ = q.shape

    def heads(t):
        return t.reshape(b, s, SB_HEADS, SB_HEAD_DIM).transpose(0, 2, 1, 3)

    qh = _rmsnorm(heads(q), q_gain)
    kh = _rmsnorm(heads(k), k_gain)
    vh = heads(v)
    nb = s // SB_BLOCK
    q_blocks = qh.reshape(b, SB_HEADS, nb, SB_BLOCK, SB_HEAD_DIM).transpose(2, 0, 1, 3, 4)
    key_pos = jnp.arange(s)
    inv_sqrt_d = 1.0 / math.sqrt(SB_HEAD_DIM)

    def block(args):
        i, qb = args
        z = jnp.einsum('bhqd,bhkd->bhqk', qb, kh,
                       preferred_element_type=jnp.float32) * inv_sqrt_d
        q_pos = i * SB_BLOCK + jnp.arange(SB_BLOCK)
        mask = key_pos[None, :] < q_pos[:, None]
        log_beta = jax.nn.log_sigmoid(z)
        log_keep = jnp.where(mask, jax.nn.log_sigmoid(-z), 0.0)
        after = lax.cumsum(log_keep, axis=3, reverse=True) - log_keep
        a = jnp.where(mask, jnp.exp(log_beta + after), 0.0)
        return jnp.einsum('bhqk,bhkd->bhqd', a.astype(vh.dtype), vh)

    out = lax.map(block, (jnp.arange(nb), q_blocks))
    return out.transpose(1, 0, 3, 2, 4).reshape(b, s, SB_WIDTH)


def _gla(q, k, v, r, a_low, w_a2, b_a2, out_gain):
    b, s, _ = q.shape
    nc = s // GLA_CHUNK
    log_alpha = jax.nn.log_sigmoid((a_low @ w_a2 + b_a2).astype(jnp.float32)) / GLA_TAU

    def chunks(t, d):
        return t.astype(jnp.float32).reshape(b, nc, GLA_CHUNK, GLA_HEADS, d).transpose(1, 0, 3, 2, 4)

    qc = chunks(q, GLA_DK) * (GLA_DK ** -0.5)
    kc = chunks(k, GLA_DK)
    vc = chunks(v, GLA_DV)
    gc = chunks(log_alpha, GLA_DK)
    causal = jnp.tril(jnp.ones((GLA_CHUNK, GLA_CHUNK), dtype=bool))

    def step(state, inp):
        qi, ki, vi, gi = inp
        bcum = jnp.cumsum(gi, axis=2)
        o_inter = jnp.einsum('bhtd,bhde->bhte', qi * jnp.exp(bcum), state)
        diff = bcum[:, :, :, None, :] - bcum[:, :, None, :, :]
        decay = jnp.where(causal[:, :, None], jnp.exp(jnp.minimum(diff, 0.0)), 0.0)
        scores = jnp.einsum('bhtd,bhsd,bhtsd->bhts', qi, ki, decay)
        o_intra = jnp.einsum('bhts,bhse->bhte', scores, vi)
        b_last = bcum[:, :, -1, :]
        state = jnp.exp(b_last)[..., None] * state + jnp.einsum(
            'bhsd,bhse->bhde', ki * jnp.exp(b_last[:, :, None, :] - bcum), vi)
        return state, o_inter + o_intra

    state0 = jnp.zeros((b, GLA_HEADS, GLA_DK, GLA_DV), jnp.float32)
    _, o = lax.scan(step, state0, (qc, kc, vc, gc))
    o = o.transpose(1, 0, 3, 2, 4).reshape(b, s, GLA_HEADS, GLA_DV)
    o = _rmsnorm(o, out_gain).reshape(b, s, GLA_WIDTH)
    return (o * jax.nn.silu(r.astype(jnp.float32))).astype(q.dtype)


def _mixer(h, w_in, w_pool, pool_scale, sb_q_gain, sb_k_gain, gla_w_a2, gla_b_a2,
           gla_out_gain, w_br_pool, w_br_sb, w_br_gla, w_out):
    b, s, _ = h.shape
    proj = h @ w_in
    (u_pool, q_sb, k_sb, v_sb, q_gla, k_gla, v_gla, r_gla, a_gla,
     gate_logits) = _split_cols(proj, IN_WIDTHS)
    y_pool = _pool_mixer(u_pool, w_pool, pool_scale)
    y_sb = _stick_breaking(q_sb, k_sb, v_sb, sb_q_gain, sb_k_gain)
    y_gla = _gla(q_gla, k_gla, v_gla, r_gla, a_gla, gla_w_a2, gla_b_a2, gla_out_gain)
    gates = jax.nn.sigmoid(gate_logits.astype(jnp.float32)).reshape(b, s, N_BRANCH, D_MODEL)
    merged = (gates[:, :, 0] * (y_pool @ w_br_pool)
              + gates[:, :, 1] * (y_sb @ w_br_sb)
              + gates[:, :, 2] * (y_gla @ w_br_gla))
    return merged.astype(h.dtype) @ w_out


def _swiglu(h, w_gate, w_up, w_down):
    return (jax.nn.silu(h @ w_gate) * (h @ w_up)) @ w_down


def setup_inputs(seed: int = 0) -> dict:
    key = jax.random.key(seed)
    ks = jax.random.split(key, 24)
    f32 = jnp.float32

    def nrm(k, shape, scale):
        return jax.random.normal(k, shape, f32) * scale

    def gain(k, shape):
        return 1.0 + 0.02 * jax.random.normal(k, shape, f32)

    L = DEPTH
    return {
        "x": nrm(ks[0], (BATCH, SEQ, D_MODEL), 1.0),
        "c": nrm(ks[1], (BATCH, D_MODEL), 1.0),
        "w_ada": nrm(ks[2], (L, D_MODEL, 6 * D_MODEL), 0.5 * D_MODEL ** -0.5),
        "b_ada": nrm(ks[3], (L, 6 * D_MODEL), 0.01),
        "g_norm1": gain(ks[4], (L, D_MODEL)),
        "w_in": nrm(ks[5], (L, D_MODEL, IN_COLS), D_MODEL ** -0.5),
        "w_pool": nrm(ks[6], (L, POOL_GROUPS, POOL_GROUP_DIM, POOL_GROUP_DIM), POOL_GROUP_DIM ** -0.5),
        "pool_scale": gain(ks[7], (L, POOL_GROUPS, POOL_GROUP_DIM)),
        "sb_q_gain": gain(ks[8], (L, SB_HEAD_DIM)),
        "sb_k_gain": gain(ks[9], (L, SB_HEAD_DIM)),
        "gla_w_a2": nrm(ks[10], (L, GLA_RANK, GLA_KEY_WIDTH), GLA_RANK ** -0.5),
        "gla_b_a2": nrm(ks[11], (L, GLA_KEY_WIDTH), 0.1),
        "gla_out_gain": gain(ks[12], (L, GLA_DV)),
        "w_br_pool": nrm(ks[13], (L, POOL_WIDTH, D_MODEL), POOL_WIDTH ** -0.5),
        "w_br_sb": nrm(ks[14], (L, SB_WIDTH, D_MODEL), SB_WIDTH ** -0.5),
        "w_br_gla": nrm(ks[15], (L, GLA_WIDTH, D_MODEL), GLA_WIDTH ** -0.5),
        "w_out": nrm(ks[16], (L, D_MODEL, D_MODEL), D_MODEL ** -0.5),
        "g_norm2": gain(ks[17], (L, D_MODEL)),
        "w_ff_gate": nrm(ks[18], (L, D_MODEL, D_FF), D_MODEL ** -0.5),
        "w_ff_up": nrm(ks[19], (L, D_MODEL, D_FF), D_MODEL ** -0.5),
        "w_ff_down": nrm(ks[20], (L, D_FF, D_MODEL), D_FF ** -0.5),
    }


def reference(x, c, w_ada, b_ada, g_norm1, w_in, w_pool, pool_scale, sb_q_gain, sb_k_gain,
              gla_w_a2, gla_b_a2, gla_out_gain, w_br_pool, w_br_sb, w_br_gla, w_out,
              g_norm2, w_ff_gate, w_ff_up, w_ff_down):
    silu_c = jax.nn.silu(c)
    for l in range(DEPTH):
        mod = (silu_c @ w_ada[l] + b_ada[l])[:, None, :]
        sh1, sc1, ga1, sh2, sc2, ga2 = jnp.split(mod, 6, axis=-1)
        h = _rmsnorm(x, g_norm1[l]) * (1.0 + sc1) + sh1
        x = x + ga1 * _mixer(h, w_in[l], w_pool[l], pool_scale[l], sb_q_gain[l], sb_k_gain[l],
                             gla_w_a2[l], gla_b_a2[l], gla_out_gain[l],
                             w_br_pool[l], w_br_sb[l], w_br_gla[l], w_out[l])
        h = _rmsnorm(x, g_norm2[l]) * (1.0 + sc2) + sh2
        x = x + ga2 * _swiglu(h, w_ff_gate[l], w_ff_up[l], w_ff_down[l])
    return x
```

```python
import functools
import math

import numpy as np
import jax
import jax.numpy as jnp
from jax import lax
from jax.experimental import pallas as pl
from jax.experimental.pallas import tpu as pltpu

F32 = jnp.float32
BF16 = jnp.bfloat16

D_MODEL = 2048
POOL_WINDOWS = (2, 4, 8, 16)
POOL_GROUPS = 4
GROUP_DIM = 128
POOL_WIDTH = POOL_GROUPS * GROUP_DIM
SB_HEADS = 8
SB_WIDTH = SB_HEADS * GROUP_DIM
GLA_HEADS = 4
GLA_WIDTH = GLA_HEADS * GROUP_DIM
GLA_RANK = 16
GLA_TAU = 16.0
GLA_CHUNK = 128
GLA_LEVELS = 7
N_BRANCH = 3
D_FF = 5632
RMS_EPS = 1e-6
LOG2E = 1.4426950408889634
POOL_HALO = 16

VMEM_LIMIT = 56 * 1024 * 1024


def _cparams(sem):
    return pltpu.CompilerParams(dimension_semantics=sem, vmem_limit_bytes=VMEM_LIMIT)


def _nt_dot(a, b):
    return lax.dot_general(a, b, (((1,), (1,)), ((), ())), preferred_element_type=F32)


def _silu(x):
    return x * jax.nn.sigmoid(x)


def _ada_kernel(c_ref, w_ref, b_ref, o_ref):
    c = c_ref[...]
    a = _silu(c).astype(BF16)
    o_ref[0] = jnp.dot(a, w_ref[0].astype(BF16), preferred_element_type=F32) + b_ref[0]


def _ada(c_pad, w_ada, b_ada, tn=1024):
    depth, d, n = w_ada.shape
    rows = c_pad.shape[0]
    return pl.pallas_call(
        _ada_kernel,
        grid=(depth, n // tn),
        in_specs=[
            pl.BlockSpec((rows, d), lambda l, j: (0, 0)),
            pl.BlockSpec((1, d, tn), lambda l, j: (l, 0, j)),
            pl.BlockSpec((1, 1, tn), lambda l, j: (l, 0, j)),
        ],
        out_specs=pl.BlockSpec((1, rows, tn), lambda l, j: (l, 0, j)),
        out_shape=jax.ShapeDtypeStruct((depth, rows, n), F32),
        compiler_params=_cparams(("arbitrary", "arbitrary")),
        name="ada_modulation",
    )(c_pad, w_ada, b_ada.reshape(depth, 1, n))


def _modulated_norm(x, gain, scale, shift):
    ms = jnp.mean(x * x, axis=-1, keepdims=True)
    y = x * lax.rsqrt(ms + RMS_EPS) * gain
    return y * (1.0 + scale) + shift


def _group_rmsnorm_store(acc, colgain_ref, o_ref, tn):
    for c in range(tn // GROUP_DIM):
        sl = slice(c * GROUP_DIM, (c + 1) * GROUP_DIM)
        blk = acc[:, sl]
        ms = jnp.mean(blk * blk, axis=-1, keepdims=True)
        o_ref[:, sl] = (blk * lax.rsqrt(ms + RMS_EPS) * colgain_ref[:, sl]).astype(o_ref.dtype)


def _normproj_kernel(x_ref, g_ref, sc_ref, sh_ref, w_ref, cg_ref, o_ref, h_ref, *, n_norm_tiles, tn):
    j = pl.program_id(1)

    @pl.when(j == 0)
    def _():
        h_ref[...] = _modulated_norm(x_ref[...], g_ref[...], sc_ref[0], sh_ref[0]).astype(BF16)

    acc = jnp.dot(h_ref[...], w_ref[...], preferred_element_type=F32)
    if n_norm_tiles == 0:
        o_ref[...] = acc.astype(o_ref.dtype)
    else:
        @pl.when(j < n_norm_tiles)
        def _():
            _group_rmsnorm_store(acc, cg_ref, o_ref, tn)

        @pl.when(j >= n_norm_tiles)
        def _():
            o_ref[...] = acc.astype(o_ref.dtype)


def _normproj(x2, gain, scale, shift, w, colgain, *, seq, tm, tn, n_norm_tiles, out_dtype):
    m, d = x2.shape
    n = w.shape[1]
    tpb = seq // tm
    kern = functools.partial(_normproj_kernel, n_norm_tiles=n_norm_tiles, tn=tn)
    return pl.pallas_call(
        kern,
        grid=(m // tm, n // tn),
        in_specs=[
            pl.BlockSpec((tm, d), lambda i, j: (i, 0)),
            pl.BlockSpec((1, d), lambda i, j: (0, 0)),
            pl.BlockSpec((1, 1, d), lambda i, j: (i // tpb, 0, 0)),
            pl.BlockSpec((1, 1, d), lambda i, j: (i // tpb, 0, 0)),
            pl.BlockSpec((d, tn), lambda i, j: (0, j)),
            pl.BlockSpec((1, tn), lambda i, j: (0, j)),
        ],
        out_specs=pl.BlockSpec((tm, tn), lambda i, j: (i, j)),
        out_shape=jax.ShapeDtypeStruct((m, n), out_dtype),
        scratch_shapes=[pltpu.VMEM((tm, d), BF16)],
        compiler_params=_cparams(("arbitrary", "arbitrary")),
        name="norm_proj",
    )(x2, gain, scale, shift, w, colgain)


def _pool_kernel(u_ref, w_ref, ps_ref, o_ref, ext_ref, *, ts):
    i = pl.program_id(1)

    @pl.when(i == 0)
    def _():
        ext_ref[0:POOL_HALO, :] = jnp.zeros((POOL_HALO, POOL_WIDTH), F32)

    @pl.when(i > 0)
    def _():
        ext_ref[0:POOL_HALO, :] = ext_ref[ts:ts + POOL_HALO, :]

    ext_ref[POOL_HALO:, :] = u_ref[0]
    pos1 = (i * ts + 1 + lax.broadcasted_iota(jnp.int32, (ts, GROUP_DIM), 0)).astype(F32)
    for g, w in enumerate(POOL_WINDOWS):
        cols = slice(g * GROUP_DIM, (g + 1) * GROUP_DIM)
        u = ext_ref[POOL_HALO:, cols]
        win = u
        for k in range(1, w):
            win = win + ext_ref[POOL_HALO - k:POOL_HALO - k + ts, cols]
        pooled = win / jnp.minimum(pos1, float(w)) - u
        y = jnp.dot(pooled.astype(BF16), w_ref[g], preferred_element_type=F32) * ps_ref[g]
        o_ref[0, :, cols] = y.astype(o_ref.dtype)


def _pool(proj3, w_pool, pool_scale, ts):
    b, s, _ = proj3.shape
    kern = functools.partial(_pool_kernel, ts=ts)
    return pl.pallas_call(
        kern,
        grid=(b, s // ts),
        in_specs=[
            pl.BlockSpec((1, ts, POOL_WIDTH), lambda bi, i: (bi, i, 0)),
            pl.BlockSpec((POOL_GROUPS, GROUP_DIM, GROUP_DIM), lambda bi, i: (0, 0, 0)),
            pl.BlockSpec((POOL_GROUPS, 1, GROUP_DIM), lambda bi, i: (0, 0, 0)),
        ],
        out_specs=pl.BlockSpec((1, ts, POOL_WIDTH), lambda bi, i: (bi, i, 0)),
        out_shape=jax.ShapeDtypeStruct((b, s, POOL_WIDTH), BF16),
        scratch_shapes=[pltpu.VMEM((ts + POOL_HALO, POOL_WIDTH), F32)],
        compiler_params=_cparams(("arbitrary", "arbitrary")),
        name="pool_mixer",
    )(proj3, w_pool, pool_scale.reshape(POOL_GROUPS, 1, GROUP_DIM))


def _sb_cumsum_matrix(tk):
    j = np.arange(tk)[:, None]
    s = np.arange(tk)[None, :]
    one = np.concatenate([(j >= s).astype(np.float32), np.ones((tk, tk), np.float32)], axis=1)
    return jnp.asarray(np.concatenate([one, one], axis=0), dtype=BF16)


def _sb_kernel(q_ref, k_ref, v_ref, w2_ref, o_ref, acc_ref, run_ref, *, tq, tk):
    i = pl.program_id(2)
    q = q_ref[0]
    w2 = w2_ref[...]
    acc_ref[...] = jnp.zeros((tq, GROUP_DIM), F32)
    run_ref[...] = jnp.zeros((tq, tk), F32)
    row = lax.broadcasted_iota(jnp.int32, (tq, tk), 0)
    col = lax.broadcasted_iota(jnp.int32, (tq, tk), 1)

    def block(jb, masked):
        start = pl.multiple_of(jb * tk, tk)
        kb = k_ref[0, pl.ds(start, tk), :]
        vb = v_ref[0, pl.ds(start, tk), :]
        z = _nt_dot(q, kb)
        e = jnp.exp2(-jnp.abs(z))
        neg_log_keep = jnp.maximum(z, 0.0) + jnp.log(1.0 + e) * LOG2E
        if masked:
            mask = (jb * tk + col) < (i * tq + row)
            neg_log_keep = jnp.where(mask, neg_log_keep, 0.0)
        hi = neg_log_keep.astype(BF16)
        lo = (neg_log_keep - hi.astype(F32)).astype(BF16)
        c = jnp.dot(jnp.concatenate([hi, lo], axis=1), w2, preferred_element_type=F32)
        a = jnp.exp2(z - c[:, :tk] - run_ref[...])
        if masked:
            a = jnp.where(mask, a, 0.0)
        acc_ref[...] += jnp.dot(a.astype(BF16), vb, preferred_element_type=F32)
        run_ref[...] += c[:, tk:]

    n_diag = tq // tk
    for d in range(n_diag):
        block(i * n_diag + (n_diag - 1 - d), True)

    def body(it, carry):
        block(i * n_diag - 1 - it, False)
        return carry

    lax.fori_loop(0, i * n_diag, body, 0)
    o_ref[0] = acc_ref[...].astype(o_ref.dtype)


def _stick_breaking(qkv, tq, tk):
    b, s, _ = qkv.shape
    kern = functools.partial(_sb_kernel, tq=tq, tk=tk)
    return pl.pallas_call(
        kern,
        grid=(b, SB_HEADS, s // tq),
        in_specs=[
            pl.BlockSpec((1, tq, GROUP_DIM), lambda bi, h, i: (bi, i, h)),
            pl.BlockSpec((1, s, GROUP_DIM), lambda bi, h, i: (bi, 0, SB_HEADS + h)),
            pl.BlockSpec((1, s, GROUP_DIM), lambda bi, h, i: (bi, 0, 2 * SB_HEADS + h)),
            pl.BlockSpec((2 * tk, 2 * tk), lambda bi, h, i: (0, 0)),
        ],
        out_specs=pl.BlockSpec((1, tq, GROUP_DIM), lambda bi, h, i: (bi, i, h)),
        out_shape=jax.ShapeDtypeStruct((b, s, SB_WIDTH), BF16),
        scratch_shapes=[pltpu.VMEM((tq, GROUP_DIM), F32), pltpu.VMEM((tq, tk), F32)],
        compiler_params=_cparams(("arbitrary", "arbitrary", "arbitrary")),
        name="stick_breaking",
    )(qkv, qkv, qkv, _sb_cumsum_matrix(tk))


def _gla_constants():
    c = GLA_CHUNK
    t = np.arange(c)[:, None]
    j = np.arange(c)[None, :]
    mats = []
    for l in range(GLA_LEVELS):
        m = 1 << l
        mid = (t // (2 * m)) * (2 * m) + m - 1
        upper = (t % (2 * m)) >= m
        mats.append(np.where(upper, (j > mid) & (j <= t), (j > t) & (j <= mid)))
    mats.append(j <= t)
    mats.append(j > t)
    mats.append(np.ones((8, c), bool))
    dst = np.concatenate(mats, axis=0).astype(np.float32)
    s = j
    masks = []
    for l in range(GLA_LEVELS):
        masks.append(((t ^ s) >> l == 1) & (t > s))
    masks.append(t == s)
    return jnp.asarray(dst, dtype=BF16), jnp.asarray(np.stack(masks).astype(np.float32))


def _split3(x):
    x1 = x.astype(BF16)
    r = x - x1.astype(F32)
    x2 = r.astype(BF16)
    x3 = (r - x2.astype(F32)).astype(BF16)
    return x1, x2, x3


def _gla_kernel(q_ref, k_ref, v_ref, r_ref, a_ref, wa_ref, ba_ref, og_ref, dst_ref, msk_ref,
                o_ref, st_ref):
    c = GLA_CHUNK

    @pl.when(pl.program_id(1) == 0)
    def _():
        st_ref[...] = jnp.zeros(st_ref.shape, F32)

    x = jnp.dot(a_ref[0].astype(BF16), wa_ref[...], preferred_element_type=F32) + ba_ref[...]
    log_sig = jnp.minimum(x, 0.0) - jnp.log(1.0 + jnp.exp(-jnp.abs(x)))
    g = log_sig * (LOG2E / GLA_TAU)
    dst = dst_ref[...]
    g1, g2, g3 = _split3(g)
    e_all = (jnp.dot(dst, g1, preferred_element_type=F32)
             + jnp.dot(dst, g2, preferred_element_type=F32)
             + jnp.dot(dst, g3, preferred_element_type=F32))

    for h in range(GLA_HEADS):
        cols = slice(h * GROUP_DIM, (h + 1) * GROUP_DIM)
        q = q_ref[0, :, cols] * (GROUP_DIM ** -0.5)
        k = k_ref[0, :, cols]
        v = v_ref[0, :, cols].astype(BF16)
        scores = msk_ref[GLA_LEVELS] * _nt_dot(q.astype(BF16), k.astype(BF16))
        for l in range(GLA_LEVELS):
            xl = jnp.exp2(e_all[l * c:(l + 1) * c, cols])
            scores = scores + msk_ref[l] * _nt_dot((q * xl).astype(BF16), (k * xl).astype(BF16))
        o = jnp.dot(scores.astype(BF16), v, preferred_element_type=F32)
        x_in = jnp.exp2(e_all[GLA_LEVELS * c:(GLA_LEVELS + 1) * c, cols])
        st = st_ref[h]
        o = o + _nt_dot((q * x_in).astype(BF16), st.astype(BF16))
        x_out = jnp.exp2(e_all[(GLA_LEVELS + 1) * c:(GLA_LEVELS + 2) * c, cols])
        x_all = jnp.exp2(e_all[(GLA_LEVELS + 2) * c:(GLA_LEVELS + 2) * c + 1, cols])
        vt = v_ref[0, :, cols].T.astype(BF16)
        st_ref[h] = st * x_all + jnp.dot(vt, (k * x_out).astype(BF16), preferred_element_type=F32)
        ms = jnp.mean(o * o, axis=-1, keepdims=True)
        o = o * lax.rsqrt(ms + RMS_EPS) * og_ref[...]
        o_ref[0, :, cols] = (o * _silu(r_ref[0, :, cols])).astype(o_ref.dtype)


def _gla(proj3, w_a2p, b_a2, out_gain):
    b, s, _ = proj3.shape
    c = GLA_CHUNK
    dst, masks = _gla_constants()
    wide = lambda blk: pl.BlockSpec((1, c, GLA_WIDTH), lambda bi, ci: (bi, ci, blk))
    const2 = lambda shape: pl.BlockSpec(shape, lambda bi, ci: (0, 0))
    a_blk = (POOL_WIDTH + 4 * GLA_WIDTH) // GROUP_DIM
    return pl.pallas_call(
        _gla_kernel,
        grid=(b, s // c),
        in_specs=[
            wide(1), wide(2), wide(3), wide(4),
            pl.BlockSpec((1, c, GROUP_DIM), lambda bi, ci: (bi, ci, a_blk)),
            const2((GROUP_DIM, GLA_WIDTH)),
            const2((1, GLA_WIDTH)),
            const2((1, GROUP_DIM)),
            const2(dst.shape),
            pl.BlockSpec(masks.shape, lambda bi, ci: (0, 0, 0)),
        ],
        out_specs=pl.BlockSpec((1, c, GLA_WIDTH), lambda bi, ci: (bi, ci, 0)),
        out_shape=jax.ShapeDtypeStruct((b, s, GLA_WIDTH), BF16),
        scratch_shapes=[pltpu.VMEM((GLA_HEADS, GROUP_DIM, GROUP_DIM), F32)],
        compiler_params=_cparams(("arbitrary", "arbitrary")),
        name="gla",
    )(proj3, proj3, proj3, proj3, proj3, w_a2p, b_a2, out_gain, dst, masks)


def _merge_kernel(x_ref, g_ref, sc_ref, sh_ref, yp_ref, ys_ref, yg_ref,
                  wgp_ref, wgs_ref, wgg_ref, wp_ref, ws_ref, wg_ref, o_ref, h_ref):
    @pl.when(pl.program_id(1) == 0)
    def _():
        h_ref[...] = _modulated_norm(x_ref[...], g_ref[...], sc_ref[0], sh_ref[0]).astype(BF16)

    h = h_ref[...]

    def branch(wgate_ref, y_ref, w_ref):
        gate = jax.nn.sigmoid(jnp.dot(h, wgate_ref[...], preferred_element_type=F32))
        return gate * jnp.dot(y_ref[...], w_ref[...], preferred_element_type=F32)

    merged = branch(wgp_ref, yp_ref, wp_ref) + branch(wgs_ref, ys_ref, ws_ref) + branch(wgg_ref, yg_ref, wg_ref)
    o_ref[...] = merged.astype(o_ref.dtype)


def _merge(x2, gain, scale, shift, y_pool, y_sb, y_gla, w_gates, w_br_pool, w_br_sb, w_br_gla,
           *, seq, tm, tn):
    m, d = x2.shape
    tpb = seq // tm
    nj = d // tn
    row = lambda width: pl.BlockSpec((tm, width), lambda i, j: (i, 0))
    mod = pl.BlockSpec((1, 1, d), lambda i, j: (i // tpb, 0, 0))
    gate_w = lambda br: pl.BlockSpec((d, tn), lambda i, j: (0, br * nj + j))
    br_w = lambda width: pl.BlockSpec((width, tn), lambda i, j: (0, j))
    return pl.pallas_call(
        _merge_kernel,
        grid=(m // tm, nj),
        in_specs=[
            row(d), pl.BlockSpec((1, d), lambda i, j: (0, 0)), mod, mod,
            row(POOL_WIDTH), row(SB_WIDTH), row(GLA_WIDTH),
            gate_w(0), gate_w(1), gate_w(2),
            br_w(POOL_WIDTH), br_w(SB_WIDTH), br_w(GLA_WIDTH),
        ],
        out_specs=pl.BlockSpec((tm, tn), lambda i, j: (i, j)),
        out_shape=jax.ShapeDtypeStruct((m, d), BF16),
        scratch_shapes=[pltpu.VMEM((tm, d), BF16)],
        compiler_params=_cparams(("arbitrary", "arbitrary")),
        name="gated_merge",
    )(x2, gain, scale, shift, y_pool, y_sb, y_gla, w_gates, w_gates, w_gates,
      w_br_pool, w_br_sb, w_br_gla)


def _resproj_kernel(a_ref, w_ref, x_ref, ga_ref, o_ref):
    y = jnp.dot(a_ref[...], w_ref[...], preferred_element_type=F32)
    o_ref[...] = x_ref[...] + ga_ref[0] * y


def _resproj(a, w, x2, gate, *, seq, tm, tn):
    m, kdim = a.shape
    d = w.shape[1]
    tpb = seq // tm
    return pl.pallas_call(
        _resproj_kernel,
        grid=(m // tm, d // tn),
        in_specs=[
            pl.BlockSpec((tm, kdim), lambda i, j: (i, 0)),
            pl.BlockSpec((kdim, tn), lambda i, j: (0, j)),
            pl.BlockSpec((tm, tn), lambda i, j: (i, j)),
            pl.BlockSpec((1, 1, tn), lambda i, j: (i // tpb, 0, j)),
        ],
        out_specs=pl.BlockSpec((tm, tn), lambda i, j: (i, j)),
        out_shape=jax.ShapeDtypeStruct((m, d), F32),
        compiler_params=_cparams(("arbitrary", "arbitrary")),
        name="residual_proj",
    )(a, w, x2, gate)


def _swiglu_kernel(x_ref, g_ref, sc_ref, sh_ref, ga_ref, wg_ref, wu_ref, wd_ref, o_ref, h_ref, acc_ref):
    f = pl.program_id(1)

    @pl.when(f == 0)
    def _():
        h_ref[...] = _modulated_norm(x_ref[...], g_ref[...], sc_ref[0], sh_ref[0]).astype(BF16)
        acc_ref[...] = jnp.zeros(acc_ref.shape, F32)

    h = h_ref[...]
    gate = jnp.dot(h, wg_ref[...], preferred_element_type=F32)
    up = jnp.dot(h, wu_ref[...], preferred_element_type=F32)
    act = (_silu(gate) * up).astype(BF16)
    acc_ref[...] += jnp.dot(act, wd_ref[...], preferred_element_type=F32)

    @pl.when(f == pl.num_programs(1) - 1)
    def _():
        o_ref[...] = x_ref[...] + ga_ref[0] * acc_ref[...]


def _swiglu(x2, gain, scale, shift, gate, w_gate, w_up, w_down, *, seq, tm, tf):
    m, d = x2.shape
    dff = w_gate.shape[1]
    tpb = seq // tm
    mod = pl.BlockSpec((1, 1, d), lambda i, f: (i // tpb, 0, 0))
    return pl.pallas_call(
        _swiglu_kernel,
        grid=(m // tm, dff // tf),
        in_specs=[
            pl.BlockSpec((tm, d), lambda i, f: (i, 0)),
            pl.BlockSpec((1, d), lambda i, f: (0, 0)),
            mod, mod, mod,
            pl.BlockSpec((d, tf), lambda i, f: (0, f)),
            pl.BlockSpec((d, tf), lambda i, f: (0, f)),
            pl.BlockSpec((tf, d), lambda i, f: (f, 0)),
        ],
        out_specs=pl.BlockSpec((tm, d), lambda i, f: (i, 0)),
        out_shape=jax.ShapeDtypeStruct((m, d), F32),
        scratch_shapes=[pltpu.VMEM((tm, d), BF16), pltpu.VMEM((tm, d), F32)],
        compiler_params=_cparams(("arbitrary", "arbitrary")),
        name="swiglu",
    )(x2, gain, scale, shift, gate, w_gate, w_up, w_down)


def _tile(n, pref):
    t = min(n, pref)
    assert n % t == 0, (n, t)
    return t


def _layer(x2, mod_l, batch, seq, g_norm1, w_in, w_pool, pool_scale, sb_q_gain, sb_k_gain,
           gla_w_a2, gla_b_a2, gla_out_gain, w_br_pool, w_br_sb, w_br_gla, w_out,
           g_norm2, w_ff_gate, w_ff_up, w_ff_down):
    d = D_MODEL
    sh1, sc1, ga1, sh2, sc2, ga2 = [mod_l[:, None, k * d:(k + 1) * d] for k in range(6)]
    g1 = g_norm1.reshape(1, d)
    g2 = g_norm2.reshape(1, d)

    o_sb = POOL_WIDTH
    o_gla = o_sb + 3 * SB_WIDTH
    o_a = o_gla + 4 * GLA_WIDTH
    o_gate = o_a + GLA_RANK
    a_pad = GROUP_DIM - GLA_RANK
    w_f32cols = jnp.concatenate(
        [w_in[:, :o_sb], w_in[:, o_gla:o_a], jnp.pad(w_in[:, o_a:o_gate], ((0, 0), (0, a_pad)))],
        axis=1).astype(BF16)
    w_sb = w_in[:, o_sb:o_gla].astype(BF16)
    w_gates = w_in[:, o_gate:].astype(BF16)

    tm = _tile(seq, 1024)
    ones_cols = jnp.ones((1, w_f32cols.shape[1]), F32)
    proj = _normproj(x2, g1, sc1, sh1, w_f32cols, ones_cols, seq=seq, tm=tm, tn=896,
                     n_norm_tiles=0, out_dtype=F32)
    q_scale = LOG2E / math.sqrt(GROUP_DIM)
    colgain = jnp.concatenate([jnp.tile(sb_q_gain * q_scale, SB_HEADS), jnp.tile(sb_k_gain, SB_HEADS),
                               jnp.ones((SB_WIDTH,), F32)]).reshape(1, 3 * SB_WIDTH)
    tn_sb = 1024
    qkv = _normproj(x2, g1, sc1, sh1, w_sb, colgain, seq=seq, tm=tm, tn=tn_sb,
                    n_norm_tiles=2 * SB_WIDTH // tn_sb, out_dtype=BF16)

    proj3 = proj.reshape(batch, seq, proj.shape[1])
    y_pool = _pool(proj3, w_pool.astype(BF16), pool_scale, ts=_tile(seq, 512))
    y_sb = _stick_breaking(qkv.reshape(batch, seq, 3 * SB_WIDTH), tq=_tile(seq, 512), tk=128)
    w_a2p = jnp.pad(gla_w_a2, ((0, a_pad), (0, 0))).astype(BF16)
    y_gla = _gla(proj3, w_a2p, gla_b_a2.reshape(1, GLA_WIDTH), gla_out_gain.reshape(1, GROUP_DIM))

    m = batch * seq
    merged = _merge(x2, g1, sc1, sh1, y_pool.reshape(m, POOL_WIDTH), y_sb.reshape(m, SB_WIDTH),
                    y_gla.reshape(m, GLA_WIDTH), w_gates, w_br_pool.astype(BF16), w_br_sb.astype(BF16),
                    w_br_gla.astype(BF16), seq=seq, tm=_tile(seq, 512), tn=512)
    x2 = _resproj(merged, w_out.astype(BF16), x2, ga1, seq=seq, tm=tm, tn=512)
    x2 = _swiglu(x2, g2, sc2, sh2, ga2, w_ff_gate.astype(BF16), w_ff_up.astype(BF16),
                 w_ff_down.astype(BF16), seq=seq, tm=_tile(seq, 512), tf=512)
    return x2


def kernel(x, c, w_ada, b_ada, g_norm1, w_in, w_pool, pool_scale, sb_q_gain, sb_k_gain, gla_w_a2, gla_b_a2, gla_out_gain, w_br_pool, w_br_sb, w_br_gla, w_out, g_norm2, w_ff_gate, w_ff_up, w_ff_down):
    batch, seq, d = x.shape
    depth = w_ada.shape[0]
    c_pad = jnp.pad(c, ((0, 8 - batch % 8 if batch % 8 else 0), (0, 0)))
    mod = _ada(c_pad, w_ada, b_ada)[:, :batch]
    x2 = x.reshape(batch * seq, d)
    for l in range(depth):
        x2 = _layer(x2, mod[l], batch, seq, g_norm1[l], w_in[l], w_pool[l], pool_scale[l],
                    sb_q_gain[l], sb_k_gain[l], gla_w_a2[l], gla_b_a2[l], gla_out_gain[l],
                    w_br_pool[l], w_br_sb[l], w_br_gla[l], w_out[l], g_norm2[l],
                    w_ff_gate[l], w_ff_up[l], w_ff_down[l])
    return x2.reshape(batch, seq, d)
```

```python
import functools
import math

import numpy as np
import jax
import jax.numpy as jnp
from jax import lax
from jax.experimental import pallas as pl
from jax.experimental.pallas import tpu as pltpu

F32 = jnp.float32
BF16 = jnp.bfloat16

D_MODEL = 2048
POOL_WINDOWS = (2, 4, 8, 16)
POOL_GROUPS = 4
GROUP_DIM = 128
POOL_WIDTH = POOL_GROUPS * GROUP_DIM
SB_HEADS = 8
SB_WIDTH = SB_HEADS * GROUP_DIM
GLA_HEADS = 4
GLA_WIDTH = GLA_HEADS * GROUP_DIM
GLA_RANK = 16
GLA_TAU = 16.0
GLA_CHUNK = 128
GLA_LEVELS = 7
N_BRANCH = 3
D_FF = 5632
RMS_EPS = 1e-6
LOG2E = 1.4426950408889634
POOL_HALO = 16
SB_DEAD_BITS = 160.0

VMEM_LIMIT = 56 * 1024 * 1024


def _cparams(sem):
    return pltpu.CompilerParams(dimension_semantics=sem, vmem_limit_bytes=VMEM_LIMIT)


def _nt_dot(a, b):
    return lax.dot_general(a, b, (((1,), (1,)), ((), ())), preferred_element_type=F32)


def _silu(x):
    return x * jax.nn.sigmoid(x)


def _ada_kernel(c_ref, w_ref, b_ref, o_ref):
    c = c_ref[...]
    a = _silu(c).astype(BF16)
    o_ref[0] = jnp.dot(a, w_ref[0].astype(BF16), preferred_element_type=F32) + b_ref[0]


def _ada(c_pad, w_ada, b_ada, tn=1024):
    depth, d, n = w_ada.shape
    rows = c_pad.shape[0]
    return pl.pallas_call(
        _ada_kernel,
        grid=(depth, n // tn),
        in_specs=[
            pl.BlockSpec((rows, d), lambda l, j: (0, 0)),
            pl.BlockSpec((1, d, tn), lambda l, j: (l, 0, j)),
            pl.BlockSpec((1, 1, tn), lambda l, j: (l, 0, j)),
        ],
        out_specs=pl.BlockSpec((1, rows, tn), lambda l, j: (l, 0, j)),
        out_shape=jax.ShapeDtypeStruct((depth, rows, n), F32),
        compiler_params=_cparams(("arbitrary", "arbitrary")),
        name="ada_modulation",
    )(c_pad, w_ada, b_ada.reshape(depth, 1, n))


def _modulated_norm(x, gain, scale, shift):
    ms = jnp.mean(x * x, axis=-1, keepdims=True)
    y = x * lax.rsqrt(ms + RMS_EPS) * gain
    return y * (1.0 + scale) + shift


def _group_rmsnorm_store(acc, colgain_ref, o_ref, tn):
    for c in range(tn // GROUP_DIM):
        sl = slice(c * GROUP_DIM, (c + 1) * GROUP_DIM)
        blk = acc[:, sl]
        ms = jnp.mean(blk * blk, axis=-1, keepdims=True)
        o_ref[:, sl] = (blk * lax.rsqrt(ms + RMS_EPS) * colgain_ref[:, sl]).astype(o_ref.dtype)


def _normproj_kernel(x_ref, g_ref, sc_ref, sh_ref, w_ref, cg_ref, o_ref, h_ref, *, n_norm_tiles, tn):
    j = pl.program_id(1)

    @pl.when(j == 0)
    def _():
        h_ref[...] = _modulated_norm(x_ref[...], g_ref[...], sc_ref[0], sh_ref[0]).astype(BF16)

    acc = jnp.dot(h_ref[...], w_ref[...], preferred_element_type=F32)
    if n_norm_tiles == 0:
        o_ref[...] = acc.astype(o_ref.dtype)
    else:
        @pl.when(j < n_norm_tiles)
        def _():
            _group_rmsnorm_store(acc, cg_ref, o_ref, tn)

        @pl.when(j >= n_norm_tiles)
        def _():
            o_ref[...] = acc.astype(o_ref.dtype)


def _normproj(x2, gain, scale, shift, w, colgain, *, seq, tm, tn, n_norm_tiles, out_dtype):
    m, d = x2.shape
    n = w.shape[1]
    tpb = seq // tm
    kern = functools.partial(_normproj_kernel, n_norm_tiles=n_norm_tiles, tn=tn)
    return pl.pallas_call(
        kern,
        grid=(m // tm, n // tn),
        in_specs=[
            pl.BlockSpec((tm, d), lambda i, j: (i, 0)),
            pl.BlockSpec((1, d), lambda i, j: (0, 0)),
            pl.BlockSpec((1, 1, d), lambda i, j: (i // tpb, 0, 0)),
            pl.BlockSpec((1, 1, d), lambda i, j: (i // tpb, 0, 0)),
            pl.BlockSpec((d, tn), lambda i, j: (0, j)),
            pl.BlockSpec((1, tn), lambda i, j: (0, j)),
        ],
        out_specs=pl.BlockSpec((tm, tn), lambda i, j: (i, j)),
        out_shape=jax.ShapeDtypeStruct((m, n), out_dtype),
        scratch_shapes=[pltpu.VMEM((tm, d), BF16)],
        compiler_params=_cparams(("arbitrary", "arbitrary")),
        name="norm_proj",
    )(x2, gain, scale, shift, w, colgain)


def _pool_kernel(u_ref, w_ref, ps_ref, o_ref, ext_ref, *, ts):
    i = pl.program_id(1)

    @pl.when(i == 0)
    def _():
        ext_ref[0:POOL_HALO, :] = jnp.zeros((POOL_HALO, POOL_WIDTH), F32)

    @pl.when(i > 0)
    def _():
        ext_ref[0:POOL_HALO, :] = ext_ref[ts:ts + POOL_HALO, :]

    ext_ref[POOL_HALO:, :] = u_ref[0]
    pos1 = (i * ts + 1 + lax.broadcasted_iota(jnp.int32, (ts, GROUP_DIM), 0)).astype(F32)
    for g, w in enumerate(POOL_WINDOWS):
        cols = slice(g * GROUP_DIM, (g + 1) * GROUP_DIM)
        u = ext_ref[POOL_HALO:, cols]
        win = u
        for k in range(1, w):
            win = win + ext_ref[POOL_HALO - k:POOL_HALO - k + ts, cols]
        pooled = win / jnp.minimum(pos1, float(w)) - u
        y = jnp.dot(pooled.astype(BF16), w_ref[g], preferred_element_type=F32) * ps_ref[g]
        o_ref[0, :, cols] = y.astype(o_ref.dtype)


def _pool(proj3, w_pool, pool_scale, ts):
    b, s, _ = proj3.shape
    kern = functools.partial(_pool_kernel, ts=ts)
    return pl.pallas_call(
        kern,
        grid=(b, s // ts),
        in_specs=[
            pl.BlockSpec((1, ts, POOL_WIDTH), lambda bi, i: (bi, i, 0)),
            pl.BlockSpec((POOL_GROUPS, GROUP_DIM, GROUP_DIM), lambda bi, i: (0, 0, 0)),
            pl.BlockSpec((POOL_GROUPS, 1, GROUP_DIM), lambda bi, i: (0, 0, 0)),
        ],
        out_specs=pl.BlockSpec((1, ts, POOL_WIDTH), lambda bi, i: (bi, i, 0)),
        out_shape=jax.ShapeDtypeStruct((b, s, POOL_WIDTH), BF16),
        scratch_shapes=[pltpu.VMEM((ts + POOL_HALO, POOL_WIDTH), F32)],
        compiler_params=_cparams(("arbitrary", "arbitrary")),
        name="pool_mixer",
    )(proj3, w_pool, pool_scale.reshape(POOL_GROUPS, 1, GROUP_DIM))


def _sb_cumsum_matrix(tk):
    j = np.arange(tk)[:, None]
    s = np.arange(tk)[None, :]
    one = np.concatenate([(j >= s).astype(np.float32), np.ones((tk, tk), np.float32)], axis=1)
    return jnp.asarray(np.concatenate([one, one], axis=0), dtype=BF16)


def _sb_kernel(q_ref, k_ref, v_ref, w2_ref, o_ref, acc_ref, run_ref, *, tq, tk):
    i = pl.program_id(2)
    w2 = w2_ref[...]
    acc_ref[...] = jnp.zeros((tq, GROUP_DIM), F32)
    run_ref[...] = jnp.zeros((tq, tk), F32)

    def block(jb, r0, masked):
        rows = slice(r0, tq)
        start = pl.multiple_of(jb * tk, tk)
        kb = k_ref[0, pl.ds(start, tk), :]
        vb = v_ref[0, pl.ds(start, tk), :]
        z = _nt_dot(q_ref[0, rows, :], kb)
        e = jnp.exp2(-jnp.abs(z))
        neg_log_keep = jnp.maximum(z, 0.0) + jnp.log(1.0 + e) * LOG2E
        if masked:
            row = lax.broadcasted_iota(jnp.int32, z.shape, 0)
            col = lax.broadcasted_iota(jnp.int32, z.shape, 1)
            mask = (jb * tk + col) < (i * tq + r0 + row)
            neg_log_keep = jnp.where(mask, neg_log_keep, 0.0)
        hi = neg_log_keep.astype(BF16)
        lo = (neg_log_keep - hi.astype(F32)).astype(BF16)
        c = jnp.dot(jnp.concatenate([hi, lo], axis=1), w2, preferred_element_type=F32)
        a = jnp.exp2(z - c[:, :tk] - run_ref[rows, :])
        if masked:
            a = jnp.where(mask, a, 0.0)
        acc_ref[rows, :] += jnp.dot(a.astype(BF16), vb, preferred_element_type=F32)
        run_ref[rows, :] += c[:, tk:]

    n_diag = tq // tk
    for d in reversed(range(n_diag)):
        block(i * n_diag + d, d * tk, True)

    def cond(carry):
        jb, min_run = carry
        return jnp.logical_and(jb >= 0, min_run < SB_DEAD_BITS)

    def body(carry):
        jb, _ = carry
        block(jb, 0, False)
        return jb - 1, jnp.min(run_ref[...])

    lax.while_loop(cond, body, (i * n_diag - 1, jnp.min(run_ref[...])))
    o_ref[0] = acc_ref[...].astype(o_ref.dtype)


def _stick_breaking(qkv, tq, tk):
    b, s, _ = qkv.shape
    kern = functools.partial(_sb_kernel, tq=tq, tk=tk)
    return pl.pallas_call(
        kern,
        grid=(b, SB_HEADS, s // tq),
        in_specs=[
            pl.BlockSpec((1, tq, GROUP_DIM), lambda bi, h, i: (bi, i, h)),
            pl.BlockSpec((1, s, GROUP_DIM), lambda bi, h, i: (bi, 0, SB_HEADS + h)),
            pl.BlockSpec((1, s, GROUP_DIM), lambda bi, h, i: (bi, 0, 2 * SB_HEADS + h)),
            pl.BlockSpec((2 * tk, 2 * tk), lambda bi, h, i: (0, 0)),
        ],
        out_specs=pl.BlockSpec((1, tq, GROUP_DIM), lambda bi, h, i: (bi, i, h)),
        out_shape=jax.ShapeDtypeStruct((b, s, SB_WIDTH), BF16),
        scratch_shapes=[pltpu.VMEM((tq, GROUP_DIM), F32), pltpu.VMEM((tq, tk), F32)],
        compiler_params=_cparams(("arbitrary", "arbitrary", "arbitrary")),
        name="stick_breaking",
    )(qkv, qkv, qkv, _sb_cumsum_matrix(tk))


def _gla_constants():
    c = GLA_CHUNK
    t = np.arange(c)[:, None]
    j = np.arange(c)[None, :]
    mats = []
    for l in range(GLA_LEVELS):
        m = 1 << l
        mid = (t // (2 * m)) * (2 * m) + m - 1
        upper = (t % (2 * m)) >= m
        mats.append(np.where(upper, (j > mid) & (j <= t), (j > t) & (j <= mid)))
    mats.append(j <= t)
    mats.append(j > t)
    mats.append(np.ones((8, c), bool))
    dst = np.concatenate(mats, axis=0).astype(np.float32)
    s = j
    masks = []
    for l in range(GLA_LEVELS):
        masks.append(((t ^ s) >> l == 1) & (t > s))
    masks.append(t == s)
    return jnp.asarray(dst, dtype=BF16), jnp.asarray(np.stack(masks).astype(np.float32))


def _split3(x):
    x1 = x.astype(BF16)
    r = x - x1.astype(F32)
    x2 = r.astype(BF16)
    x3 = (r - x2.astype(F32)).astype(BF16)
    return x1, x2, x3


def _gla_kernel(q_ref, k_ref, v_ref, r_ref, a_ref, wa_ref, ba_ref, og_ref, dst_ref, msk_ref,
                o_ref, st_ref):
    c = GLA_CHUNK

    @pl.when(pl.program_id(1) == 0)
    def _():
        st_ref[...] = jnp.zeros(st_ref.shape, F32)

    x = jnp.dot(a_ref[0].astype(BF16), wa_ref[...], preferred_element_type=F32) + ba_ref[...]
    log_sig = jnp.minimum(x, 0.0) - jnp.log(1.0 + jnp.exp(-jnp.abs(x)))
    g = log_sig * (LOG2E / GLA_TAU)
    dst = dst_ref[...]
    g1, g2, g3 = _split3(g)
    e_all = (jnp.dot(dst, g1, preferred_element_type=F32)
             + jnp.dot(dst, g2, preferred_element_type=F32)
             + jnp.dot(dst, g3, preferred_element_type=F32))

    for h in range(GLA_HEADS):
        cols = slice(h * GROUP_DIM, (h + 1) * GROUP_DIM)
        q = q_ref[0, :, cols] * (GROUP_DIM ** -0.5)
        k = k_ref[0, :, cols]
        v = v_ref[0, :, cols].astype(BF16)
        scores = msk_ref[GLA_LEVELS] * _nt_dot(q.astype(BF16), k.astype(BF16))
        for l in range(GLA_LEVELS):
            xl = jnp.exp2(e_all[l * c:(l + 1) * c, cols])
            scores = scores + msk_ref[l] * _nt_dot((q * xl).astype(BF16), (k * xl).astype(BF16))
        o = jnp.dot(scores.astype(BF16), v, preferred_element_type=F32)
        x_in = jnp.exp2(e_all[GLA_LEVELS * c:(GLA_LEVELS + 1) * c, cols])
        st = st_ref[h]
        o = o + _nt_dot((q * x_in).astype(BF16), st.astype(BF16))
        x_out = jnp.exp2(e_all[(GLA_LEVELS + 1) * c:(GLA_LEVELS + 2) * c, cols])
        x_all = jnp.exp2(e_all[(GLA_LEVELS + 2) * c:(GLA_LEVELS + 2) * c + 1, cols])
        vt = v_ref[0, :, cols].T.astype(BF16)
        st_ref[h] = st * x_all + jnp.dot(vt, (k * x_out).astype(BF16), preferred_element_type=F32)
        ms = jnp.mean(o * o, axis=-1, keepdims=True)
        o = o * lax.rsqrt(ms + RMS_EPS) * og_ref[...]
        o_ref[0, :, cols] = (o * _silu(r_ref[0, :, cols])).astype(o_ref.dtype)


def _gla(proj3, w_a2p, b_a2, out_gain):
    b, s, _ = proj3.shape
    c = GLA_CHUNK
    dst, masks = _gla_constants()
    wide = lambda blk: pl.BlockSpec((1, c, GLA_WIDTH), lambda bi, ci: (bi, ci, blk))
    const2 = lambda shape: pl.BlockSpec(shape, lambda bi, ci: (0, 0))
    a_blk = (POOL_WIDTH + 4 * GLA_WIDTH) // GROUP_DIM
    return pl.pallas_call(
        _gla_kernel,
        grid=(b, s // c),
        in_specs=[
            wide(1), wide(2), wide(3), wide(4),
            pl.BlockSpec((1, c, GROUP_DIM), lambda bi, ci: (bi, ci, a_blk)),
            const2((GROUP_DIM, GLA_WIDTH)),
            const2((1, GLA_WIDTH)),
            const2((1, GROUP_DIM)),
            const2(dst.shape),
            pl.BlockSpec(masks.shape, lambda bi, ci: (0, 0, 0)),
        ],
        out_specs=pl.BlockSpec((1, c, GLA_WIDTH), lambda bi, ci: (bi, ci, 0)),
        out_shape=jax.ShapeDtypeStruct((b, s, GLA_WIDTH), BF16),
        scratch_shapes=[pltpu.VMEM((GLA_HEADS, GROUP_DIM, GROUP_DIM), F32)],
        compiler_params=_cparams(("arbitrary", "arbitrary")),
        name="gla",
    )(proj3, proj3, proj3, proj3, proj3, w_a2p, b_a2, out_gain, dst, masks)


def _merge_kernel(x_ref, g_ref, sc_ref, sh_ref, yp_ref, ys_ref, yg_ref,
                  wgp_ref, wgs_ref, wgg_ref, wp_ref, ws_ref, wg_ref, o_ref, h_ref):
    @pl.when(pl.program_id(1) == 0)
    def _():
        h_ref[...] = _modulated_norm(x_ref[...], g_ref[...], sc_ref[0], sh_ref[0]).astype(BF16)

    h = h_ref[...]

    def branch(wgate_ref, y_ref, w_ref):
        gate = jax.nn.sigmoid(jnp.dot(h, wgate_ref[...], preferred_element_type=F32))
        return gate * jnp.dot(y_ref[...], w_ref[...], preferred_element_type=F32)

    merged = branch(wgp_ref, yp_ref, wp_ref) + branch(wgs_ref, ys_ref, ws_ref) + branch(wgg_ref, yg_ref, wg_ref)
    o_ref[...] = merged.astype(o_ref.dtype)


def _merge(x2, gain, scale, shift, y_pool, y_sb, y_gla, w_gates, w_br_pool, w_br_sb, w_br_gla,
           *, seq, tm, tn):
    m, d = x2.shape
    tpb = seq // tm
    nj = d // tn
    row = lambda width: pl.BlockSpec((tm, width), lambda i, j: (i, 0))
    mod = pl.BlockSpec((1, 1, d), lambda i, j: (i // tpb, 0, 0))
    gate_w = lambda br: pl.BlockSpec((d, tn), lambda i, j: (0, br * nj + j))
    br_w = lambda width: pl.BlockSpec((width, tn), lambda i, j: (0, j))
    return pl.pallas_call(
        _merge_kernel,
        grid=(m // tm, nj),
        in_specs=[
            row(d), pl.BlockSpec((1, d), lambda i, j: (0, 0)), mod, mod,
            row(POOL_WIDTH), row(SB_WIDTH), row(GLA_WIDTH),
            gate_w(0), gate_w(1), gate_w(2),
            br_w(POOL_WIDTH), br_w(SB_WIDTH), br_w(GLA_WIDTH),
        ],
        out_specs=pl.BlockSpec((tm, tn), lambda i, j: (i, j)),
        out_shape=jax.ShapeDtypeStruct((m, d), BF16),
        scratch_shapes=[pltpu.VMEM((tm, d), BF16)],
        compiler_params=_cparams(("arbitrary", "arbitrary")),
        name="gated_merge",
    )(x2, gain, scale, shift, y_pool, y_sb, y_gla, w_gates, w_gates, w_gates,
      w_br_pool, w_br_sb, w_br_gla)


def _resproj_kernel(a_ref, w_ref, x_ref, ga_ref, o_ref):
    y = jnp.dot(a_ref[...], w_ref[...], preferred_element_type=F32)
    o_ref[...] = x_ref[...] + ga_ref[0] * y


def _resproj(a, w, x2, gate, *, seq, tm, tn):
    m, kdim = a.shape
    d = w.shape[1]
    tpb = seq // tm
    return pl.pallas_call(
        _resproj_kernel,
        grid=(m // tm, d // tn),
        in_specs=[
            pl.BlockSpec((tm, kdim), lambda i, j: (i, 0)),
            pl.BlockSpec((kdim, tn), lambda i, j: (0, j)),
            pl.BlockSpec((tm, tn), lambda i, j: (i, j)),
            pl.BlockSpec((1, 1, tn), lambda i, j: (i // tpb, 0, j)),
        ],
        out_specs=pl.BlockSpec((tm, tn), lambda i, j: (i, j)),
        out_shape=jax.ShapeDtypeStruct((m, d), F32),
        compiler_params=_cparams(("arbitrary", "arbitrary")),
        name="residual_proj",
    )(a, w, x2, gate)


def _swiglu_kernel(x_ref, g_ref, sc_ref, sh_ref, ga_ref, wg_ref, wu_ref, wd_ref, o_ref, h_ref, acc_ref):
    f = pl.program_id(1)

    @pl.when(f == 0)
    def _():
        h_ref[...] = _modulated_norm(x_ref[...], g_ref[...], sc_ref[0], sh_ref[0]).astype(BF16)
        acc_ref[...] = jnp.zeros(acc_ref.shape, F32)

    h = h_ref[...]
    gate = jnp.dot(h, wg_ref[...], preferred_element_type=F32)
    up = jnp.dot(h, wu_ref[...], preferred_element_type=F32)
    act = (_silu(gate) * up).astype(BF16)
    acc_ref[...] += jnp.dot(act, wd_ref[...], preferred_element_type=F32)

    @pl.when(f == pl.num_programs(1) - 1)
    def _():
        o_ref[...] = x_ref[...] + ga_ref[0] * acc_ref[...]


def _swiglu(x2, gain, scale, shift, gate, w_gate, w_up, w_down, *, seq, tm, tf):
    m, d = x2.shape
    dff = w_gate.shape[1]
    tpb = seq // tm
    mod = pl.BlockSpec((1, 1, d), lambda i, f: (i // tpb, 0, 0))
    return pl.pallas_call(
        _swiglu_kernel,
        grid=(m // tm, dff // tf),
        in_specs=[
            pl.BlockSpec((tm, d), lambda i, f: (i, 0)),
            pl.BlockSpec((1, d), lambda i, f: (0, 0)),
            mod, mod, mod,
            pl.BlockSpec((d, tf), lambda i, f: (0, f)),
            pl.BlockSpec((d, tf), lambda i, f: (0, f)),
            pl.BlockSpec((tf, d), lambda i, f: (f, 0)),
        ],
        out_specs=pl.BlockSpec((tm, d), lambda i, f: (i, 0)),
        out_shape=jax.ShapeDtypeStruct((m, d), F32),
        scratch_shapes=[pltpu.VMEM((tm, d), BF16), pltpu.VMEM((tm, d), F32)],
        compiler_params=_cparams(("arbitrary", "arbitrary")),
        name="swiglu",
    )(x2, gain, scale, shift, gate, w_gate, w_up, w_down)


def _tile(n, pref):
    t = min(n, pref)
    assert n % t == 0, (n, t)
    return t


def _layer(x2, mod_l, batch, seq, g_norm1, w_in, w_pool, pool_scale, sb_q_gain, sb_k_gain,
           gla_w_a2, gla_b_a2, gla_out_gain, w_br_pool, w_br_sb, w_br_gla, w_out,
           g_norm2, w_ff_gate, w_ff_up, w_ff_down):
    d = D_MODEL
    sh1, sc1, ga1, sh2, sc2, ga2 = [mod_l[:, None, k * d:(k + 1) * d] for k in range(6)]
    g1 = g_norm1.reshape(1, d)
    g2 = g_norm2.reshape(1, d)

    o_sb = POOL_WIDTH
    o_gla = o_sb + 3 * SB_WIDTH
    o_a = o_gla + 4 * GLA_WIDTH
    o_gate = o_a + GLA_RANK
    a_pad = GROUP_DIM - GLA_RANK
    w_f32cols = jnp.concatenate(
        [w_in[:, :o_sb], w_in[:, o_gla:o_a], jnp.pad(w_in[:, o_a:o_gate], ((0, 0), (0, a_pad)))],
        axis=1).astype(BF16)
    w_sb = w_in[:, o_sb:o_gla].astype(BF16)
    w_gates = w_in[:, o_gate:].astype(BF16)

    tm = _tile(seq, 1024)
    ones_cols = jnp.ones((1, w_f32cols.shape[1]), F32)
    proj = _normproj(x2, g1, sc1, sh1, w_f32cols, ones_cols, seq=seq, tm=tm, tn=896,
                     n_norm_tiles=0, out_dtype=F32)
    q_scale = LOG2E / math.sqrt(GROUP_DIM)
    colgain = jnp.concatenate([jnp.tile(sb_q_gain * q_scale, SB_HEADS), jnp.tile(sb_k_gain, SB_HEADS),
                               jnp.ones((SB_WIDTH,), F32)]).reshape(1, 3 * SB_WIDTH)
    tn_sb = 1024
    qkv = _normproj(x2, g1, sc1, sh1, w_sb, colgain, seq=seq, tm=tm, tn=tn_sb,
                    n_norm_tiles=2 * SB_WIDTH // tn_sb, out_dtype=BF16)

    proj3 = proj.reshape(batch, seq, proj.shape[1])
    y_pool = _pool(proj3, w_pool.astype(BF16), pool_scale, ts=_tile(seq, 512))
    y_sb = _stick_breaking(qkv.reshape(batch, seq, 3 * SB_WIDTH), tq=_tile(seq, 512), tk=128)
    w_a2p = jnp.pad(gla_w_a2, ((0, a_pad), (0, 0))).astype(BF16)
    y_gla = _gla(proj3, w_a2p, gla_b_a2.reshape(1, GLA_WIDTH), gla_out_gain.reshape(1, GROUP_DIM))

    m = batch * seq
    merged = _merge(x2, g1, sc1, sh1, y_pool.reshape(m, POOL_WIDTH), y_sb.reshape(m, SB_WIDTH),
                    y_gla.reshape(m, GLA_WIDTH), w_gates, w_br_pool.astype(BF16), w_br_sb.astype(BF16),
                    w_br_gla.astype(BF16), seq=seq, tm=_tile(seq, 512), tn=512)
    x2 = _resproj(merged, w_out.astype(BF16), x2, ga1, seq=seq, tm=tm, tn=512)
    x2 = _swiglu(x2, g2, sc2, sh2, ga2, w_ff_gate.astype(BF16), w_ff_up.astype(BF16),
                 w_ff_down.astype(BF16), seq=seq, tm=_tile(seq, 512), tf=512)
    return x2


def kernel(x, c, w_ada, b_ada, g_norm1, w_in, w_pool, pool_scale, sb_q_gain, sb_k_gain, gla_w_a2, gla_b_a2, gla_out_gain, w_br_pool, w_br_sb, w_br_gla, w_out, g_norm2, w_ff_gate, w_ff_up, w_ff_down):
    batch, seq, d = x.shape
    depth = w_ada.shape[0]
    c_pad = jnp.pad(c, ((0, 8 - batch % 8 if batch % 8 else 0), (0, 0)))
    mod = _ada(c_pad, w_ada, b_ada)[:, :batch]
    x2 = x.reshape(batch * seq, d)
    for l in range(depth):
        x2 = _layer(x2, mod[l], batch, seq, g_norm1[l], w_in[l], w_pool[l], pool_scale[l],
                    sb_q_gain[l], sb_k_gain[l], gla_w_a2[l], gla_b_a2[l], gla_out_gain[l],
                    w_br_pool[l], w_br_sb[l], w_br_gla[l], w_out[l], g_norm2[l],
                    w_ff_gate[l], w_ff_up[l], w_ff_down[l])
    return x2.reshape(batch, seq, d)
```

```python
import functools
import math

import numpy as np
import jax
import jax.numpy as jnp
from jax import lax
from jax.experimental import pallas as pl
from jax.experimental.pallas import tpu as pltpu

F32 = jnp.float32
BF16 = jnp.bfloat16

D_MODEL = 2048
POOL_WINDOWS = (2, 4, 8, 16)
POOL_GROUPS = 4
GROUP_DIM = 128
POOL_WIDTH = POOL_GROUPS * GROUP_DIM
SB_HEADS = 8
SB_WIDTH = SB_HEADS * GROUP_DIM
GLA_HEADS = 4
GLA_WIDTH = GLA_HEADS * GROUP_DIM
GLA_RANK = 16
GLA_TAU = 16.0
GLA_CHUNK = 128
GLA_LEVELS = 7
N_BRANCH = 3
D_FF = 5632
RMS_EPS = 1e-6
LOG2E = 1.4426950408889634
POOL_HALO = 16
SB_DEAD_BITS = 160.0
SB_MASKED_BITS = 1.0e4
SB_STATIC_PAST = 2

VMEM_LIMIT = 56 * 1024 * 1024


def _cparams(sem):
    return pltpu.CompilerParams(dimension_semantics=sem, vmem_limit_bytes=VMEM_LIMIT)


def _nt_dot(a, b):
    return lax.dot_general(a, b, (((1,), (1,)), ((), ())), preferred_element_type=F32)


def _silu(x):
    return x * jax.nn.sigmoid(x)


def _ada_kernel(c_ref, w_ref, b_ref, o_ref):
    c = c_ref[...]
    a = _silu(c).astype(BF16)
    o_ref[0] = jnp.dot(a, w_ref[0].astype(BF16), preferred_element_type=F32) + b_ref[0]


def _ada(c_pad, w_ada, b_ada, tn=1024):
    depth, d, n = w_ada.shape
    rows = c_pad.shape[0]
    return pl.pallas_call(
        _ada_kernel,
        grid=(depth, n // tn),
        in_specs=[
            pl.BlockSpec((rows, d), lambda l, j: (0, 0)),
            pl.BlockSpec((1, d, tn), lambda l, j: (l, 0, j)),
            pl.BlockSpec((1, 1, tn), lambda l, j: (l, 0, j)),
        ],
        out_specs=pl.BlockSpec((1, rows, tn), lambda l, j: (l, 0, j)),
        out_shape=jax.ShapeDtypeStruct((depth, rows, n), F32),
        compiler_params=_cparams(("arbitrary", "arbitrary")),
        name="ada_modulation",
    )(c_pad, w_ada, b_ada.reshape(depth, 1, n))


def _modulated_norm(x, gain, scale, shift):
    ms = jnp.mean(x * x, axis=-1, keepdims=True)
    y = x * lax.rsqrt(ms + RMS_EPS) * gain
    return y * (1.0 + scale) + shift


def _group_rmsnorm_store(acc, colgain_ref, o_ref, tn):
    for c in range(tn // GROUP_DIM):
        sl = slice(c * GROUP_DIM, (c + 1) * GROUP_DIM)
        blk = acc[:, sl]
        ms = jnp.mean(blk * blk, axis=-1, keepdims=True)
        o_ref[:, sl] = (blk * lax.rsqrt(ms + RMS_EPS) * colgain_ref[:, sl]).astype(o_ref.dtype)


def _norm_kernel(x_ref, g_ref, sc_ref, sh_ref, h_ref):
    h_ref[...] = _modulated_norm(x_ref[...], g_ref[...], sc_ref[0], sh_ref[0]).astype(BF16)


def _norm(x2, gain, scale, shift, *, seq, tm):
    m, d = x2.shape
    tpb = seq // tm
    mod = pl.BlockSpec((1, 1, d), lambda i: (i // tpb, 0, 0))
    return pl.pallas_call(
        _norm_kernel,
        grid=(m // tm,),
        in_specs=[pl.BlockSpec((tm, d), lambda i: (i, 0)), pl.BlockSpec((1, d), lambda i: (0, 0)), mod, mod],
        out_specs=pl.BlockSpec((tm, d), lambda i: (i, 0)),
        out_shape=jax.ShapeDtypeStruct((m, d), BF16),
        compiler_params=_cparams(("arbitrary",)),
        name="modulated_norm",
    )(x2, gain, scale, shift)


def _proj_kernel(h_ref, w_ref, cg_ref, o_ref, *, n_norm_tiles, tn):
    j = pl.program_id(1)
    acc = jnp.dot(h_ref[...], w_ref[...], preferred_element_type=F32)
    if n_norm_tiles == 0:
        o_ref[...] = acc.astype(o_ref.dtype)
    else:
        @pl.when(j < n_norm_tiles)
        def _():
            _group_rmsnorm_store(acc, cg_ref, o_ref, tn)

        @pl.when(j >= n_norm_tiles)
        def _():
            o_ref[...] = acc.astype(o_ref.dtype)


def _proj(h, w, colgain, *, tm, tn, n_norm_tiles, out_dtype):
    m, d = h.shape
    n = w.shape[1]
    kern = functools.partial(_proj_kernel, n_norm_tiles=n_norm_tiles, tn=tn)
    return pl.pallas_call(
        kern,
        grid=(m // tm, n // tn),
        in_specs=[
            pl.BlockSpec((tm, d), lambda i, j: (i, 0)),
            pl.BlockSpec((d, tn), lambda i, j: (0, j)),
            pl.BlockSpec((1, tn), lambda i, j: (0, j)),
        ],
        out_specs=pl.BlockSpec((tm, tn), lambda i, j: (i, j)),
        out_shape=jax.ShapeDtypeStruct((m, n), out_dtype),
        compiler_params=_cparams(("arbitrary", "arbitrary")),
        name="in_proj",
    )(h, w, colgain)


def _pool_kernel(u_ref, w_ref, ps_ref, o_ref, ext_ref, *, ts):
    i = pl.program_id(1)

    @pl.when(i == 0)
    def _():
        ext_ref[0:POOL_HALO, :] = jnp.zeros((POOL_HALO, POOL_WIDTH), F32)

    @pl.when(i > 0)
    def _():
        ext_ref[0:POOL_HALO, :] = ext_ref[ts:ts + POOL_HALO, :]

    ext_ref[POOL_HALO:, :] = u_ref[0]
    pos1 = (i * ts + 1 + lax.broadcasted_iota(jnp.int32, (ts, GROUP_DIM), 0)).astype(F32)
    for g, w in enumerate(POOL_WINDOWS):
        cols = slice(g * GROUP_DIM, (g + 1) * GROUP_DIM)
        u = ext_ref[POOL_HALO:, cols]
        win = u
        for k in range(1, w):
            win = win + ext_ref[POOL_HALO - k:POOL_HALO - k + ts, cols]
        pooled = win / jnp.minimum(pos1, float(w)) - u
        y = jnp.dot(pooled.astype(BF16), w_ref[g], preferred_element_type=F32) * ps_ref[g]
        o_ref[0, :, cols] = y.astype(o_ref.dtype)


def _pool(proj3, w_pool, pool_scale, ts):
    b, s, _ = proj3.shape
    kern = functools.partial(_pool_kernel, ts=ts)
    return pl.pallas_call(
        kern,
        grid=(b, s // ts),
        in_specs=[
            pl.BlockSpec((1, ts, POOL_WIDTH), lambda bi, i: (bi, i, 0)),
            pl.BlockSpec((POOL_GROUPS, GROUP_DIM, GROUP_DIM), lambda bi, i: (0, 0, 0)),
            pl.BlockSpec((POOL_GROUPS, 1, GROUP_DIM), lambda bi, i: (0, 0, 0)),
        ],
        out_specs=pl.BlockSpec((1, ts, POOL_WIDTH), lambda bi, i: (bi, i, 0)),
        out_shape=jax.ShapeDtypeStruct((b, s, POOL_WIDTH), BF16),
        scratch_shapes=[pltpu.VMEM((ts + POOL_HALO, POOL_WIDTH), F32)],
        compiler_params=_cparams(("arbitrary", "arbitrary")),
        name="pool_mixer",
    )(proj3, w_pool, pool_scale.reshape(POOL_GROUPS, 1, GROUP_DIM))


def _sb_cumsum_matrix(tk):
    j = np.arange(tk)[:, None]
    s = np.arange(tk)[None, :]
    one = np.concatenate([(j >= s).astype(np.float32), np.ones((tk, tk), np.float32)], axis=1)
    return jnp.asarray(np.concatenate([one, one], axis=0), dtype=BF16)


def _sb_kernel(q_ref, k_ref, v_ref, w2_ref, o_ref, acc_ref, run_ref, *, tq, tk):
    i = pl.program_id(2)
    w2 = w2_ref[...]
    acc_ref[...] = jnp.zeros((tq, GROUP_DIM), F32)
    run_ref[...] = jnp.zeros((tq, tk), F32)

    def scores(jb, r0, masked):
        start = pl.multiple_of(jb * tk, tk)
        z = _nt_dot(q_ref[0, r0:tq, :], k_ref[0, pl.ds(start, tk), :])
        e = jnp.exp2(-jnp.abs(z))
        neg_log_keep = jnp.maximum(z, 0.0) + jnp.log(1.0 + e) * LOG2E
        if masked:
            row = lax.broadcasted_iota(jnp.int32, z.shape, 0)
            col = lax.broadcasted_iota(jnp.int32, z.shape, 1)
            mask = (jb * tk + col) < (i * tq + r0 + row)
            neg_log_keep = jnp.where(mask, neg_log_keep, 0.0)
            z = jnp.where(mask, z, -SB_MASKED_BITS)
        hi = neg_log_keep.astype(BF16)
        lo = (neg_log_keep - hi.astype(F32)).astype(BF16)
        return z, jnp.dot(jnp.concatenate([hi, lo], axis=1), w2, preferred_element_type=F32)

    def accumulate(jb, r0, z, c, penalty):
        start = pl.multiple_of(jb * tk, tk)
        run = run_ref[r0:tq, :]
        if penalty is not None:
            run = run + penalty
        a = jnp.exp2(z - c[:, :tk] - run)
        acc_ref[r0:tq, :] += jnp.dot(a.astype(BF16), v_ref[0, pl.ds(start, tk), :], preferred_element_type=F32)
        run_ref[r0:tq, :] = run + c[:, tk:]

    n_diag = tq // tk
    steps = [(i * n_diag + d, d * tk, True, None) for d in reversed(range(n_diag))]
    for p in range(SB_STATIC_PAST):
        jb = i * n_diag - 1 - p
        steps.append((jnp.maximum(jb, 0), 0, False, jnp.where(jb >= 0, 0.0, SB_MASKED_BITS)))
    staged = [scores(jb, r0, masked) for jb, r0, masked, _ in steps]
    for (jb, r0, _, penalty), (z, c) in zip(steps, staged):
        accumulate(jb, r0, z, c, penalty)

    def cond(carry):
        jb, min_run = carry
        return jnp.logical_and(jb >= 0, min_run < SB_DEAD_BITS)

    def body(carry):
        jb, _ = carry
        z, c = scores(jb, 0, False)
        accumulate(jb, 0, z, c, None)
        return jb - 1, jnp.min(run_ref[...])

    lax.while_loop(cond, body, (i * n_diag - 1 - SB_STATIC_PAST, jnp.min(run_ref[...])))
    o_ref[0] = acc_ref[...].astype(o_ref.dtype)


def _stick_breaking(qkv, tq, tk):
    b, s, _ = qkv.shape
    kern = functools.partial(_sb_kernel, tq=tq, tk=tk)
    return pl.pallas_call(
        kern,
        grid=(b, SB_HEADS, s // tq),
        in_specs=[
            pl.BlockSpec((1, tq, GROUP_DIM), lambda bi, h, i: (bi, i, h)),
            pl.BlockSpec((1, s, GROUP_DIM), lambda bi, h, i: (bi, 0, SB_HEADS + h)),
            pl.BlockSpec((1, s, GROUP_DIM), lambda bi, h, i: (bi, 0, 2 * SB_HEADS + h)),
            pl.BlockSpec((2 * tk, 2 * tk), lambda bi, h, i: (0, 0)),
        ],
        out_specs=pl.BlockSpec((1, tq, GROUP_DIM), lambda bi, h, i: (bi, i, h)),
        out_shape=jax.ShapeDtypeStruct((b, s, SB_WIDTH), BF16),
        scratch_shapes=[pltpu.VMEM((tq, GROUP_DIM), F32), pltpu.VMEM((tq, tk), F32)],
        compiler_params=_cparams(("arbitrary", "arbitrary", "arbitrary")),
        name="stick_breaking",
    )(qkv, qkv, qkv, _sb_cumsum_matrix(tk))


def _gla_constants():
    c = GLA_CHUNK
    t = np.arange(c)[:, None]
    j = np.arange(c)[None, :]
    dst = np.concatenate([j <= t, np.ones((8, c), bool)], axis=0).astype(np.float32)
    s = j
    masks = []
    for l in range(GLA_LEVELS):
        masks.append(((t ^ s) >> l == 1) & (t > s))
    masks.append(t == s)
    return jnp.asarray(dst, dtype=BF16), jnp.asarray(np.stack(masks).astype(np.float32))


def _split3(x):
    x1 = x.astype(BF16)
    r = x - x1.astype(F32)
    x2 = r.astype(BF16)
    x3 = (r - x2.astype(F32)).astype(BF16)
    return x1, x2, x3


def _gla_kernel(q_ref, k_ref, v_ref, r_ref, a_ref, wa_ref, ba_ref, og_ref, dst_ref, msk_ref,
                o_ref, st_ref):
    c = GLA_CHUNK

    @pl.when(pl.program_id(1) == 0)
    def _():
        st_ref[...] = jnp.zeros(st_ref.shape, F32)

    x = jnp.dot(a_ref[0].astype(BF16), wa_ref[...], preferred_element_type=F32) + ba_ref[...]
    log_sig = jnp.minimum(x, 0.0) - jnp.log(1.0 + jnp.exp(-jnp.abs(x)))
    g = log_sig * (LOG2E / GLA_TAU)
    dst = dst_ref[...]
    g1, g2, g3 = _split3(g)
    sums = (jnp.dot(dst, g1, preferred_element_type=F32)
            + jnp.dot(dst, g2, preferred_element_type=F32)
            + jnp.dot(dst, g3, preferred_element_type=F32))
    prefix = sums[:c]
    total = sums[c:c + 1]
    e_in = prefix
    e_out = total - prefix

    row = lax.broadcasted_iota(jnp.int32, g.shape, 0)
    pos = row & 3
    g_prev = pltpu.roll(g, 1, 0)
    g_next = pltpu.roll(g, c - 1, 0)
    e_lvl = [jnp.where((row & 1) == 1, g, 0.0),
             jnp.where(pos == 0, g_next, jnp.where(pos == 1, 0.0, jnp.where(pos == 2, g, g_prev + g)))]
    for l in range(2, GLA_LEVELS):
        m = 1 << l
        blocks = prefix.reshape(c // (2 * m), 2 * m, GLA_WIDTH)
        e_lvl.append((-jnp.abs(blocks - blocks[:, m - 1:m, :])).reshape(c, GLA_WIDTH))

    for h in range(GLA_HEADS):
        cols = slice(h * GROUP_DIM, (h + 1) * GROUP_DIM)
        q = q_ref[0, :, cols] * (GROUP_DIM ** -0.5)
        k = k_ref[0, :, cols]
        v = v_ref[0, :, cols].astype(BF16)
        scores = msk_ref[GLA_LEVELS] * _nt_dot(q.astype(BF16), k.astype(BF16))
        for l in range(GLA_LEVELS):
            xl = jnp.exp2(e_lvl[l][:, cols])
            scores = scores + msk_ref[l] * _nt_dot((q * xl).astype(BF16), (k * xl).astype(BF16))
        o = jnp.dot(scores.astype(BF16), v, preferred_element_type=F32)
        x_in = jnp.exp2(e_in[:, cols])
        st = st_ref[h]
        o = o + _nt_dot((q * x_in).astype(BF16), st.astype(BF16))
        x_out = jnp.exp2(e_out[:, cols])
        x_all = jnp.exp2(total[:, cols])
        vt = v_ref[0, :, cols].T.astype(BF16)
        st_ref[h] = st * x_all + jnp.dot(vt, (k * x_out).astype(BF16), preferred_element_type=F32)
        ms = jnp.mean(o * o, axis=-1, keepdims=True)
        o = o * lax.rsqrt(ms + RMS_EPS) * og_ref[...]
        o_ref[0, :, cols] = (o * _silu(r_ref[0, :, cols])).astype(o_ref.dtype)


def _gla(proj3, w_a2p, b_a2, out_gain):
    b, s, _ = proj3.shape
    c = GLA_CHUNK
    dst, masks = _gla_constants()
    wide = lambda blk: pl.BlockSpec((1, c, GLA_WIDTH), lambda bi, ci: (bi, ci, blk))
    const2 = lambda shape: pl.BlockSpec(shape, lambda bi, ci: (0, 0))
    a_blk = (POOL_WIDTH + 4 * GLA_WIDTH) // GROUP_DIM
    return pl.pallas_call(
        _gla_kernel,
        grid=(b, s // c),
        in_specs=[
            wide(1), wide(2), wide(3), wide(4),
            pl.BlockSpec((1, c, GROUP_DIM), lambda bi, ci: (bi, ci, a_blk)),
            const2((GROUP_DIM, GLA_WIDTH)),
            const2((1, GLA_WIDTH)),
            const2((1, GROUP_DIM)),
            const2(dst.shape),
            pl.BlockSpec(masks.shape, lambda bi, ci: (0, 0, 0)),
        ],
        out_specs=pl.BlockSpec((1, c, GLA_WIDTH), lambda bi, ci: (bi, ci, 0)),
        out_shape=jax.ShapeDtypeStruct((b, s, GLA_WIDTH), BF16),
        scratch_shapes=[pltpu.VMEM((GLA_HEADS, GROUP_DIM, GROUP_DIM), F32)],
        compiler_params=_cparams(("arbitrary", "arbitrary")),
        name="gla",
    )(proj3, proj3, proj3, proj3, proj3, w_a2p, b_a2, out_gain, dst, masks)


def _merge_kernel(h_ref, yp_ref, ys_ref, yg_ref, wgp_ref, wgs_ref, wgg_ref, wp_ref, ws_ref, wg_ref, o_ref):
    h = h_ref[...]

    def branch(wgate_ref, y_ref, w_ref):
        gate = jax.nn.sigmoid(jnp.dot(h, wgate_ref[...], preferred_element_type=F32))
        return gate * jnp.dot(y_ref[...], w_ref[...], preferred_element_type=F32)

    merged = branch(wgp_ref, yp_ref, wp_ref) + branch(wgs_ref, ys_ref, ws_ref) + branch(wgg_ref, yg_ref, wg_ref)
    o_ref[...] = merged.astype(o_ref.dtype)


def _merge(h, y_pool, y_sb, y_gla, w_gates, w_br_pool, w_br_sb, w_br_gla, *, tm, tn):
    m, d = h.shape
    nj = d // tn
    row = lambda width: pl.BlockSpec((tm, width), lambda i, j: (i, 0))
    gate_w = lambda br: pl.BlockSpec((d, tn), lambda i, j: (0, br * nj + j))
    br_w = lambda width: pl.BlockSpec((width, tn), lambda i, j: (0, j))
    return pl.pallas_call(
        _merge_kernel,
        grid=(m // tm, nj),
        in_specs=[
            row(d), row(POOL_WIDTH), row(SB_WIDTH), row(GLA_WIDTH),
            gate_w(0), gate_w(1), gate_w(2),
            br_w(POOL_WIDTH), br_w(SB_WIDTH), br_w(GLA_WIDTH),
        ],
        out_specs=pl.BlockSpec((tm, tn), lambda i, j: (i, j)),
        out_shape=jax.ShapeDtypeStruct((m, d), BF16),
        compiler_params=_cparams(("arbitrary", "arbitrary")),
        name="gated_merge",
    )(h, y_pool, y_sb, y_gla, w_gates, w_gates, w_gates, w_br_pool, w_br_sb, w_br_gla)


def _resproj_kernel(a_ref, w_ref, x_ref, ga_ref, g_ref, sc_ref, sh_ref, o_ref, h_ref):
    y = jnp.dot(a_ref[...], w_ref[...], preferred_element_type=F32)
    x_new = x_ref[...] + ga_ref[0] * y
    o_ref[...] = x_new
    h_ref[...] = _modulated_norm(x_new, g_ref[...], sc_ref[0], sh_ref[0]).astype(BF16)


def _resproj(a, w, x2, gate, gain, scale, shift, *, seq, tm):
    m, kdim = a.shape
    d = w.shape[1]
    tpb = seq // tm
    row = lambda width: pl.BlockSpec((tm, width), lambda i: (i, 0))
    mod = pl.BlockSpec((1, 1, d), lambda i: (i // tpb, 0, 0))
    return pl.pallas_call(
        _resproj_kernel,
        grid=(m // tm,),
        in_specs=[row(kdim), pl.BlockSpec((kdim, d), lambda i: (0, 0)), row(d), mod,
                  pl.BlockSpec((1, d), lambda i: (0, 0)), mod, mod],
        out_specs=[row(d), row(d)],
        out_shape=[jax.ShapeDtypeStruct((m, d), F32), jax.ShapeDtypeStruct((m, d), BF16)],
        compiler_params=_cparams(("arbitrary",)),
        name="residual_proj",
    )(a, w, x2, gate, gain, scale, shift)


def _swiglu_kernel(*refs, emit_next):
    if emit_next:
        h_ref, x_ref, ga_ref, wg_ref, wu_ref, wd_ref, g_ref, sc_ref, sh_ref, o_ref, hn_ref, acc_ref = refs
    else:
        h_ref, x_ref, ga_ref, wg_ref, wu_ref, wd_ref, o_ref, acc_ref = refs
    f = pl.program_id(1)

    @pl.when(f == 0)
    def _():
        acc_ref[...] = jnp.zeros(acc_ref.shape, F32)

    h = h_ref[...]
    gate = jnp.dot(h, wg_ref[...], preferred_element_type=F32)
    up = jnp.dot(h, wu_ref[...], preferred_element_type=F32)
    act = (_silu(gate) * up).astype(BF16)
    acc_ref[...] += jnp.dot(act, wd_ref[...], preferred_element_type=F32)

    @pl.when(f == pl.num_programs(1) - 1)
    def _():
        x_new = x_ref[...] + ga_ref[0] * acc_ref[...]
        o_ref[...] = x_new
        if emit_next:
            hn_ref[...] = _modulated_norm(x_new, g_ref[...], sc_ref[0], sh_ref[0]).astype(BF16)


def _swiglu(h, x2, gate, w_gate, w_up, w_down, next_norm, *, seq, tm, tf):
    m, d = x2.shape
    dff = w_gate.shape[1]
    tpb = seq // tm
    emit_next = next_norm is not None
    row = pl.BlockSpec((tm, d), lambda i, f: (i, 0))
    mod = pl.BlockSpec((1, 1, d), lambda i, f: (i // tpb, 0, 0))
    in_specs = [row, row, mod,
                pl.BlockSpec((d, tf), lambda i, f: (0, f)),
                pl.BlockSpec((d, tf), lambda i, f: (0, f)),
                pl.BlockSpec((tf, d), lambda i, f: (f, 0))]
    args = [h, x2, gate, w_gate, w_up, w_down]
    out_specs = [row]
    out_shape = [jax.ShapeDtypeStruct((m, d), F32)]
    if emit_next:
        in_specs += [pl.BlockSpec((1, d), lambda i, f: (0, 0)), mod, mod]
        args += list(next_norm)
        out_specs.append(row)
        out_shape.append(jax.ShapeDtypeStruct((m, d), BF16))
    outs = pl.pallas_call(
        functools.partial(_swiglu_kernel, emit_next=emit_next),
        grid=(m // tm, dff // tf),
        in_specs=in_specs,
        out_specs=out_specs,
        out_shape=out_shape,
        scratch_shapes=[pltpu.VMEM((tm, d), F32)],
        compiler_params=_cparams(("arbitrary", "arbitrary")),
        name="swiglu",
    )(*args)
    return (outs[0], outs[1]) if emit_next else (outs[0], None)


def _tile(n, pref):
    t = min(n, pref)
    assert n % t == 0, (n, t)
    return t


def _mod_vectors(mod_l):
    d = D_MODEL
    return [mod_l[:, None, k * d:(k + 1) * d] for k in range(6)]


def _layer(x2, h1, mod_l, next_norm, batch, seq, w_in, w_pool, pool_scale, sb_q_gain, sb_k_gain,
           gla_w_a2, gla_b_a2, gla_out_gain, w_br_pool, w_br_sb, w_br_gla, w_out,
           g_norm2, w_ff_gate, w_ff_up, w_ff_down):
    d = D_MODEL
    _, _, ga1, sh2, sc2, ga2 = _mod_vectors(mod_l)

    o_sb = POOL_WIDTH
    o_gla = o_sb + 3 * SB_WIDTH
    o_a = o_gla + 4 * GLA_WIDTH
    o_gate = o_a + GLA_RANK
    a_pad = GROUP_DIM - GLA_RANK
    w_f32cols = jnp.concatenate(
        [w_in[:, :o_sb], w_in[:, o_gla:o_a], jnp.pad(w_in[:, o_a:o_gate], ((0, 0), (0, a_pad)))],
        axis=1).astype(BF16)
    w_sb = w_in[:, o_sb:o_gla].astype(BF16)
    w_gates = w_in[:, o_gate:].astype(BF16)

    tm = _tile(seq, 1024)
    ones_cols = jnp.ones((1, w_f32cols.shape[1]), F32)
    proj = _proj(h1, w_f32cols, ones_cols, tm=tm, tn=896, n_norm_tiles=0, out_dtype=F32)
    q_scale = LOG2E / math.sqrt(GROUP_DIM)
    colgain = jnp.concatenate([jnp.tile(sb_q_gain * q_scale, SB_HEADS), jnp.tile(sb_k_gain, SB_HEADS),
                               jnp.ones((SB_WIDTH,), F32)]).reshape(1, 3 * SB_WIDTH)
    tn_sb = 1024
    qkv = _proj(h1, w_sb, colgain, tm=tm, tn=tn_sb, n_norm_tiles=2 * SB_WIDTH // tn_sb, out_dtype=BF16)

    proj3 = proj.reshape(batch, seq, proj.shape[1])
    y_pool = _pool(proj3, w_pool.astype(BF16), pool_scale, ts=_tile(seq, 512))
    y_sb = _stick_breaking(qkv.reshape(batch, seq, 3 * SB_WIDTH), tq=_tile(seq, 512), tk=128)
    w_a2p = jnp.pad(gla_w_a2, ((0, a_pad), (0, 0))).astype(BF16)
    y_gla = _gla(proj3, w_a2p, gla_b_a2.reshape(1, GLA_WIDTH), gla_out_gain.reshape(1, GROUP_DIM))

    m = batch * seq
    merged = _merge(h1, y_pool.reshape(m, POOL_WIDTH), y_sb.reshape(m, SB_WIDTH), y_gla.reshape(m, GLA_WIDTH),
                    w_gates, w_br_pool.astype(BF16), w_br_sb.astype(BF16), w_br_gla.astype(BF16),
                    tm=tm, tn=512)
    tm_half = _tile(seq, 512)
    x2, h2 = _resproj(merged, w_out.astype(BF16), x2, ga1, g_norm2.reshape(1, d), sc2, sh2, seq=seq, tm=tm_half)
    return _swiglu(h2, x2, ga2, w_ff_gate.astype(BF16), w_ff_up.astype(BF16), w_ff_down.astype(BF16),
                   next_norm, seq=seq, tm=tm_half, tf=512)


def kernel(x, c, w_ada, b_ada, g_norm1, w_in, w_pool, pool_scale, sb_q_gain, sb_k_gain, gla_w_a2, gla_b_a2, gla_out_gain, w_br_pool, w_br_sb, w_br_gla, w_out, g_norm2, w_ff_gate, w_ff_up, w_ff_down):
    batch, seq, d = x.shape
    depth = w_ada.shape[0]
    c_pad = jnp.pad(c, ((0, 8 - batch % 8 if batch % 8 else 0), (0, 0)))
    mod = _ada(c_pad, w_ada, b_ada)[:, :batch]
    x2 = x.reshape(batch * seq, d)

    def first_norm(l):
        sh1, sc1 = _mod_vectors(mod[l])[:2]
        return g_norm1[l].reshape(1, d), sc1, sh1

    h1 = _norm(x2, *first_norm(0), seq=seq, tm=_tile(seq, 512))
    for l in range(depth):
        next_norm = first_norm(l + 1) if l + 1 < depth else None
        x2, h1 = _layer(x2, h1, mod[l], next_norm, batch, seq, w_in[l], w_pool[l], pool_scale[l],
                        sb_q_gain[l], sb_k_gain[l], gla_w_a2[l], gla_b_a2[l], gla_out_gain[l],
                        w_br_pool[l], w_br_sb[l], w_br_gla[l], w_out[l], g_norm2[l],
                        w_ff_gate[l], w_ff_up[l], w_ff_down[l])
    return x2.reshape(batch, seq, d)
```

```python
import functools
import math

import numpy as np
import jax
import jax.numpy as jnp
from jax import lax
from jax.experimental import pallas as pl
from jax.experimental.pallas import tpu as pltpu

F32 = jnp.float32
BF16 = jnp.bfloat16

D_MODEL = 2048
POOL_WINDOWS = (2, 4, 8, 16)
POOL_GROUPS = 4
GROUP_DIM = 128
POOL_WIDTH = POOL_GROUPS * GROUP_DIM
SB_HEADS = 8
SB_WIDTH = SB_HEADS * GROUP_DIM
GLA_HEADS = 4
GLA_WIDTH = GLA_HEADS * GROUP_DIM
GLA_RANK = 16
GLA_TAU = 16.0
GLA_CHUNK = 128
GLA_LEVELS = 7
N_BRANCH = 3
D_FF = 5632
RMS_EPS = 1e-6
LOG2E = 1.4426950408889634
POOL_HALO = 16
SB_DEAD_BITS = 160.0
SB_MASKED_BITS = 1.0e4
SB_STATIC_LAGS = 3

VMEM_LIMIT = 56 * 1024 * 1024
MXU_WIDTH = 256
ROW_CHUNK = 256


def _cparams(sem):
    return pltpu.CompilerParams(dimension_semantics=sem, vmem_limit_bytes=VMEM_LIMIT)


def _nt_dot(a, b):
    return lax.dot_general(a, b, (((1,), (1,)), ((), ())), preferred_element_type=F32)


def _silu(x):
    return x * jax.nn.sigmoid(x)


def _ada_kernel(c_ref, w_ref, b_ref, o_ref):
    c = c_ref[...]
    a = _silu(c).astype(BF16)
    o_ref[0] = jnp.dot(a, w_ref[0].astype(BF16), preferred_element_type=F32) + b_ref[0]


def _ada(c_pad, w_ada, b_ada, tn=1024):
    depth, d, n = w_ada.shape
    rows = c_pad.shape[0]
    return pl.pallas_call(
        _ada_kernel,
        grid=(depth, n // tn),
        in_specs=[
            pl.BlockSpec((rows, d), lambda l, j: (0, 0)),
            pl.BlockSpec((1, d, tn), lambda l, j: (l, 0, j)),
            pl.BlockSpec((1, 1, tn), lambda l, j: (l, 0, j)),
        ],
        out_specs=pl.BlockSpec((1, rows, tn), lambda l, j: (l, 0, j)),
        out_shape=jax.ShapeDtypeStruct((depth, rows, n), F32),
        compiler_params=_cparams(("arbitrary", "arbitrary")),
        name="ada_modulation",
    )(c_pad, w_ada, b_ada.reshape(depth, 1, n))


def _modulated_norm(x, gain, scale, shift):
    ms = jnp.mean(x * x, axis=-1, keepdims=True)
    y = x * lax.rsqrt(ms + RMS_EPS) * gain
    return y * (1.0 + scale) + shift


def _group_rmsnorm(acc, colgain):
    out = []
    for c in range(acc.shape[1] // GROUP_DIM):
        sl = slice(c * GROUP_DIM, (c + 1) * GROUP_DIM)
        blk = acc[:, sl]
        ms = jnp.mean(blk * blk, axis=-1, keepdims=True)
        out.append(blk * lax.rsqrt(ms + RMS_EPS) * colgain[:, sl])
    return jnp.concatenate(out, axis=1)


def _norm_kernel(x_ref, g_ref, sc_ref, sh_ref, h_ref):
    h_ref[...] = _modulated_norm(x_ref[...], g_ref[...], sc_ref[0], sh_ref[0]).astype(BF16)


def _norm(x2, gain, scale, shift, *, seq, tm):
    m, d = x2.shape
    tpb = seq // tm
    mod = pl.BlockSpec((1, 1, d), lambda i: (i // tpb, 0, 0))
    return pl.pallas_call(
        _norm_kernel,
        grid=(m // tm,),
        in_specs=[pl.BlockSpec((tm, d), lambda i: (i, 0)), pl.BlockSpec((1, d), lambda i: (0, 0)), mod, mod],
        out_specs=pl.BlockSpec((tm, d), lambda i: (i, 0)),
        out_shape=jax.ShapeDtypeStruct((m, d), BF16),
        compiler_params=_cparams(("arbitrary",)),
        name="modulated_norm",
    )(x2, gain, scale, shift)


def _proj_kernel(h_ref, w_ref, cg_ref, o_ref, *, n_norm_tiles, tn):
    j = pl.program_id(1)
    h = h_ref[...]

    def tile(normed):
        for c in range(tn // MXU_WIDTH):
            cols = slice(c * MXU_WIDTH, (c + 1) * MXU_WIDTH)
            acc = jnp.dot(h, w_ref[:, cols], preferred_element_type=F32)
            if normed:
                acc = _group_rmsnorm(acc, cg_ref[:, cols])
            o_ref[:, cols] = acc.astype(o_ref.dtype)

    if n_norm_tiles == 0:
        tile(False)
    else:
        pl.when(j < n_norm_tiles)(lambda: tile(True))
        pl.when(j >= n_norm_tiles)(lambda: tile(False))


def _proj(h, w, colgain, *, tm, tn, n_norm_tiles, out_dtype):
    m, d = h.shape
    n = w.shape[1]
    kern = functools.partial(_proj_kernel, n_norm_tiles=n_norm_tiles, tn=tn)
    return pl.pallas_call(
        kern,
        grid=(m // tm, n // tn),
        in_specs=[
            pl.BlockSpec((tm, d), lambda i, j: (i, 0)),
            pl.BlockSpec((d, tn), lambda i, j: (0, j)),
            pl.BlockSpec((1, tn), lambda i, j: (0, j)),
        ],
        out_specs=pl.BlockSpec((tm, tn), lambda i, j: (i, j)),
        out_shape=jax.ShapeDtypeStruct((m, n), out_dtype),
        compiler_params=_cparams(("arbitrary", "arbitrary")),
        name="in_proj",
    )(h, w, colgain)


def _pool_kernel(u_ref, w_ref, ps_ref, o_ref, ext_ref, *, ts):
    i = pl.program_id(1)

    @pl.when(i == 0)
    def _():
        ext_ref[0:POOL_HALO, :] = jnp.zeros((POOL_HALO, POOL_WIDTH), F32)

    @pl.when(i > 0)
    def _():
        ext_ref[0:POOL_HALO, :] = ext_ref[ts:ts + POOL_HALO, :]

    ext_ref[POOL_HALO:, :] = u_ref[0]
    pos1 = (i * ts + 1 + lax.broadcasted_iota(jnp.int32, (ts, GROUP_DIM), 0)).astype(F32)
    for g, w in enumerate(POOL_WINDOWS):
        cols = slice(g * GROUP_DIM, (g + 1) * GROUP_DIM)
        u = ext_ref[POOL_HALO:, cols]
        win = u
        for k in range(1, w):
            win = win + ext_ref[POOL_HALO - k:POOL_HALO - k + ts, cols]
        pooled = win / jnp.minimum(pos1, float(w)) - u
        y = jnp.dot(pooled.astype(BF16), w_ref[g], preferred_element_type=F32) * ps_ref[g]
        o_ref[0, :, cols] = y.astype(o_ref.dtype)


def _pool(proj3, w_pool, pool_scale, ts):
    b, s, _ = proj3.shape
    kern = functools.partial(_pool_kernel, ts=ts)
    return pl.pallas_call(
        kern,
        grid=(b, s // ts),
        in_specs=[
            pl.BlockSpec((1, ts, POOL_WIDTH), lambda bi, i: (bi, i, 0)),
            pl.BlockSpec((POOL_GROUPS, GROUP_DIM, GROUP_DIM), lambda bi, i: (0, 0, 0)),
            pl.BlockSpec((POOL_GROUPS, 1, GROUP_DIM), lambda bi, i: (0, 0, 0)),
        ],
        out_specs=pl.BlockSpec((1, ts, POOL_WIDTH), lambda bi, i: (bi, i, 0)),
        out_shape=jax.ShapeDtypeStruct((b, s, POOL_WIDTH), BF16),
        scratch_shapes=[pltpu.VMEM((ts + POOL_HALO, POOL_WIDTH), F32)],
        compiler_params=_cparams(("arbitrary", "arbitrary")),
        name="pool_mixer",
    )(proj3, w_pool, pool_scale.reshape(POOL_GROUPS, 1, GROUP_DIM))


def _sb_cumsum_matrix(tk):
    j = np.arange(tk)[:, None]
    s = np.arange(tk)[None, :]
    one = np.concatenate([(j >= s).astype(np.float32), np.ones((tk, tk), np.float32)], axis=1)
    return jnp.asarray(np.concatenate([one, one], axis=0), dtype=BF16)


def _sb_kernel(q_ref, k_ref, v_ref, w2_ref, o_ref, acc_ref, run_ref, *, tq, tk):
    i = pl.program_id(2)
    w2 = w2_ref[...]
    n_sub = tq // tk
    acc_ref[...] = jnp.zeros((tq, GROUP_DIM), F32)
    run_ref[...] = jnp.zeros((tq, tk), F32)
    row = lax.broadcasted_iota(jnp.int32, (tq, tk), 0)
    col = lax.broadcasted_iota(jnp.int32, (tq, tk), 1)
    diag_mask = col < (row & (tk - 1))
    sub = lambda r: slice(r * tk, (r + 1) * tk)

    def key_blocks(lag):
        blocks = []
        for r in range(n_sub):
            jb = i * n_sub + r - lag
            start = pl.multiple_of(jnp.maximum(jb, 0) * tk, tk)
            blocks.append((start, jnp.where(jb >= 0, 0.0, SB_MASKED_BITS)))
        return blocks

    def scores(blocks, diagonal):
        z = jnp.concatenate([_nt_dot(q_ref[0, sub(r), :], k_ref[0, pl.ds(start, tk), :])
                             for r, (start, _) in enumerate(blocks)], axis=0)
        neg_abs = lax.bitcast_convert_type(
            lax.bitcast_convert_type(z, jnp.uint32) | jnp.uint32(0x80000000), F32)
        neg_log_keep = jnp.maximum(z, 0.0) + jnp.log(1.0 + jnp.exp2(neg_abs)) * LOG2E
        if diagonal:
            neg_log_keep = jnp.where(diag_mask, neg_log_keep, 0.0)
            z = jnp.where(diag_mask, z, -SB_MASKED_BITS)
        hi = lax.bitcast_convert_type(
            lax.bitcast_convert_type(neg_log_keep, jnp.uint32) & jnp.uint32(0xFFFF0000), F32)
        lo = neg_log_keep - hi
        c = jnp.dot(jnp.concatenate([hi.astype(BF16), lo.astype(BF16)], axis=1), w2, preferred_element_type=F32)
        return z, c

    def accumulate(blocks, z, c, diagonal):
        run = run_ref[...]
        if not diagonal:
            run = run + jnp.concatenate([jnp.full((tk, tk), pen, F32) for _, pen in blocks], axis=0)
        a = jnp.exp2(z - c[:, :tk] - run).astype(BF16)
        for r, (start, _) in enumerate(blocks):
            acc_ref[sub(r), :] += jnp.dot(a[sub(r), :], v_ref[0, pl.ds(start, tk), :], preferred_element_type=F32)
        run_ref[...] = run + c[:, tk:]

    lags = [key_blocks(lag) for lag in range(SB_STATIC_LAGS)]
    staged = [scores(blocks, lag == 0) for lag, blocks in enumerate(lags)]
    for lag, (blocks, (z, c)) in enumerate(zip(lags, staged)):
        accumulate(blocks, z, c, lag == 0)

    def cond(carry):
        lag, min_run = carry
        return jnp.logical_and(lag <= i * n_sub + n_sub - 1, min_run < SB_DEAD_BITS)

    def body(carry):
        lag, _ = carry
        blocks = key_blocks(lag)
        z, c = scores(blocks, False)
        accumulate(blocks, z, c, False)
        return lag + 1, jnp.min(run_ref[...])

    lax.while_loop(cond, body, (jnp.int32(SB_STATIC_LAGS), jnp.min(run_ref[...])))
    o_ref[0] = acc_ref[...].astype(o_ref.dtype)


def _stick_breaking(qkv, tq, tk):
    b, s, _ = qkv.shape
    kern = functools.partial(_sb_kernel, tq=tq, tk=tk)
    return pl.pallas_call(
        kern,
        grid=(b, SB_HEADS, s // tq),
        in_specs=[
            pl.BlockSpec((1, tq, GROUP_DIM), lambda bi, h, i: (bi, i, h)),
            pl.BlockSpec((1, s, GROUP_DIM), lambda bi, h, i: (bi, 0, SB_HEADS + h)),
            pl.BlockSpec((1, s, GROUP_DIM), lambda bi, h, i: (bi, 0, 2 * SB_HEADS + h)),
            pl.BlockSpec((2 * tk, 2 * tk), lambda bi, h, i: (0, 0)),
        ],
        out_specs=pl.BlockSpec((1, tq, GROUP_DIM), lambda bi, h, i: (bi, i, h)),
        out_shape=jax.ShapeDtypeStruct((b, s, SB_WIDTH), BF16),
        scratch_shapes=[pltpu.VMEM((tq, GROUP_DIM), F32), pltpu.VMEM((tq, tk), F32)],
        compiler_params=_cparams(("arbitrary", "arbitrary", "arbitrary")),
        name="stick_breaking",
    )(qkv, qkv, qkv, _sb_cumsum_matrix(tk))


def _gla_constants():
    c = GLA_CHUNK
    t = np.arange(c)[:, None]
    j = np.arange(c)[None, :]
    dst = np.concatenate([j <= t, np.ones((8, c), bool)], axis=0).astype(np.float32)
    s = j
    masks = []
    for l in range(GLA_LEVELS):
        masks.append(((t ^ s) >> l == 1) & (t > s))
    masks.append(t == s)
    return jnp.asarray(dst, dtype=BF16), jnp.asarray(np.stack(masks).astype(np.float32))


def _split3(x):
    x1 = x.astype(BF16)
    r = x - x1.astype(F32)
    x2 = r.astype(BF16)
    x3 = (r - x2.astype(F32)).astype(BF16)
    return x1, x2, x3


def _gla_kernel(q_ref, k_ref, v_ref, r_ref, a_ref, wa_ref, ba_ref, og_ref, dst_ref, msk_ref,
                o_ref, st_ref):
    c = GLA_CHUNK

    @pl.when(pl.program_id(1) == 0)
    def _():
        st_ref[...] = jnp.zeros(st_ref.shape, F32)

    x = jnp.dot(a_ref[0].astype(BF16), wa_ref[...], preferred_element_type=F32) + ba_ref[...]
    log_sig = jnp.minimum(x, 0.0) - jnp.log(1.0 + jnp.exp(-jnp.abs(x)))
    g = log_sig * (LOG2E / GLA_TAU)
    dst = dst_ref[...]
    g1, g2, g3 = _split3(g)
    sums = (jnp.dot(dst, g1, preferred_element_type=F32)
            + jnp.dot(dst, g2, preferred_element_type=F32)
            + jnp.dot(dst, g3, preferred_element_type=F32))
    prefix = sums[:c]
    total = sums[c:c + 1]
    e_in = prefix
    e_out = total - prefix

    row = lax.broadcasted_iota(jnp.int32, g.shape, 0)
    pos = row & 3
    g_prev = pltpu.roll(g, 1, 0)
    g_next = pltpu.roll(g, c - 1, 0)
    e_lvl = [jnp.where((row & 1) == 1, g, 0.0),
             jnp.where(pos == 0, g_next, jnp.where(pos == 1, 0.0, jnp.where(pos == 2, g, g_prev + g)))]
    for l in range(2, GLA_LEVELS):
        m = 1 << l
        blocks = prefix.reshape(c // (2 * m), 2 * m, GLA_WIDTH)
        e_lvl.append((-jnp.abs(blocks - blocks[:, m - 1:m, :])).reshape(c, GLA_WIDTH))

    for h in range(GLA_HEADS):
        cols = slice(h * GROUP_DIM, (h + 1) * GROUP_DIM)
        q = q_ref[0, :, cols] * (GROUP_DIM ** -0.5)
        k = k_ref[0, :, cols]
        v = v_ref[0, :, cols].astype(BF16)
        scores = msk_ref[GLA_LEVELS] * _nt_dot(q.astype(BF16), k.astype(BF16))
        for l in range(GLA_LEVELS):
            xl = jnp.exp2(e_lvl[l][:, cols])
            scores = scores + msk_ref[l] * _nt_dot((q * xl).astype(BF16), (k * xl).astype(BF16))
        o = jnp.dot(scores.astype(BF16), v, preferred_element_type=F32)
        x_in = jnp.exp2(e_in[:, cols])
        st = st_ref[h]
        o = o + _nt_dot((q * x_in).astype(BF16), st.astype(BF16))
        x_out = jnp.exp2(e_out[:, cols])
        x_all = jnp.exp2(total[:, cols])
        vt = v_ref[0, :, cols].T.astype(BF16)
        st_ref[h] = st * x_all + jnp.dot(vt, (k * x_out).astype(BF16), preferred_element_type=F32)
        ms = jnp.mean(o * o, axis=-1, keepdims=True)
        o = o * lax.rsqrt(ms + RMS_EPS) * og_ref[...]
        o_ref[0, :, cols] = (o * _silu(r_ref[0, :, cols])).astype(o_ref.dtype)


def _gla(proj3, w_a2p, b_a2, out_gain):
    b, s, _ = proj3.shape
    c = GLA_CHUNK
    dst, masks = _gla_constants()
    wide = lambda blk: pl.BlockSpec((1, c, GLA_WIDTH), lambda bi, ci: (bi, ci, blk))
    const2 = lambda shape: pl.BlockSpec(shape, lambda bi, ci: (0, 0))
    a_blk = (POOL_WIDTH + 4 * GLA_WIDTH) // GROUP_DIM
    return pl.pallas_call(
        _gla_kernel,
        grid=(b, s // c),
        in_specs=[
            wide(1), wide(2), wide(3), wide(4),
            pl.BlockSpec((1, c, GROUP_DIM), lambda bi, ci: (bi, ci, a_blk)),
            const2((GROUP_DIM, GLA_WIDTH)),
            const2((1, GLA_WIDTH)),
            const2((1, GROUP_DIM)),
            const2(dst.shape),
            pl.BlockSpec(masks.shape, lambda bi, ci: (0, 0, 0)),
        ],
        out_specs=pl.BlockSpec((1, c, GLA_WIDTH), lambda bi, ci: (bi, ci, 0)),
        out_shape=jax.ShapeDtypeStruct((b, s, GLA_WIDTH), BF16),
        scratch_shapes=[pltpu.VMEM((GLA_HEADS, GROUP_DIM, GROUP_DIM), F32)],
        compiler_params=_cparams(("arbitrary", "arbitrary")),
        name="gla",
    )(proj3, proj3, proj3, proj3, proj3, w_a2p, b_a2, out_gain, dst, masks)


def _merge_kernel(h_ref, yp_ref, ys_ref, yg_ref, wgp_ref, wgs_ref, wgg_ref, wp_ref, ws_ref, wg_ref, o_ref):
    h = h_ref[...]

    def branch(wgate_ref, y_ref, w_ref):
        gate = jax.nn.sigmoid(jnp.dot(h, wgate_ref[...], preferred_element_type=F32))
        return gate * jnp.dot(y_ref[...], w_ref[...], preferred_element_type=F32)

    merged = branch(wgp_ref, yp_ref, wp_ref) + branch(wgs_ref, ys_ref, ws_ref) + branch(wgg_ref, yg_ref, wg_ref)
    o_ref[...] = merged.astype(o_ref.dtype)


def _merge(h, y_pool, y_sb, y_gla, w_gates, w_br_pool, w_br_sb, w_br_gla, *, tm, tn):
    m, d = h.shape
    nj = d // tn
    row = lambda width: pl.BlockSpec((tm, width), lambda i, j: (i, 0))
    gate_w = lambda br: pl.BlockSpec((d, tn), lambda i, j: (0, br * nj + j))
    br_w = lambda width: pl.BlockSpec((width, tn), lambda i, j: (0, j))
    return pl.pallas_call(
        _merge_kernel,
        grid=(m // tm, nj),
        in_specs=[
            row(d), row(POOL_WIDTH), row(SB_WIDTH), row(GLA_WIDTH),
            gate_w(0), gate_w(1), gate_w(2),
            br_w(POOL_WIDTH), br_w(SB_WIDTH), br_w(GLA_WIDTH),
        ],
        out_specs=pl.BlockSpec((tm, tn), lambda i, j: (i, j)),
        out_shape=jax.ShapeDtypeStruct((m, d), BF16),
        compiler_params=_cparams(("arbitrary", "arbitrary")),
        name="gated_merge",
    )(h, y_pool, y_sb, y_gla, w_gates, w_gates, w_gates, w_br_pool, w_br_sb, w_br_gla)


def _resproj_kernel(a_ref, w_ref, x_ref, ga_ref, g_ref, sc_ref, sh_ref, o_ref, h_ref):
    for r in range(a_ref.shape[0] // ROW_CHUNK):
        rows = slice(r * ROW_CHUNK, (r + 1) * ROW_CHUNK)
        y = jnp.dot(a_ref[rows, :], w_ref[...], preferred_element_type=F32)
        x_new = x_ref[rows, :] + ga_ref[0] * y
        o_ref[rows, :] = x_new
        h_ref[rows, :] = _modulated_norm(x_new, g_ref[...], sc_ref[0], sh_ref[0]).astype(BF16)


def _resproj(a, w, x2, gate, gain, scale, shift, *, seq, tm):
    m, kdim = a.shape
    d = w.shape[1]
    tpb = seq // tm
    row = lambda width: pl.BlockSpec((tm, width), lambda i: (i, 0))
    mod = pl.BlockSpec((1, 1, d), lambda i: (i // tpb, 0, 0))
    return pl.pallas_call(
        _resproj_kernel,
        grid=(m // tm,),
        in_specs=[row(kdim), pl.BlockSpec((kdim, d), lambda i: (0, 0)), row(d), mod,
                  pl.BlockSpec((1, d), lambda i: (0, 0)), mod, mod],
        out_specs=[row(d), row(d)],
        out_shape=[jax.ShapeDtypeStruct((m, d), F32), jax.ShapeDtypeStruct((m, d), BF16)],
        compiler_params=_cparams(("arbitrary",)),
        name="residual_proj",
    )(a, w, x2, gate, gain, scale, shift)


def _swiglu_kernel(*refs, emit_next):
    if emit_next:
        h_ref, x_ref, ga_ref, wg_ref, wu_ref, wd_ref, g_ref, sc_ref, sh_ref, o_ref, hn_ref, acc_ref = refs
    else:
        h_ref, x_ref, ga_ref, wg_ref, wu_ref, wd_ref, o_ref, acc_ref = refs
    f = pl.program_id(1)
    last = pl.num_programs(1) - 1

    def partial_down(rows):
        h = h_ref[rows, :]
        gate = jnp.dot(h, wg_ref[...], preferred_element_type=F32)
        up = jnp.dot(h, wu_ref[...], preferred_element_type=F32)
        act = (_silu(gate) * up).astype(BF16)
        return jnp.dot(act, wd_ref[...], preferred_element_type=F32)

    @pl.when(f == 0)
    def _():
        acc_ref[...] = partial_down(slice(None))

    @pl.when(jnp.logical_and(f > 0, f < last))
    def _():
        acc_ref[...] += partial_down(slice(None))

    @pl.when(f == last)
    def _():
        for r in range(h_ref.shape[0] // ROW_CHUNK):
            rows = slice(r * ROW_CHUNK, (r + 1) * ROW_CHUNK)
            x_new = x_ref[rows, :] + ga_ref[0] * (acc_ref[rows, :] + partial_down(rows))
            o_ref[rows, :] = x_new
            if emit_next:
                hn_ref[rows, :] = _modulated_norm(x_new, g_ref[...], sc_ref[0], sh_ref[0]).astype(BF16)


def _swiglu(h, x2, gate, w_gate, w_up, w_down, next_norm, *, seq, tm, tf):
    m, d = x2.shape
    dff = w_gate.shape[1]
    tpb = seq // tm
    emit_next = next_norm is not None
    row = pl.BlockSpec((tm, d), lambda i, f: (i, 0))
    mod = pl.BlockSpec((1, 1, d), lambda i, f: (i // tpb, 0, 0))
    in_specs = [row, row, mod,
                pl.BlockSpec((d, tf), lambda i, f: (0, f)),
                pl.BlockSpec((d, tf), lambda i, f: (0, f)),
                pl.BlockSpec((tf, d), lambda i, f: (f, 0))]
    args = [h, x2, gate, w_gate, w_up, w_down]
    out_specs = [row]
    out_shape = [jax.ShapeDtypeStruct((m, d), F32)]
    if emit_next:
        in_specs += [pl.BlockSpec((1, d), lambda i, f: (0, 0)), mod, mod]
        args += list(next_norm)
        out_specs.append(row)
        out_shape.append(jax.ShapeDtypeStruct((m, d), BF16))
    outs = pl.pallas_call(
        functools.partial(_swiglu_kernel, emit_next=emit_next),
        grid=(m // tm, dff // tf),
        in_specs=in_specs,
        out_specs=out_specs,
        out_shape=out_shape,
        scratch_shapes=[pltpu.VMEM((tm, d), F32)],
        compiler_params=_cparams(("arbitrary", "arbitrary")),
        name="swiglu",
    )(*args)
    return (outs[0], outs[1]) if emit_next else (outs[0], None)


def _tile(n, pref):
    t = min(n, pref)
    assert n % t == 0, (n, t)
    return t


def _mod_vectors(mod_l):
    d = D_MODEL
    return [mod_l[:, None, k * d:(k + 1) * d] for k in range(6)]


def _layer(x2, h1, mod_l, next_norm, batch, seq, w_in, w_pool, pool_scale, sb_q_gain, sb_k_gain,
           gla_w_a2, gla_b_a2, gla_out_gain, w_br_pool, w_br_sb, w_br_gla, w_out,
           g_norm2, w_ff_gate, w_ff_up, w_ff_down):
    d = D_MODEL
    _, _, ga1, sh2, sc2, ga2 = _mod_vectors(mod_l)

    o_sb = POOL_WIDTH
    o_gla = o_sb + 3 * SB_WIDTH
    o_a = o_gla + 4 * GLA_WIDTH
    o_gate = o_a + GLA_RANK
    w_f32cols = jnp.concatenate(
        [w_in[:, :o_sb], w_in[:, o_gla:o_a], jnp.pad(w_in[:, o_a:o_gate], ((0, 0), (0, MXU_WIDTH - GLA_RANK)))],
        axis=1).astype(BF16)
    w_sb = w_in[:, o_sb:o_gla].astype(BF16)
    w_gates = w_in[:, o_gate:].astype(BF16)

    tm = _tile(seq, 1024)
    tm_half = _tile(seq, 512)
    n_f32 = w_f32cols.shape[1]
    proj = _proj(h1, w_f32cols, jnp.ones((1, n_f32), F32), tm=tm_half, tn=n_f32, n_norm_tiles=0, out_dtype=F32)
    q_scale = LOG2E / math.sqrt(GROUP_DIM)
    colgain = jnp.concatenate([jnp.tile(sb_q_gain * q_scale, SB_HEADS), jnp.tile(sb_k_gain, SB_HEADS),
                               jnp.ones((SB_WIDTH,), F32)]).reshape(1, 3 * SB_WIDTH)
    tn_sb = 1024
    qkv = _proj(h1, w_sb, colgain, tm=tm, tn=tn_sb, n_norm_tiles=2 * SB_WIDTH // tn_sb, out_dtype=BF16)

    proj3 = proj.reshape(batch, seq, proj.shape[1])
    y_pool = _pool(proj3, w_pool.astype(BF16), pool_scale, ts=_tile(seq, 512))
    y_sb = _stick_breaking(qkv.reshape(batch, seq, 3 * SB_WIDTH), tq=_tile(seq, 512), tk=128)
    w_a2p = jnp.pad(gla_w_a2, ((0, GROUP_DIM - GLA_RANK), (0, 0))).astype(BF16)
    y_gla = _gla(proj3, w_a2p, gla_b_a2.reshape(1, GLA_WIDTH), gla_out_gain.reshape(1, GROUP_DIM))

    m = batch * seq
    merged = _merge(h1, y_pool.reshape(m, POOL_WIDTH), y_sb.reshape(m, SB_WIDTH), y_gla.reshape(m, GLA_WIDTH),
                    w_gates, w_br_pool.astype(BF16), w_br_sb.astype(BF16), w_br_gla.astype(BF16),
                    tm=tm, tn=512)
    x2, h2 = _resproj(merged, w_out.astype(BF16), x2, ga1, g_norm2.reshape(1, d), sc2, sh2, seq=seq, tm=tm_half)
    return _swiglu(h2, x2, ga2, w_ff_gate.astype(BF16), w_ff_up.astype(BF16), w_ff_down.astype(BF16),
                   next_norm, seq=seq, tm=tm_half, tf=512)


def kernel(x, c, w_ada, b_ada, g_norm1, w_in, w_pool, pool_scale, sb_q_gain, sb_k_gain, gla_w_a2, gla_b_a2, gla_out_gain, w_br_pool, w_br_sb, w_br_gla, w_out, g_norm2, w_ff_gate, w_ff_up, w_ff_down):
    batch, seq, d = x.shape
    depth = w_ada.shape[0]
    c_pad = jnp.pad(c, ((0, 8 - batch % 8 if batch % 8 else 0), (0, 0)))
    mod = _ada(c_pad, w_ada, b_ada)[:, :batch]
    x2 = x.reshape(batch * seq, d)

    def first_norm(l):
        sh1, sc1 = _mod_vectors(mod[l])[:2]
        return g_norm1[l].reshape(1, d), sc1, sh1

    h1 = _norm(x2, *first_norm(0), seq=seq, tm=_tile(seq, 512))
    for l in range(depth):
        next_norm = first_norm(l + 1) if l + 1 < depth else None
        x2, h1 = _layer(x2, h1, mod[l], next_norm, batch, seq, w_in[l], w_pool[l], pool_scale[l],
                        sb_q_gain[l], sb_k_gain[l], gla_w_a2[l], gla_b_a2[l], gla_out_gain[l],
                        w_br_pool[l], w_br_sb[l], w_br_gla[l], w_out[l], g_norm2[l],
                        w_ff_gate[l], w_ff_up[l], w_ff_down[l])
    return x2.reshape(batch, seq, d)
```

```python
import functools
import math

import numpy as np
import jax
import jax.numpy as jnp
from jax import lax
from jax.experimental import pallas as pl
from jax.experimental.pallas import tpu as pltpu

F32 = jnp.float32
BF16 = jnp.bfloat16

D_MODEL = 2048
POOL_WINDOWS = (2, 4, 8, 16)
POOL_GROUPS = 4
GROUP_DIM = 128
POOL_WIDTH = POOL_GROUPS * GROUP_DIM
SB_HEADS = 8
SB_WIDTH = SB_HEADS * GROUP_DIM
GLA_HEADS = 4
GLA_WIDTH = GLA_HEADS * GROUP_DIM
GLA_RANK = 16
GLA_TAU = 16.0
GLA_CHUNK = 128
GLA_LEVELS = 7
N_BRANCH = 3
D_FF = 5632
RMS_EPS = 1e-6
LOG2E = 1.4426950408889634
POOL_HALO = 16
SB_DEAD_BITS = 160.0
SB_MASKED_BITS = 1.0e4
SB_STATIC_LAGS = 3

VMEM_LIMIT = 56 * 1024 * 1024
MXU_WIDTH = 256
ROW_CHUNK = 256


def _cparams(sem):
    return pltpu.CompilerParams(dimension_semantics=sem, vmem_limit_bytes=VMEM_LIMIT)


def _nt_dot(a, b):
    return lax.dot_general(a, b, (((1,), (1,)), ((), ())), preferred_element_type=F32)


def _silu(x):
    return x * jax.nn.sigmoid(x)


def _ada_kernel(c_ref, w_ref, b_ref, o_ref):
    c = c_ref[...]
    a = _silu(c).astype(BF16)
    o_ref[0] = jnp.dot(a, w_ref[0].astype(BF16), preferred_element_type=F32) + b_ref[0]


def _ada(c_pad, w_ada, b_ada, tn=1024):
    depth, d, n = w_ada.shape
    rows = c_pad.shape[0]
    return pl.pallas_call(
        _ada_kernel,
        grid=(depth, n // tn),
        in_specs=[
            pl.BlockSpec((rows, d), lambda l, j: (0, 0)),
            pl.BlockSpec((1, d, tn), lambda l, j: (l, 0, j)),
            pl.BlockSpec((1, 1, tn), lambda l, j: (l, 0, j)),
        ],
        out_specs=pl.BlockSpec((1, rows, tn), lambda l, j: (l, 0, j)),
        out_shape=jax.ShapeDtypeStruct((depth, rows, n), F32),
        compiler_params=_cparams(("arbitrary", "arbitrary")),
        name="ada_modulation",
    )(c_pad, w_ada, b_ada.reshape(depth, 1, n))


def _modulated_norm(x, gain, scale, shift):
    ms = jnp.mean(x * x, axis=-1, keepdims=True)
    y = x * lax.rsqrt(ms + RMS_EPS) * gain
    return y * (1.0 + scale) + shift


def _group_rmsnorm(acc, colgain):
    out = []
    for c in range(acc.shape[1] // GROUP_DIM):
        sl = slice(c * GROUP_DIM, (c + 1) * GROUP_DIM)
        blk = acc[:, sl]
        ms = jnp.mean(blk * blk, axis=-1, keepdims=True)
        out.append(blk * lax.rsqrt(ms + RMS_EPS) * colgain[:, sl])
    return jnp.concatenate(out, axis=1)


def _norm_kernel(x_ref, g_ref, sc_ref, sh_ref, h_ref):
    h_ref[...] = _modulated_norm(x_ref[...], g_ref[...], sc_ref[0], sh_ref[0]).astype(BF16)


def _norm(x2, gain, scale, shift, *, seq, tm):
    m, d = x2.shape
    tpb = seq // tm
    mod = pl.BlockSpec((1, 1, d), lambda i: (i // tpb, 0, 0))
    return pl.pallas_call(
        _norm_kernel,
        grid=(m // tm,),
        in_specs=[pl.BlockSpec((tm, d), lambda i: (i, 0)), pl.BlockSpec((1, d), lambda i: (0, 0)), mod, mod],
        out_specs=pl.BlockSpec((tm, d), lambda i: (i, 0)),
        out_shape=jax.ShapeDtypeStruct((m, d), BF16),
        compiler_params=_cparams(("arbitrary",)),
        name="modulated_norm",
    )(x2, gain, scale, shift)


def _proj_kernel(h_ref, w_ref, cg_ref, o_ref, *, n_norm_tiles, tn):
    j = pl.program_id(1)
    h = h_ref[...]

    def tile(normed):
        for c in range(tn // MXU_WIDTH):
            cols = slice(c * MXU_WIDTH, (c + 1) * MXU_WIDTH)
            acc = jnp.dot(h, w_ref[:, cols], preferred_element_type=F32)
            if normed:
                acc = _group_rmsnorm(acc, cg_ref[:, cols])
            o_ref[:, cols] = acc.astype(o_ref.dtype)

    if n_norm_tiles == 0:
        tile(False)
    else:
        pl.when(j < n_norm_tiles)(lambda: tile(True))
        pl.when(j >= n_norm_tiles)(lambda: tile(False))


def _proj(h, w, colgain, *, tm, tn, n_norm_tiles, out_dtype):
    m, d = h.shape
    n = w.shape[1]
    kern = functools.partial(_proj_kernel, n_norm_tiles=n_norm_tiles, tn=tn)
    return pl.pallas_call(
        kern,
        grid=(m // tm, n // tn),
        in_specs=[
            pl.BlockSpec((tm, d), lambda i, j: (i, 0)),
            pl.BlockSpec((d, tn), lambda i, j: (0, j)),
            pl.BlockSpec((1, tn), lambda i, j: (0, j)),
        ],
        out_specs=pl.BlockSpec((tm, tn), lambda i, j: (i, j)),
        out_shape=jax.ShapeDtypeStruct((m, n), out_dtype),
        compiler_params=_cparams(("arbitrary", "arbitrary")),
        name="in_proj",
    )(h, w, colgain)


def _pool_kernel(u_ref, w_ref, ps_ref, o_ref, ext_ref, *, ts):
    i = pl.program_id(1)

    @pl.when(i == 0)
    def _():
        ext_ref[0:POOL_HALO, :] = jnp.zeros((POOL_HALO, POOL_WIDTH), F32)

    @pl.when(i > 0)
    def _():
        ext_ref[0:POOL_HALO, :] = ext_ref[ts:ts + POOL_HALO, :]

    ext_ref[POOL_HALO:, :] = u_ref[0]
    pos1 = (i * ts + 1 + lax.broadcasted_iota(jnp.int32, (ts, GROUP_DIM), 0)).astype(F32)
    for g, w in enumerate(POOL_WINDOWS):
        cols = slice(g * GROUP_DIM, (g + 1) * GROUP_DIM)
        u = ext_ref[POOL_HALO:, cols]
        win = u
        for k in range(1, w):
            win = win + ext_ref[POOL_HALO - k:POOL_HALO - k + ts, cols]
        pooled = win / jnp.minimum(pos1, float(w)) - u
        y = jnp.dot(pooled.astype(BF16), w_ref[g], preferred_element_type=F32) * ps_ref[g]
        o_ref[0, :, cols] = y.astype(o_ref.dtype)


def _pool(proj3, w_pool, pool_scale, ts):
    b, s, _ = proj3.shape
    kern = functools.partial(_pool_kernel, ts=ts)
    return pl.pallas_call(
        kern,
        grid=(b, s // ts),
        in_specs=[
            pl.BlockSpec((1, ts, POOL_WIDTH), lambda bi, i: (bi, i, 0)),
            pl.BlockSpec((POOL_GROUPS, GROUP_DIM, GROUP_DIM), lambda bi, i: (0, 0, 0)),
            pl.BlockSpec((POOL_GROUPS, 1, GROUP_DIM), lambda bi, i: (0, 0, 0)),
        ],
        out_specs=pl.BlockSpec((1, ts, POOL_WIDTH), lambda bi, i: (bi, i, 0)),
        out_shape=jax.ShapeDtypeStruct((b, s, POOL_WIDTH), BF16),
        scratch_shapes=[pltpu.VMEM((ts + POOL_HALO, POOL_WIDTH), F32)],
        compiler_params=_cparams(("arbitrary", "arbitrary")),
        name="pool_mixer",
    )(proj3, w_pool, pool_scale.reshape(POOL_GROUPS, 1, GROUP_DIM))


def _sb_cumsum_matrix(tk):
    j = np.arange(tk)[:, None]
    s = np.arange(tk)[None, :]
    one = np.concatenate([(j >= s).astype(np.float32), np.ones((tk, tk), np.float32)], axis=1)
    return jnp.asarray(np.concatenate([one, one], axis=0), dtype=BF16)


def _sb_kernel(q_ref, k_ref, v_ref, w2_ref, o_ref, acc_ref, run_ref, *, tq, tk):
    i = pl.program_id(2)
    w2 = w2_ref[...]
    n_sub = tq // tk
    acc_ref[...] = jnp.zeros((tq, GROUP_DIM), F32)
    run_ref[...] = jnp.zeros((tq, tk), F32)
    row = lax.broadcasted_iota(jnp.int32, (tq, tk), 0)
    col = lax.broadcasted_iota(jnp.int32, (tq, tk), 1)
    diag_mask = col < (row & (tk - 1))
    sub = lambda r: slice(r * tk, (r + 1) * tk)

    def key_blocks(lag):
        blocks = []
        for r in range(n_sub):
            jb = i * n_sub + r - lag
            start = pl.multiple_of(jnp.maximum(jb, 0) * tk, tk)
            blocks.append((start, jnp.where(jb >= 0, 0.0, SB_MASKED_BITS)))
        return blocks

    def scores(blocks, diagonal):
        z = jnp.concatenate([_nt_dot(q_ref[0, sub(r), :], k_ref[0, pl.ds(start, tk), :])
                             for r, (start, _) in enumerate(blocks)], axis=0)
        neg_abs = lax.bitcast_convert_type(
            lax.bitcast_convert_type(z, jnp.uint32) | jnp.uint32(0x80000000), F32)
        neg_log_keep = jnp.maximum(z, 0.0) + jnp.log(1.0 + jnp.exp2(neg_abs)) * LOG2E
        if diagonal:
            neg_log_keep = jnp.where(diag_mask, neg_log_keep, 0.0)
            z = jnp.where(diag_mask, z, -SB_MASKED_BITS)
        hi = lax.bitcast_convert_type(
            lax.bitcast_convert_type(neg_log_keep, jnp.uint32) & jnp.uint32(0xFFFF0000), F32)
        lo = neg_log_keep - hi
        c = jnp.dot(jnp.concatenate([hi.astype(BF16), lo.astype(BF16)], axis=1), w2, preferred_element_type=F32)
        return z, c

    def accumulate(blocks, z, c, diagonal):
        run = run_ref[...]
        if not diagonal:
            run = run + jnp.concatenate([jnp.full((tk, tk), pen, F32) for _, pen in blocks], axis=0)
        a = jnp.exp2(z - c[:, :tk] - run).astype(BF16)
        for r, (start, _) in enumerate(blocks):
            acc_ref[sub(r), :] += jnp.dot(a[sub(r), :], v_ref[0, pl.ds(start, tk), :], preferred_element_type=F32)
        run_ref[...] = run + c[:, tk:]

    lags = [key_blocks(lag) for lag in range(SB_STATIC_LAGS)]
    staged = [scores(blocks, lag == 0) for lag, blocks in enumerate(lags)]
    for lag, (blocks, (z, c)) in enumerate(zip(lags, staged)):
        accumulate(blocks, z, c, lag == 0)

    def cond(carry):
        lag, min_run = carry
        return jnp.logical_and(lag <= i * n_sub + n_sub - 1, min_run < SB_DEAD_BITS)

    def body(carry):
        lag, _ = carry
        blocks = key_blocks(lag)
        z, c = scores(blocks, False)
        accumulate(blocks, z, c, False)
        return lag + 1, jnp.min(run_ref[...])

    lax.while_loop(cond, body, (jnp.int32(SB_STATIC_LAGS), jnp.min(run_ref[...])))
    o_ref[0] = acc_ref[...].astype(o_ref.dtype)


def _stick_breaking(qkv, tq, tk):
    b, s, _ = qkv.shape
    kern = functools.partial(_sb_kernel, tq=tq, tk=tk)
    return pl.pallas_call(
        kern,
        grid=(b, SB_HEADS, s // tq),
        in_specs=[
            pl.BlockSpec((1, tq, GROUP_DIM), lambda bi, h, i: (bi, i, h)),
            pl.BlockSpec((1, s, GROUP_DIM), lambda bi, h, i: (bi, 0, SB_HEADS + h)),
            pl.BlockSpec((1, s, GROUP_DIM), lambda bi, h, i: (bi, 0, 2 * SB_HEADS + h)),
            pl.BlockSpec((2 * tk, 2 * tk), lambda bi, h, i: (0, 0)),
        ],
        out_specs=pl.BlockSpec((1, tq, GROUP_DIM), lambda bi, h, i: (bi, i, h)),
        out_shape=jax.ShapeDtypeStruct((b, s, SB_WIDTH), BF16),
        scratch_shapes=[pltpu.VMEM((tq, GROUP_DIM), F32), pltpu.VMEM((tq, tk), F32)],
        compiler_params=_cparams(("arbitrary", "arbitrary", "arbitrary")),
        name="stick_breaking",
    )(qkv, qkv, qkv, _sb_cumsum_matrix(tk))


def _gla_constants():
    c = GLA_CHUNK
    t = np.arange(c)[:, None]
    j = np.arange(c)[None, :]
    dst = np.concatenate([j <= t, np.ones((8, c), bool)], axis=0).astype(np.float32)
    s = j
    masks = []
    for l in range(GLA_LEVELS):
        masks.append(((t ^ s) >> l == 1) & (t > s))
    masks.append(t == s)
    return jnp.asarray(dst, dtype=BF16), jnp.asarray(np.stack(masks).astype(np.float32))


def _split3(x):
    x1 = x.astype(BF16)
    r = x - x1.astype(F32)
    x2 = r.astype(BF16)
    x3 = (r - x2.astype(F32)).astype(BF16)
    return x1, x2, x3


def _gla_kernel(q_ref, k_ref, v_ref, r_ref, a_ref, wa_ref, ba_ref, og_ref, dst_ref, msk_ref,
                o_ref, st_ref):
    @pl.when(pl.program_id(1) == 0)
    def _():
        st_ref[...] = jnp.zeros(st_ref.shape, F32)

    for cc in range(q_ref.shape[1] // GLA_CHUNK):
        _gla_chunk(slice(cc * GLA_CHUNK, (cc + 1) * GLA_CHUNK), q_ref, k_ref, v_ref, r_ref, a_ref,
                   wa_ref, ba_ref, og_ref, dst_ref, msk_ref, o_ref, st_ref)


def _gla_chunk(rows, q_ref, k_ref, v_ref, r_ref, a_ref, wa_ref, ba_ref, og_ref, dst_ref, msk_ref,
               o_ref, st_ref):
    c = GLA_CHUNK
    x = jnp.dot(a_ref[0, rows, :].astype(BF16), wa_ref[...], preferred_element_type=F32) + ba_ref[...]
    log_sig = jnp.minimum(x, 0.0) - jnp.log(1.0 + jnp.exp(-jnp.abs(x)))
    g = log_sig * (LOG2E / GLA_TAU)
    dst = dst_ref[...]
    g1, g2, g3 = _split3(g)
    sums = (jnp.dot(dst, g1, preferred_element_type=F32)
            + jnp.dot(dst, g2, preferred_element_type=F32)
            + jnp.dot(dst, g3, preferred_element_type=F32))
    prefix = sums[:c]
    total = sums[c:c + 1]
    e_in = prefix
    e_out = total - prefix

    row = lax.broadcasted_iota(jnp.int32, g.shape, 0)
    pos = row & 3
    g_prev = pltpu.roll(g, 1, 0)
    g_next = pltpu.roll(g, c - 1, 0)
    e_lvl = [jnp.where((row & 1) == 1, g, 0.0),
             jnp.where(pos == 0, g_next, jnp.where(pos == 1, 0.0, jnp.where(pos == 2, g, g_prev + g)))]
    for l in range(2, GLA_LEVELS):
        m = 1 << l
        blocks = prefix.reshape(c // (2 * m), 2 * m, GLA_WIDTH)
        e_lvl.append((-jnp.abs(blocks - blocks[:, m - 1:m, :])).reshape(c, GLA_WIDTH))

    for h in range(GLA_HEADS):
        cols = slice(h * GROUP_DIM, (h + 1) * GROUP_DIM)
        q = q_ref[0, rows, cols] * (GROUP_DIM ** -0.5)
        k = k_ref[0, rows, cols]
        v = v_ref[0, rows, cols].astype(BF16)
        scores = msk_ref[GLA_LEVELS] * _nt_dot(q.astype(BF16), k.astype(BF16))
        for l in range(GLA_LEVELS):
            xl = jnp.exp2(e_lvl[l][:, cols])
            scores = scores + msk_ref[l] * _nt_dot((q * xl).astype(BF16), (k * xl).astype(BF16))
        o = jnp.dot(scores.astype(BF16), v, preferred_element_type=F32)
        x_in = jnp.exp2(e_in[:, cols])
        st = st_ref[h]
        o = o + _nt_dot((q * x_in).astype(BF16), st.astype(BF16))
        x_out = jnp.exp2(e_out[:, cols])
        x_all = jnp.exp2(total[:, cols])
        vt = v_ref[0, rows, cols].T.astype(BF16)
        st_ref[h] = st * x_all + jnp.dot(vt, (k * x_out).astype(BF16), preferred_element_type=F32)
        ms = jnp.mean(o * o, axis=-1, keepdims=True)
        o = o * lax.rsqrt(ms + RMS_EPS) * og_ref[...]
        o_ref[0, rows, cols] = (o * _silu(r_ref[0, rows, cols])).astype(o_ref.dtype)


def _gla(proj3, w_a2p, b_a2, out_gain, chunks_per_step):
    b, s, _ = proj3.shape
    c = GLA_CHUNK * chunks_per_step
    dst, masks = _gla_constants()
    wide = lambda blk: pl.BlockSpec((1, c, GLA_WIDTH), lambda bi, ci: (bi, ci, blk))
    const2 = lambda shape: pl.BlockSpec(shape, lambda bi, ci: (0, 0))
    a_blk = (POOL_WIDTH + 4 * GLA_WIDTH) // GROUP_DIM
    return pl.pallas_call(
        _gla_kernel,
        grid=(b, s // c),
        in_specs=[
            wide(1), wide(2), wide(3), wide(4),
            pl.BlockSpec((1, c, GROUP_DIM), lambda bi, ci: (bi, ci, a_blk)),
            const2((GROUP_DIM, GLA_WIDTH)),
            const2((1, GLA_WIDTH)),
            const2((1, GROUP_DIM)),
            const2(dst.shape),
            pl.BlockSpec(masks.shape, lambda bi, ci: (0, 0, 0)),
        ],
        out_specs=pl.BlockSpec((1, c, GLA_WIDTH), lambda bi, ci: (bi, ci, 0)),
        out_shape=jax.ShapeDtypeStruct((b, s, GLA_WIDTH), BF16),
        scratch_shapes=[pltpu.VMEM((GLA_HEADS, GROUP_DIM, GROUP_DIM), F32)],
        compiler_params=_cparams(("arbitrary", "arbitrary")),
        name="gla",
    )(proj3, proj3, proj3, proj3, proj3, w_a2p, b_a2, out_gain, dst, masks)


def _merge_kernel(h_ref, yp_ref, ys_ref, yg_ref, wgp_ref, wgs_ref, wgg_ref, wp_ref, ws_ref, wg_ref, o_ref):
    h = h_ref[...]

    def branch(wgate_ref, y_ref, w_ref):
        gate = jax.nn.sigmoid(jnp.dot(h, wgate_ref[...], preferred_element_type=F32))
        return gate * jnp.dot(y_ref[...], w_ref[...], preferred_element_type=F32)

    merged = branch(wgp_ref, yp_ref, wp_ref) + branch(wgs_ref, ys_ref, ws_ref) + branch(wgg_ref, yg_ref, wg_ref)
    o_ref[...] = merged.astype(o_ref.dtype)


def _merge(h, y_pool, y_sb, y_gla, w_gates, w_br_pool, w_br_sb, w_br_gla, *, tm, tn):
    m, d = h.shape
    nj = d // tn
    row = lambda width: pl.BlockSpec((tm, width), lambda i, j: (i, 0))
    gate_w = lambda br: pl.BlockSpec((d, tn), lambda i, j: (0, br * nj + j))
    br_w = lambda width: pl.BlockSpec((width, tn), lambda i, j: (0, j))
    return pl.pallas_call(
        _merge_kernel,
        grid=(m // tm, nj),
        in_specs=[
            row(d), row(POOL_WIDTH), row(SB_WIDTH), row(GLA_WIDTH),
            gate_w(0), gate_w(1), gate_w(2),
            br_w(POOL_WIDTH), br_w(SB_WIDTH), br_w(GLA_WIDTH),
        ],
        out_specs=pl.BlockSpec((tm, tn), lambda i, j: (i, j)),
        out_shape=jax.ShapeDtypeStruct((m, d), BF16),
        compiler_params=_cparams(("arbitrary", "arbitrary")),
        name="gated_merge",
    )(h, y_pool, y_sb, y_gla, w_gates, w_gates, w_gates, w_br_pool, w_br_sb, w_br_gla)


def _resproj_kernel(a_ref, w_ref, x_ref, ga_ref, g_ref, sc_ref, sh_ref, o_ref, h_ref):
    for r in range(a_ref.shape[0] // ROW_CHUNK):
        rows = slice(r * ROW_CHUNK, (r + 1) * ROW_CHUNK)
        y = jnp.dot(a_ref[rows, :], w_ref[...], preferred_element_type=F32)
        x_new = x_ref[rows, :] + ga_ref[0] * y
        o_ref[rows, :] = x_new
        h_ref[rows, :] = _modulated_norm(x_new, g_ref[...], sc_ref[0], sh_ref[0]).astype(BF16)


def _resproj(a, w, x2, gate, gain, scale, shift, *, seq, tm):
    m, kdim = a.shape
    d = w.shape[1]
    tpb = seq // tm
    row = lambda width: pl.BlockSpec((tm, width), lambda i: (i, 0))
    mod = pl.BlockSpec((1, 1, d), lambda i: (i // tpb, 0, 0))
    return pl.pallas_call(
        _resproj_kernel,
        grid=(m // tm,),
        in_specs=[row(kdim), pl.BlockSpec((kdim, d), lambda i: (0, 0)), row(d), mod,
                  pl.BlockSpec((1, d), lambda i: (0, 0)), mod, mod],
        out_specs=[row(d), row(d)],
        out_shape=[jax.ShapeDtypeStruct((m, d), F32), jax.ShapeDtypeStruct((m, d), BF16)],
        compiler_params=_cparams(("arbitrary",)),
        name="residual_proj",
    )(a, w, x2, gate, gain, scale, shift)


def _swiglu_kernel(*refs, emit_next, x_chunks):
    if emit_next:
        h_ref, x_ref, ga_ref, wg_ref, wu_ref, wd_ref, g_ref, sc_ref, sh_ref, o_ref, hn_ref = refs
    else:
        h_ref, x_ref, ga_ref, wg_ref, wu_ref, wd_ref, o_ref = refs
    f = pl.program_id(1)
    last = pl.num_programs(1) - 1
    xw = x_ref.shape[1]

    def gated_partial(rows):
        h = h_ref[rows, :]
        gate = jnp.dot(h, wg_ref[...], preferred_element_type=F32)
        up = jnp.dot(h, wu_ref[...], preferred_element_type=F32)
        act = (_silu(gate) * up).astype(BF16)
        return ga_ref[0] * jnp.dot(act, wd_ref[...], preferred_element_type=F32)

    row_chunks = [slice(r * ROW_CHUNK, (r + 1) * ROW_CHUNK) for r in range(h_ref.shape[0] // ROW_CHUNK)]

    @pl.when(f == 0)
    def _():
        for rows in row_chunks:
            o_ref[rows, :] = gated_partial(rows)

    @pl.when(jnp.logical_and(f > 0, f < last))
    def _():
        for rows in row_chunks:
            o_ref[rows, :] += gated_partial(rows)

    for c in range(x_chunks):
        @pl.when(f == c)
        def _():
            o_ref[:, c * xw:(c + 1) * xw] += x_ref[...]

    @pl.when(f == last)
    def _():
        for rows in row_chunks:
            x_new = o_ref[rows, :] + gated_partial(rows)
            o_ref[rows, :] = x_new
            if emit_next:
                hn_ref[rows, :] = _modulated_norm(x_new, g_ref[...], sc_ref[0], sh_ref[0]).astype(BF16)


def _swiglu(h, x2, gate, w_gate, w_up, w_down, next_norm, *, seq, tm, tf):
    m, d = x2.shape
    dff = w_gate.shape[1]
    tpb = seq // tm
    n_steps = dff // tf
    x_chunks = 8
    assert x_chunks < n_steps and d % (x_chunks * GROUP_DIM) == 0
    emit_next = next_norm is not None
    row = pl.BlockSpec((tm, d), lambda i, f: (i, 0))
    mod = pl.BlockSpec((1, 1, d), lambda i, f: (i // tpb, 0, 0))
    in_specs = [row, pl.BlockSpec((tm, d // x_chunks), lambda i, f: (i, jnp.minimum(f, x_chunks - 1))), mod,
                pl.BlockSpec((d, tf), lambda i, f: (0, f)),
                pl.BlockSpec((d, tf), lambda i, f: (0, f)),
                pl.BlockSpec((tf, d), lambda i, f: (f, 0))]
    args = [h, x2, gate, w_gate, w_up, w_down]
    out_specs = [row]
    out_shape = [jax.ShapeDtypeStruct((m, d), F32)]
    if emit_next:
        in_specs += [pl.BlockSpec((1, d), lambda i, f: (0, 0)), mod, mod]
        args += list(next_norm)
        out_specs.append(row)
        out_shape.append(jax.ShapeDtypeStruct((m, d), BF16))
    outs = pl.pallas_call(
        functools.partial(_swiglu_kernel, emit_next=emit_next, x_chunks=x_chunks),
        grid=(m // tm, n_steps),
        in_specs=in_specs,
        out_specs=out_specs,
        out_shape=out_shape,
        compiler_params=_cparams(("arbitrary", "arbitrary")),
        name="swiglu",
    )(*args)
    return (outs[0], outs[1]) if emit_next else (outs[0], None)


def _tile(n, pref):
    t = min(n, pref)
    assert n % t == 0, (n, t)
    return t


def _mod_vectors(mod_l):
    d = D_MODEL
    return [mod_l[:, None, k * d:(k + 1) * d] for k in range(6)]


def _layer(x2, h1, mod_l, next_norm, batch, seq, w_in, w_pool, pool_scale, sb_q_gain, sb_k_gain,
           gla_w_a2, gla_b_a2, gla_out_gain, w_br_pool, w_br_sb, w_br_gla, w_out,
           g_norm2, w_ff_gate, w_ff_up, w_ff_down):
    d = D_MODEL
    _, _, ga1, sh2, sc2, ga2 = _mod_vectors(mod_l)

    o_sb = POOL_WIDTH
    o_gla = o_sb + 3 * SB_WIDTH
    o_a = o_gla + 4 * GLA_WIDTH
    o_gate = o_a + GLA_RANK
    w_f32cols = jnp.concatenate(
        [w_in[:, :o_sb], w_in[:, o_gla:o_a], jnp.pad(w_in[:, o_a:o_gate], ((0, 0), (0, MXU_WIDTH - GLA_RANK)))],
        axis=1).astype(BF16)
    w_sb = w_in[:, o_sb:o_gla].astype(BF16)
    w_gates = w_in[:, o_gate:].astype(BF16)

    tm = _tile(seq, 1024)
    tm_half = _tile(seq, 512)
    n_f32 = w_f32cols.shape[1]
    proj = _proj(h1, w_f32cols, jnp.ones((1, n_f32), F32), tm=tm_half, tn=n_f32, n_norm_tiles=0, out_dtype=F32)
    q_scale = LOG2E / math.sqrt(GROUP_DIM)
    colgain = jnp.concatenate([jnp.tile(sb_q_gain * q_scale, SB_HEADS), jnp.tile(sb_k_gain, SB_HEADS),
                               jnp.ones((SB_WIDTH,), F32)]).reshape(1, 3 * SB_WIDTH)
    tn_sb = 1024
    qkv = _proj(h1, w_sb, colgain, tm=tm, tn=tn_sb, n_norm_tiles=2 * SB_WIDTH // tn_sb, out_dtype=BF16)

    proj3 = proj.reshape(batch, seq, proj.shape[1])
    y_pool = _pool(proj3, w_pool.astype(BF16), pool_scale, ts=_tile(seq, 512))
    y_sb = _stick_breaking(qkv.reshape(batch, seq, 3 * SB_WIDTH), tq=tm, tk=GROUP_DIM)
    w_a2p = jnp.pad(gla_w_a2, ((0, GROUP_DIM - GLA_RANK), (0, 0))).astype(BF16)
    y_gla = _gla(proj3, w_a2p, gla_b_a2.reshape(1, GLA_WIDTH), gla_out_gain.reshape(1, GROUP_DIM),
                 chunks_per_step=2)

    m = batch * seq
    merged = _merge(h1, y_pool.reshape(m, POOL_WIDTH), y_sb.reshape(m, SB_WIDTH), y_gla.reshape(m, GLA_WIDTH),
                    w_gates, w_br_pool.astype(BF16), w_br_sb.astype(BF16), w_br_gla.astype(BF16),
                    tm=tm, tn=512)
    x2, h2 = _resproj(merged, w_out.astype(BF16), x2, ga1, g_norm2.reshape(1, d), sc2, sh2, seq=seq, tm=tm_half)
    return _swiglu(h2, x2, ga2, w_ff_gate.astype(BF16), w_ff_up.astype(BF16), w_ff_down.astype(BF16),
                   next_norm, seq=seq, tm=tm, tf=512)


def kernel(x, c, w_ada, b_ada, g_norm1, w_in, w_pool, pool_scale, sb_q_gain, sb_k_gain, gla_w_a2, gla_b_a2, gla_out_gain, w_br_pool, w_br_sb, w_br_gla, w_out, g_norm2, w_ff_gate, w_ff_up, w_ff_down):
    batch, seq, d = x.shape
    depth = w_ada.shape[0]
    c_pad = jnp.pad(c, ((0, 8 - batch % 8 if batch % 8 else 0), (0, 0)))
    mod = _ada(c_pad, w_ada, b_ada)[:, :batch]
    x2 = x.reshape(batch * seq, d)

    def first_norm(l):
        sh1, sc1 = _mod_vectors(mod[l])[:2]
        return g_norm1[l].reshape(1, d), sc1, sh1

    h1 = _norm(x2, *first_norm(0), seq=seq, tm=_tile(seq, 512))
    for l in range(depth):
        next_norm = first_norm(l + 1) if l + 1 < depth else None
        x2, h1 = _layer(x2, h1, mod[l], next_norm, batch, seq, w_in[l], w_pool[l], pool_scale[l],
                        sb_q_gain[l], sb_k_gain[l], gla_w_a2[l], gla_b_a2[l], gla_out_gain[l],
                        w_br_pool[l], w_br_sb[l], w_br_gla[l], w_out[l], g_norm2[l],
                        w_ff_gate[l], w_ff_up[l], w_ff_down[l])
    return x2.reshape(batch, seq, d)
```

```python
import functools
import math
from typing import NamedTuple

import numpy as np
import jax
import jax.numpy as jnp
from jax import lax
from jax.experimental import pallas as pl
from jax.experimental.pallas import tpu as pltpu

F32 = jnp.float32
BF16 = jnp.bfloat16

D_MODEL = 2048
POOL_WINDOWS = (2, 4, 8, 16)
POOL_GROUPS = 4
GROUP_DIM = 128
POOL_WIDTH = POOL_GROUPS * GROUP_DIM
SB_HEADS = 8
SB_WIDTH = SB_HEADS * GROUP_DIM
GLA_HEADS = 4
GLA_WIDTH = GLA_HEADS * GROUP_DIM
GLA_RANK = 16
GLA_TAU = 16.0
GLA_CHUNK = 128
GLA_LEVELS = 7
N_BRANCH = 3
D_FF = 5632
RMS_EPS = 1e-6
LOG2E = 1.4426950408889634
POOL_HALO = 16
SB_DEAD_BITS = 160.0
SB_MASKED_BITS = 1.0e4
SB_STATIC_LAGS = 3

VMEM_LIMIT = 56 * 1024 * 1024
MXU_WIDTH = 256
ROW_CHUNK = 256


def _cparams(sem):
    return pltpu.CompilerParams(dimension_semantics=sem, vmem_limit_bytes=VMEM_LIMIT)


def _nt_dot(a, b):
    return lax.dot_general(a, b, (((1,), (1,)), ((), ())), preferred_element_type=F32)


def _silu(x):
    return x * jax.nn.sigmoid(x)


def _ada_kernel(c_ref, w_ref, b_ref, o_ref):
    c = c_ref[...]
    a = _silu(c).astype(BF16)
    o_ref[0] = jnp.dot(a, w_ref[0].astype(BF16), preferred_element_type=F32) + b_ref[0]


def _ada(c_pad, w_ada, b_ada, tn=1024):
    depth, d, n = w_ada.shape
    rows = c_pad.shape[0]
    return pl.pallas_call(
        _ada_kernel,
        grid=(depth, n // tn),
        in_specs=[
            pl.BlockSpec((rows, d), lambda l, j: (0, 0)),
            pl.BlockSpec((1, d, tn), lambda l, j: (l, 0, j)),
            pl.BlockSpec((1, 1, tn), lambda l, j: (l, 0, j)),
        ],
        out_specs=pl.BlockSpec((1, rows, tn), lambda l, j: (l, 0, j)),
        out_shape=jax.ShapeDtypeStruct((depth, rows, n), F32),
        compiler_params=_cparams(("arbitrary", "arbitrary")),
        name="ada_modulation",
    )(c_pad, w_ada, b_ada.reshape(depth, 1, n))


def _modulated_norm(x, gain, scale, shift):
    ms = jnp.mean(x * x, axis=-1, keepdims=True)
    y = x * lax.rsqrt(ms + RMS_EPS) * gain
    return y * (1.0 + scale) + shift


def _group_rmsnorm(acc, colgain):
    out = []
    for c in range(acc.shape[1] // GROUP_DIM):
        sl = slice(c * GROUP_DIM, (c + 1) * GROUP_DIM)
        blk = acc[:, sl]
        ms = jnp.mean(blk * blk, axis=-1, keepdims=True)
        out.append(blk * lax.rsqrt(ms + RMS_EPS) * colgain[:, sl])
    return jnp.concatenate(out, axis=1)


def _norm_kernel(x_ref, g_ref, sc_ref, sh_ref, h_ref):
    h_ref[...] = _modulated_norm(x_ref[...], g_ref[...], sc_ref[0], sh_ref[0]).astype(BF16)


def _norm(x2, gain, scale, shift, *, seq, tm):
    m, d = x2.shape
    tpb = seq // tm
    mod = pl.BlockSpec((1, 1, d), lambda i: (i // tpb, 0, 0))
    return pl.pallas_call(
        _norm_kernel,
        grid=(m // tm,),
        in_specs=[pl.BlockSpec((tm, d), lambda i: (i, 0)), pl.BlockSpec((1, d), lambda i: (0, 0)), mod, mod],
        out_specs=pl.BlockSpec((tm, d), lambda i: (i, 0)),
        out_shape=jax.ShapeDtypeStruct((m, d), BF16),
        compiler_params=_cparams(("arbitrary",)),
        name="modulated_norm",
    )(x2, gain, scale, shift)


def _proj_kernel(h_ref, w_ref, cg_ref, o_ref, *, n_norm_tiles, tn):
    j = pl.program_id(1)
    h = h_ref[...]

    def tile(normed):
        for c in range(tn // MXU_WIDTH):
            cols = slice(c * MXU_WIDTH, (c + 1) * MXU_WIDTH)
            acc = jnp.dot(h, w_ref[:, cols], preferred_element_type=F32)
            if normed:
                acc = _group_rmsnorm(acc, cg_ref[:, cols])
            o_ref[:, cols] = acc.astype(o_ref.dtype)

    if n_norm_tiles == 0:
        tile(False)
    else:
        pl.when(j < n_norm_tiles)(lambda: tile(True))
        pl.when(j >= n_norm_tiles)(lambda: tile(False))


def _proj(h, w, colgain, *, tm, tn, n_norm_tiles, out_dtype):
    m, d = h.shape
    n = w.shape[1]
    kern = functools.partial(_proj_kernel, n_norm_tiles=n_norm_tiles, tn=tn)
    return pl.pallas_call(
        kern,
        grid=(m // tm, n // tn),
        in_specs=[
            pl.BlockSpec((tm, d), lambda i, j: (i, 0)),
            pl.BlockSpec((d, tn), lambda i, j: (0, j)),
            pl.BlockSpec((1, tn), lambda i, j: (0, j)),
        ],
        out_specs=pl.BlockSpec((tm, tn), lambda i, j: (i, j)),
        out_shape=jax.ShapeDtypeStruct((m, n), out_dtype),
        compiler_params=_cparams(("arbitrary", "arbitrary")),
        name="in_proj",
    )(h, w, colgain)


def _pool_kernel(u_ref, w_ref, ps_ref, o_ref, ext_ref, *, ts):
    i = pl.program_id(1)

    @pl.when(i == 0)
    def _():
        ext_ref[0:POOL_HALO, :] = jnp.zeros((POOL_HALO, POOL_WIDTH), F32)

    @pl.when(i > 0)
    def _():
        ext_ref[0:POOL_HALO, :] = ext_ref[ts:ts + POOL_HALO, :]

    ext_ref[POOL_HALO:, :] = u_ref[0]
    pos1 = (i * ts + 1 + lax.broadcasted_iota(jnp.int32, (ts, GROUP_DIM), 0)).astype(F32)
    for g, w in enumerate(POOL_WINDOWS):
        cols = slice(g * GROUP_DIM, (g + 1) * GROUP_DIM)
        u = ext_ref[POOL_HALO:, cols]
        win = u
        for k in range(1, w):
            win = win + ext_ref[POOL_HALO - k:POOL_HALO - k + ts, cols]
        pooled = win / jnp.minimum(pos1, float(w)) - u
        y = jnp.dot(pooled.astype(BF16), w_ref[g], preferred_element_type=F32) * ps_ref[g]
        o_ref[0, :, cols] = y.astype(o_ref.dtype)


def _pool(proj3, w_pool, pool_scale, ts):
    b, s, _ = proj3.shape
    kern = functools.partial(_pool_kernel, ts=ts)
    return pl.pallas_call(
        kern,
        grid=(b, s // ts),
        in_specs=[
            pl.BlockSpec((1, ts, POOL_WIDTH), lambda bi, i: (bi, i, 0)),
            pl.BlockSpec((POOL_GROUPS, GROUP_DIM, GROUP_DIM), lambda bi, i: (0, 0, 0)),
            pl.BlockSpec((POOL_GROUPS, 1, GROUP_DIM), lambda bi, i: (0, 0, 0)),
        ],
        out_specs=pl.BlockSpec((1, ts, POOL_WIDTH), lambda bi, i: (bi, i, 0)),
        out_shape=jax.ShapeDtypeStruct((b, s, POOL_WIDTH), BF16),
        scratch_shapes=[pltpu.VMEM((ts + POOL_HALO, POOL_WIDTH), F32)],
        compiler_params=_cparams(("arbitrary", "arbitrary")),
        name="pool_mixer",
    )(proj3, w_pool, pool_scale.reshape(POOL_GROUPS, 1, GROUP_DIM))


def _sb_cumsum_matrix(tk):
    j = np.arange(tk)[:, None]
    s = np.arange(tk)[None, :]
    one = np.concatenate([(j >= s).astype(np.float32), np.ones((tk, tk), np.float32)], axis=1)
    return jnp.asarray(np.concatenate([one, one], axis=0), dtype=BF16)


def _sb_kernel(q_ref, k_ref, v_ref, w2_ref, o_ref, acc_ref, run_ref, *, tq, tk):
    i = pl.program_id(2)
    w2 = w2_ref[...]
    n_sub = tq // tk
    acc_ref[...] = jnp.zeros((tq, GROUP_DIM), F32)
    run_ref[...] = jnp.zeros((tq, tk), F32)
    row = lax.broadcasted_iota(jnp.int32, (tq, tk), 0)
    col = lax.broadcasted_iota(jnp.int32, (tq, tk), 1)
    diag_mask = col < (row & (tk - 1))
    sub = lambda r: slice(r * tk, (r + 1) * tk)

    def key_blocks(lag):
        blocks = []
        for r in range(n_sub):
            jb = i * n_sub + r - lag
            start = pl.multiple_of(jnp.maximum(jb, 0) * tk, tk)
            blocks.append((start, jnp.where(jb >= 0, 0.0, SB_MASKED_BITS)))
        return blocks

    def scores(blocks, diagonal):
        z = jnp.concatenate([_nt_dot(q_ref[0, sub(r), :], k_ref[0, pl.ds(start, tk), :])
                             for r, (start, _) in enumerate(blocks)], axis=0)
        neg_abs = lax.bitcast_convert_type(
            lax.bitcast_convert_type(z, jnp.uint32) | jnp.uint32(0x80000000), F32)
        neg_log_keep = jnp.maximum(z, 0.0) + jnp.log(1.0 + jnp.exp2(neg_abs)) * LOG2E
        if diagonal:
            neg_log_keep = jnp.where(diag_mask, neg_log_keep, 0.0)
            z = jnp.where(diag_mask, z, -SB_MASKED_BITS)
        hi = lax.bitcast_convert_type(
            lax.bitcast_convert_type(neg_log_keep, jnp.uint32) & jnp.uint32(0xFFFF0000), F32)
        lo = neg_log_keep - hi
        c = jnp.dot(jnp.concatenate([hi.astype(BF16), lo.astype(BF16)], axis=1), w2, preferred_element_type=F32)
        return z, c

    def accumulate(blocks, z, c, diagonal):
        run = run_ref[...]
        if not diagonal:
            run = run + jnp.concatenate([jnp.full((tk, tk), pen, F32) for _, pen in blocks], axis=0)
        a = jnp.exp2(z - c[:, :tk] - run).astype(BF16)
        for r, (start, _) in enumerate(blocks):
            acc_ref[sub(r), :] += jnp.dot(a[sub(r), :], v_ref[0, pl.ds(start, tk), :], preferred_element_type=F32)
        run_ref[...] = run + c[:, tk:]

    lags = [key_blocks(lag) for lag in range(SB_STATIC_LAGS)]
    staged = [scores(blocks, lag == 0) for lag, blocks in enumerate(lags)]
    for lag, (blocks, (z, c)) in enumerate(zip(lags, staged)):
        accumulate(blocks, z, c, lag == 0)

    def cond(carry):
        lag, min_run = carry
        return jnp.logical_and(lag <= i * n_sub + n_sub - 1, min_run < SB_DEAD_BITS)

    def body(carry):
        lag, _ = carry
        blocks = key_blocks(lag)
        z, c = scores(blocks, False)
        accumulate(blocks, z, c, False)
        return lag + 1, jnp.min(run_ref[...])

    lax.while_loop(cond, body, (jnp.int32(SB_STATIC_LAGS), jnp.min(run_ref[...])))
    o_ref[0] = acc_ref[...].astype(o_ref.dtype)


def _stick_breaking(qkv, tq, tk):
    b, s, _ = qkv.shape
    kern = functools.partial(_sb_kernel, tq=tq, tk=tk)
    return pl.pallas_call(
        kern,
        grid=(b, SB_HEADS, s // tq),
        in_specs=[
            pl.BlockSpec((1, tq, GROUP_DIM), lambda bi, h, i: (bi, i, h)),
            pl.BlockSpec((1, s, GROUP_DIM), lambda bi, h, i: (bi, 0, SB_HEADS + h)),
            pl.BlockSpec((1, s, GROUP_DIM), lambda bi, h, i: (bi, 0, 2 * SB_HEADS + h)),
            pl.BlockSpec((2 * tk, 2 * tk), lambda bi, h, i: (0, 0)),
        ],
        out_specs=pl.BlockSpec((1, tq, GROUP_DIM), lambda bi, h, i: (bi, i, h)),
        out_shape=jax.ShapeDtypeStruct((b, s, SB_WIDTH), BF16),
        scratch_shapes=[pltpu.VMEM((tq, GROUP_DIM), F32), pltpu.VMEM((tq, tk), F32)],
        compiler_params=_cparams(("arbitrary", "arbitrary", "arbitrary")),
        name="stick_breaking",
    )(qkv, qkv, qkv, _sb_cumsum_matrix(tk))


def _gla_constants():
    c = GLA_CHUNK
    t = np.arange(c)[:, None]
    j = np.arange(c)[None, :]
    dst = np.concatenate([j <= t, np.ones((8, c), bool)], axis=0).astype(np.float32)
    s = j
    masks = []
    for l in range(GLA_LEVELS):
        masks.append(((t ^ s) >> l == 1) & (t > s))
    masks.append(t == s)
    return jnp.asarray(dst, dtype=BF16), jnp.asarray(np.stack(masks).astype(np.float32))


def _split3(x):
    x1 = x.astype(BF16)
    r = x - x1.astype(F32)
    x2 = r.astype(BF16)
    x3 = (r - x2.astype(F32)).astype(BF16)
    return x1, x2, x3


def _gla_kernel(q_ref, k_ref, v_ref, r_ref, a_ref, wa_ref, ba_ref, og_ref, dst_ref, msk_ref,
                o_ref, st_ref):
    @pl.when(pl.program_id(1) == 0)
    def _():
        st_ref[...] = jnp.zeros(st_ref.shape, F32)

    for cc in range(q_ref.shape[1] // GLA_CHUNK):
        _gla_chunk(slice(cc * GLA_CHUNK, (cc + 1) * GLA_CHUNK), q_ref, k_ref, v_ref, r_ref, a_ref,
                   wa_ref, ba_ref, og_ref, dst_ref, msk_ref, o_ref, st_ref)


def _gla_chunk(rows, q_ref, k_ref, v_ref, r_ref, a_ref, wa_ref, ba_ref, og_ref, dst_ref, msk_ref,
               o_ref, st_ref):
    c = GLA_CHUNK
    x = jnp.dot(a_ref[0, rows, :].astype(BF16), wa_ref[...], preferred_element_type=F32) + ba_ref[...]
    log_sig = jnp.minimum(x, 0.0) - jnp.log(1.0 + jnp.exp(-jnp.abs(x)))
    g = log_sig * (LOG2E / GLA_TAU)
    dst = dst_ref[...]
    g1, g2, g3 = _split3(g)
    sums = (jnp.dot(dst, g1, preferred_element_type=F32)
            + jnp.dot(dst, g2, preferred_element_type=F32)
            + jnp.dot(dst, g3, preferred_element_type=F32))
    prefix = sums[:c]
    total = sums[c:c + 1]
    e_in = prefix
    e_out = total - prefix

    row = lax.broadcasted_iota(jnp.int32, g.shape, 0)
    pos = row & 3
    g_prev = pltpu.roll(g, 1, 0)
    g_next = pltpu.roll(g, c - 1, 0)
    e_lvl = [jnp.where((row & 1) == 1, g, 0.0),
             jnp.where(pos == 0, g_next, jnp.where(pos == 1, 0.0, jnp.where(pos == 2, g, g_prev + g)))]
    for l in range(2, GLA_LEVELS):
        m = 1 << l
        blocks = prefix.reshape(c // (2 * m), 2 * m, GLA_WIDTH)
        e_lvl.append((-jnp.abs(blocks - blocks[:, m - 1:m, :])).reshape(c, GLA_WIDTH))

    for h in range(GLA_HEADS):
        cols = slice(h * GROUP_DIM, (h + 1) * GROUP_DIM)
        q = q_ref[0, rows, cols] * (GROUP_DIM ** -0.5)
        k = k_ref[0, rows, cols]
        v = v_ref[0, rows, cols].astype(BF16)
        scores = msk_ref[GLA_LEVELS] * _nt_dot(q.astype(BF16), k.astype(BF16))
        for l in range(GLA_LEVELS):
            xl = jnp.exp2(e_lvl[l][:, cols])
            scores = scores + msk_ref[l] * _nt_dot((q * xl).astype(BF16), (k * xl).astype(BF16))
        o = jnp.dot(scores.astype(BF16), v, preferred_element_type=F32)
        x_in = jnp.exp2(e_in[:, cols])
        st = st_ref[h]
        o = o + _nt_dot((q * x_in).astype(BF16), st.astype(BF16))
        x_out = jnp.exp2(e_out[:, cols])
        x_all = jnp.exp2(total[:, cols])
        vt = v_ref[0, rows, cols].T.astype(BF16)
        st_ref[h] = st * x_all + jnp.dot(vt, (k * x_out).astype(BF16), preferred_element_type=F32)
        ms = jnp.mean(o * o, axis=-1, keepdims=True)
        o = o * lax.rsqrt(ms + RMS_EPS) * og_ref[...]
        o_ref[0, rows, cols] = (o * _silu(r_ref[0, rows, cols])).astype(o_ref.dtype)


def _gla(proj3, w_a2p, b_a2, out_gain, chunks_per_step):
    b, s, _ = proj3.shape
    c = GLA_CHUNK * chunks_per_step
    dst, masks = _gla_constants()
    wide = lambda blk: pl.BlockSpec((1, c, GLA_WIDTH), lambda bi, ci: (bi, ci, blk))
    const2 = lambda shape: pl.BlockSpec(shape, lambda bi, ci: (0, 0))
    a_blk = (POOL_WIDTH + 4 * GLA_WIDTH) // GROUP_DIM
    return pl.pallas_call(
        _gla_kernel,
        grid=(b, s // c),
        in_specs=[
            wide(1), wide(2), wide(3), wide(4),
            pl.BlockSpec((1, c, GROUP_DIM), lambda bi, ci: (bi, ci, a_blk)),
            const2((GROUP_DIM, GLA_WIDTH)),
            const2((1, GLA_WIDTH)),
            const2((1, GROUP_DIM)),
            const2(dst.shape),
            pl.BlockSpec(masks.shape, lambda bi, ci: (0, 0, 0)),
        ],
        out_specs=pl.BlockSpec((1, c, GLA_WIDTH), lambda bi, ci: (bi, ci, 0)),
        out_shape=jax.ShapeDtypeStruct((b, s, GLA_WIDTH), BF16),
        scratch_shapes=[pltpu.VMEM((GLA_HEADS, GROUP_DIM, GROUP_DIM), F32)],
        compiler_params=_cparams(("arbitrary", "arbitrary")),
        name="gla",
    )(proj3, proj3, proj3, proj3, proj3, w_a2p, b_a2, out_gain, dst, masks)


def _merge_kernel(h_ref, yp_ref, ys_ref, yg_ref, wgp_ref, wgs_ref, wgg_ref, wp_ref, ws_ref, wg_ref, o_ref):
    h = h_ref[...]

    def branch(wgate_ref, y_ref, w_ref):
        gate = jax.nn.sigmoid(jnp.dot(h, wgate_ref[...], preferred_element_type=F32))
        return gate * jnp.dot(y_ref[...], w_ref[...], preferred_element_type=F32)

    merged = branch(wgp_ref, yp_ref, wp_ref) + branch(wgs_ref, ys_ref, ws_ref) + branch(wgg_ref, yg_ref, wg_ref)
    o_ref[...] = merged.astype(o_ref.dtype)


def _merge(h, y_pool, y_sb, y_gla, w_gates, w_br_pool, w_br_sb, w_br_gla, *, tm, tn):
    m, d = h.shape
    nj = d // tn
    row = lambda width: pl.BlockSpec((tm, width), lambda i, j: (i, 0))
    gate_w = lambda br: pl.BlockSpec((d, tn), lambda i, j: (0, br * nj + j))
    br_w = lambda width: pl.BlockSpec((width, tn), lambda i, j: (0, j))
    return pl.pallas_call(
        _merge_kernel,
        grid=(m // tm, nj),
        in_specs=[
            row(d), row(POOL_WIDTH), row(SB_WIDTH), row(GLA_WIDTH),
            gate_w(0), gate_w(1), gate_w(2),
            br_w(POOL_WIDTH), br_w(SB_WIDTH), br_w(GLA_WIDTH),
        ],
        out_specs=pl.BlockSpec((tm, tn), lambda i, j: (i, j)),
        out_shape=jax.ShapeDtypeStruct((m, d), BF16),
        compiler_params=_cparams(("arbitrary", "arbitrary")),
        name="gated_merge",
    )(h, y_pool, y_sb, y_gla, w_gates, w_gates, w_gates, w_br_pool, w_br_sb, w_br_gla)


def _resproj_kernel(a_ref, w_ref, x_ref, ga_ref, g_ref, sc_ref, sh_ref, o_ref, h_ref):
    for r in range(a_ref.shape[0] // ROW_CHUNK):
        rows = slice(r * ROW_CHUNK, (r + 1) * ROW_CHUNK)
        y = jnp.dot(a_ref[rows, :], w_ref[...], preferred_element_type=F32)
        x_new = x_ref[rows, :] + ga_ref[0] * y
        o_ref[rows, :] = x_new
        h_ref[rows, :] = _modulated_norm(x_new, g_ref[...], sc_ref[0], sh_ref[0]).astype(BF16)


def _resproj(a, w, x2, gate, gain, scale, shift, *, seq, tm):
    m, kdim = a.shape
    d = w.shape[1]
    tpb = seq // tm
    row = lambda width: pl.BlockSpec((tm, width), lambda i: (i, 0))
    mod = pl.BlockSpec((1, 1, d), lambda i: (i // tpb, 0, 0))
    return pl.pallas_call(
        _resproj_kernel,
        grid=(m // tm,),
        in_specs=[row(kdim), pl.BlockSpec((kdim, d), lambda i: (0, 0)), row(d), mod,
                  pl.BlockSpec((1, d), lambda i: (0, 0)), mod, mod],
        out_specs=[row(d), row(d)],
        out_shape=[jax.ShapeDtypeStruct((m, d), F32), jax.ShapeDtypeStruct((m, d), BF16)],
        compiler_params=_cparams(("arbitrary",)),
        name="residual_proj",
    )(a, w, x2, gate, gain, scale, shift)


def _swiglu_kernel(*refs, emit_next, x_chunks):
    if emit_next:
        h_ref, x_ref, ga_ref, wg_ref, wu_ref, wd_ref, g_ref, sc_ref, sh_ref, o_ref, hn_ref = refs
    else:
        h_ref, x_ref, ga_ref, wg_ref, wu_ref, wd_ref, o_ref = refs
    f = pl.program_id(1)
    last = pl.num_programs(1) - 1
    xw = x_ref.shape[1]

    def gated_partial(rows):
        h = h_ref[rows, :]
        gate = jnp.dot(h, wg_ref[...], preferred_element_type=F32)
        up = jnp.dot(h, wu_ref[...], preferred_element_type=F32)
        act = (_silu(gate) * up).astype(BF16)
        return ga_ref[0] * jnp.dot(act, wd_ref[...], preferred_element_type=F32)

    row_chunks = [slice(r * ROW_CHUNK, (r + 1) * ROW_CHUNK) for r in range(h_ref.shape[0] // ROW_CHUNK)]

    @pl.when(f == 0)
    def _():
        for rows in row_chunks:
            o_ref[rows, :] = gated_partial(rows)

    @pl.when(jnp.logical_and(f > 0, f < last))
    def _():
        for rows in row_chunks:
            o_ref[rows, :] += gated_partial(rows)

    for c in range(x_chunks):
        @pl.when(f == c)
        def _():
            o_ref[:, c * xw:(c + 1) * xw] += x_ref[...]

    @pl.when(f == last)
    def _():
        for rows in row_chunks:
            x_new = o_ref[rows, :] + gated_partial(rows)
            o_ref[rows, :] = x_new
            if emit_next:
                hn_ref[rows, :] = _modulated_norm(x_new, g_ref[...], sc_ref[0], sh_ref[0]).astype(BF16)


def _swiglu(h, x2, gate, w_gate, w_up, w_down, next_norm, *, seq, tm, tf):
    m, d = x2.shape
    dff = w_gate.shape[1]
    tpb = seq // tm
    n_steps = dff // tf
    x_chunks = 8
    assert x_chunks < n_steps and d % (x_chunks * GROUP_DIM) == 0
    emit_next = next_norm is not None
    row = pl.BlockSpec((tm, d), lambda i, f: (i, 0))
    mod = pl.BlockSpec((1, 1, d), lambda i, f: (i // tpb, 0, 0))
    in_specs = [row, pl.BlockSpec((tm, d // x_chunks), lambda i, f: (i, jnp.minimum(f, x_chunks - 1))), mod,
                pl.BlockSpec((d, tf), lambda i, f: (0, f)),
                pl.BlockSpec((d, tf), lambda i, f: (0, f)),
                pl.BlockSpec((tf, d), lambda i, f: (f, 0))]
    args = [h, x2, gate, w_gate, w_up, w_down]
    out_specs = [row]
    out_shape = [jax.ShapeDtypeStruct((m, d), F32)]
    if emit_next:
        in_specs += [pl.BlockSpec((1, d), lambda i, f: (0, 0)), mod, mod]
        args += list(next_norm)
        out_specs.append(row)
        out_shape.append(jax.ShapeDtypeStruct((m, d), BF16))
    outs = pl.pallas_call(
        functools.partial(_swiglu_kernel, emit_next=emit_next, x_chunks=x_chunks),
        grid=(m // tm, n_steps),
        in_specs=in_specs,
        out_specs=out_specs,
        out_shape=out_shape,
        compiler_params=_cparams(("arbitrary", "arbitrary")),
        name="swiglu",
    )(*args)
    return (outs[0], outs[1]) if emit_next else (outs[0], None)


class _Tiles(NamedTuple):
    rows: int
    rows_f32: int
    pool_rows: int
    qkv_cols: int
    merge_cols: int
    ff_cols: int
    gla_chunks: int


def _tile(n, pref):
    t = min(n, pref)
    assert n % t == 0, (n, t)
    return t


def _tiles(seq):
    return _Tiles(rows=_tile(seq, 1024), rows_f32=_tile(seq, 512), pool_rows=_tile(seq, 2048),
                  qkv_cols=1024, merge_cols=512, ff_cols=512,
                  gla_chunks=_tile(seq // GLA_CHUNK, 4))


def _mod_vectors(mod_l):
    d = D_MODEL
    return [mod_l[:, None, k * d:(k + 1) * d] for k in range(6)]


def _layer(x2, h1, mod_l, next_norm, batch, seq, w_in, w_pool, pool_scale, sb_q_gain, sb_k_gain,
           gla_w_a2, gla_b_a2, gla_out_gain, w_br_pool, w_br_sb, w_br_gla, w_out,
           g_norm2, w_ff_gate, w_ff_up, w_ff_down):
    d = D_MODEL
    t = _tiles(seq)
    _, _, ga1, sh2, sc2, ga2 = _mod_vectors(mod_l)

    o_sb = POOL_WIDTH
    o_gla = o_sb + 3 * SB_WIDTH
    o_a = o_gla + 4 * GLA_WIDTH
    o_gate = o_a + GLA_RANK
    w_f32cols = jnp.concatenate(
        [w_in[:, :o_sb], w_in[:, o_gla:o_a], jnp.pad(w_in[:, o_a:o_gate], ((0, 0), (0, MXU_WIDTH - GLA_RANK)))],
        axis=1)
    w_sb = w_in[:, o_sb:o_gla]
    w_gates = w_in[:, o_gate:]

    n_f32 = w_f32cols.shape[1]
    proj = _proj(h1, w_f32cols, jnp.ones((1, n_f32), F32), tm=t.rows_f32, tn=n_f32, n_norm_tiles=0,
                 out_dtype=F32)
    q_scale = LOG2E / math.sqrt(GROUP_DIM)
    colgain = jnp.concatenate([jnp.tile(sb_q_gain * q_scale, SB_HEADS), jnp.tile(sb_k_gain, SB_HEADS),
                               jnp.ones((SB_WIDTH,), F32)]).reshape(1, 3 * SB_WIDTH)
    assert (2 * SB_WIDTH) % t.qkv_cols == 0
    qkv = _proj(h1, w_sb, colgain, tm=t.rows, tn=t.qkv_cols, n_norm_tiles=2 * SB_WIDTH // t.qkv_cols,
                out_dtype=BF16)

    proj3 = proj.reshape(batch, seq, n_f32)
    y_pool = _pool(proj3, w_pool, pool_scale, ts=t.pool_rows)
    y_sb = _stick_breaking(qkv.reshape(batch, seq, 3 * SB_WIDTH), tq=t.rows, tk=GROUP_DIM)
    w_a2p = jnp.pad(gla_w_a2, ((0, GROUP_DIM - GLA_RANK), (0, 0)))
    y_gla = _gla(proj3, w_a2p, gla_b_a2.reshape(1, GLA_WIDTH), gla_out_gain.reshape(1, GROUP_DIM),
                 chunks_per_step=t.gla_chunks)

    m = batch * seq
    merged = _merge(h1, y_pool.reshape(m, POOL_WIDTH), y_sb.reshape(m, SB_WIDTH), y_gla.reshape(m, GLA_WIDTH),
                    w_gates, w_br_pool, w_br_sb, w_br_gla, tm=t.rows, tn=t.merge_cols)
    x2, h2 = _resproj(merged, w_out, x2, ga1, g_norm2.reshape(1, d), sc2, sh2, seq=seq, tm=t.rows_f32)
    return _swiglu(h2, x2, ga2, w_ff_gate, w_ff_up, w_ff_down, next_norm, seq=seq, tm=t.rows, tf=t.ff_cols)


def kernel(x, c, w_ada, b_ada, g_norm1, w_in, w_pool, pool_scale, sb_q_gain, sb_k_gain, gla_w_a2, gla_b_a2, gla_out_gain, w_br_pool, w_br_sb, w_br_gla, w_out, g_norm2, w_ff_gate, w_ff_up, w_ff_down):
    batch, seq, d = x.shape
    depth = w_ada.shape[0]
    c_pad = jnp.pad(c, ((0, 8 - batch % 8 if batch % 8 else 0), (0, 0)))
    mod = _ada(c_pad, w_ada, b_ada)[:, :batch]
    x2 = x.reshape(batch * seq, d)
    (w_in, w_pool, gla_w_a2, w_br_pool, w_br_sb, w_br_gla, w_out, w_ff_gate, w_ff_up, w_ff_down) = [
        w.astype(BF16) for w in (w_in, w_pool, gla_w_a2, w_br_pool, w_br_sb, w_br_gla, w_out,
                                 w_ff_gate, w_ff_up, w_ff_down)]

    def first_norm(l):
        sh1, sc1 = _mod_vectors(mod[l])[:2]
        return g_norm1[l].reshape(1, d), sc1, sh1

    h1 = _norm(x2, *first_norm(0), seq=seq, tm=_tiles(seq).rows_f32)
    for l in range(depth):
        next_norm = first_norm(l + 1) if l + 1 < depth else None
        x2, h1 = _layer(x2, h1, mod[l], next_norm, batch, seq, w_in[l], w_pool[l], pool_scale[l],
                        sb_q_gain[l], sb_k_gain[l], gla_w_a2[l], gla_b_a2[l], gla_out_gain[l],
                        w_br_pool[l], w_br_sb[l], w_br_gla[l], w_out[l], g_norm2[l],
                        w_ff_gate[l], w_ff_up[l], w_ff_down[l])
    return x2.reshape(batch, seq, d)
```

```python
import functools
import math
from typing import NamedTuple

import numpy as np
import jax
import jax.numpy as jnp
from jax import lax
from jax.experimental import pallas as pl
from jax.experimental.pallas import tpu as pltpu

F32 = jnp.float32
BF16 = jnp.bfloat16

D_MODEL = 2048
POOL_WINDOWS = (2, 4, 8, 16)
POOL_GROUPS = 4
GROUP_DIM = 128
POOL_WIDTH = POOL_GROUPS * GROUP_DIM
SB_HEADS = 8
SB_WIDTH = SB_HEADS * GROUP_DIM
GLA_HEADS = 4
GLA_WIDTH = GLA_HEADS * GROUP_DIM
GLA_RANK = 16
GLA_TAU = 16.0
GLA_CHUNK = 128
GLA_LEVELS = 7
N_BRANCH = 3
D_FF = 5632
RMS_EPS = 1e-6
LOG2E = 1.4426950408889634
POOL_HALO = 16
SB_DEAD_BITS = 160.0
SB_MASKED_BITS = 1.0e4
SB_STATIC_LAGS = 3

VMEM_LIMIT = 56 * 1024 * 1024
MXU_WIDTH = 256
ROW_CHUNK = 256


def _cparams(sem):
    return pltpu.CompilerParams(dimension_semantics=sem, vmem_limit_bytes=VMEM_LIMIT)


def _nt_dot(a, b):
    return lax.dot_general(a, b, (((1,), (1,)), ((), ())), preferred_element_type=F32)


def _silu(x):
    return x * jax.nn.sigmoid(x)


def _ada_kernel(c_ref, w_ref, b_ref, o_ref):
    c = c_ref[...]
    a = _silu(c).astype(BF16)
    o_ref[0] = jnp.dot(a, w_ref[0].astype(BF16), preferred_element_type=F32) + b_ref[0]


def _ada(c_pad, w_ada, b_ada, tn=1024):
    depth, d, n = w_ada.shape
    rows = c_pad.shape[0]
    return pl.pallas_call(
        _ada_kernel,
        grid=(depth, n // tn),
        in_specs=[
            pl.BlockSpec((rows, d), lambda l, j: (0, 0)),
            pl.BlockSpec((1, d, tn), lambda l, j: (l, 0, j)),
            pl.BlockSpec((1, 1, tn), lambda l, j: (l, 0, j)),
        ],
        out_specs=pl.BlockSpec((1, rows, tn), lambda l, j: (l, 0, j)),
        out_shape=jax.ShapeDtypeStruct((depth, rows, n), F32),
        compiler_params=_cparams(("arbitrary", "arbitrary")),
        name="ada_modulation",
    )(c_pad, w_ada, b_ada.reshape(depth, 1, n))


W_IN_SB = POOL_WIDTH
W_IN_GLA = W_IN_SB + 3 * SB_WIDTH
W_IN_A = W_IN_GLA + 4 * GLA_WIDTH
W_IN_GATE = W_IN_A + GLA_RANK
W_IN_COLS = W_IN_GATE + N_BRANCH * D_MODEL
A_LOW_COL = POOL_WIDTH + 4 * GLA_WIDTH
F32_COLS = A_LOW_COL + MXU_WIDTH


def _split_w_in_kernel(w_ref, f_ref, sb_ref, g_ref):
    f_ref[0, :, :POOL_WIDTH] = w_ref[0, :, :W_IN_SB].astype(BF16)
    f_ref[0, :, POOL_WIDTH:A_LOW_COL] = w_ref[0, :, W_IN_GLA:W_IN_A].astype(BF16)
    f_ref[0, :, A_LOW_COL:] = jnp.zeros((f_ref.shape[1], MXU_WIDTH), BF16)
    f_ref[0, :, A_LOW_COL:A_LOW_COL + GLA_RANK] = w_ref[0, :, W_IN_A:W_IN_GATE].astype(BF16)
    sb_ref[0] = w_ref[0, :, W_IN_SB:W_IN_GLA].astype(BF16)
    g_ref[0] = w_ref[0, :, W_IN_GATE:].astype(BF16)


def _split_w_in(w_in, tr=256):
    depth, d, n = w_in.shape
    assert n == W_IN_COLS
    widths = (F32_COLS, 3 * SB_WIDTH, N_BRANCH * D_MODEL)
    return pl.pallas_call(
        _split_w_in_kernel,
        grid=(depth, d // tr),
        in_specs=[pl.BlockSpec((1, tr, n), lambda l, i: (l, i, 0))],
        out_specs=[pl.BlockSpec((1, tr, w), lambda l, i: (l, i, 0)) for w in widths],
        out_shape=[jax.ShapeDtypeStruct((depth, d, w), BF16) for w in widths],
        compiler_params=_cparams(("arbitrary", "arbitrary")),
        name="split_w_in",
    )(w_in)


def _modulated_norm(x, gain, scale, shift):
    ms = jnp.mean(x * x, axis=-1, keepdims=True)
    y = x * lax.rsqrt(ms + RMS_EPS) * gain
    return y * (1.0 + scale) + shift


def _group_rmsnorm(acc, colgain):
    out = []
    for c in range(acc.shape[1] // GROUP_DIM):
        sl = slice(c * GROUP_DIM, (c + 1) * GROUP_DIM)
        blk = acc[:, sl]
        ms = jnp.mean(blk * blk, axis=-1, keepdims=True)
        out.append(blk * lax.rsqrt(ms + RMS_EPS) * colgain[:, sl])
    return jnp.concatenate(out, axis=1)


def _norm_kernel(x_ref, g_ref, sc_ref, sh_ref, h_ref):
    h_ref[...] = _modulated_norm(x_ref[...], g_ref[...], sc_ref[0], sh_ref[0]).astype(BF16)


def _norm(x2, gain, scale, shift, *, seq, tm):
    m, d = x2.shape
    tpb = seq // tm
    mod = pl.BlockSpec((1, 1, d), lambda i: (i // tpb, 0, 0))
    return pl.pallas_call(
        _norm_kernel,
        grid=(m // tm,),
        in_specs=[pl.BlockSpec((tm, d), lambda i: (i, 0)), pl.BlockSpec((1, d), lambda i: (0, 0)), mod, mod],
        out_specs=pl.BlockSpec((tm, d), lambda i: (i, 0)),
        out_shape=jax.ShapeDtypeStruct((m, d), BF16),
        compiler_params=_cparams(("arbitrary",)),
        name="modulated_norm",
    )(x2, gain, scale, shift)


def _proj_kernel(h_ref, w_ref, cg_ref, o_ref, *, n_norm_tiles, tn):
    j = pl.program_id(1)
    h = h_ref[...]

    def tile(normed):
        for c in range(tn // MXU_WIDTH):
            cols = slice(c * MXU_WIDTH, (c + 1) * MXU_WIDTH)
            acc = jnp.dot(h, w_ref[:, cols], preferred_element_type=F32)
            if normed:
                acc = _group_rmsnorm(acc, cg_ref[:, cols])
            o_ref[:, cols] = acc.astype(o_ref.dtype)

    if n_norm_tiles == 0:
        tile(False)
    else:
        pl.when(j < n_norm_tiles)(lambda: tile(True))
        pl.when(j >= n_norm_tiles)(lambda: tile(False))


def _proj(h, w, layer, colgain, *, tm, tn, n_norm_tiles, out_dtype):
    m, d = h.shape
    n = w.shape[2]
    kern = functools.partial(_proj_kernel, n_norm_tiles=n_norm_tiles, tn=tn)
    return pl.pallas_call(
        kern,
        grid=(m // tm, n // tn),
        in_specs=[
            pl.BlockSpec((tm, d), lambda i, j: (i, 0)),
            pl.BlockSpec((None, d, tn), lambda i, j: (layer, 0, j)),
            pl.BlockSpec((1, tn), lambda i, j: (0, j)),
        ],
        out_specs=pl.BlockSpec((tm, tn), lambda i, j: (i, j)),
        out_shape=jax.ShapeDtypeStruct((m, n), out_dtype),
        compiler_params=_cparams(("arbitrary", "arbitrary")),
        name="in_proj",
    )(h, w, colgain)


def _pool_kernel(u_ref, w_ref, ps_ref, o_ref, ext_ref, *, ts):
    i = pl.program_id(1)

    @pl.when(i == 0)
    def _():
        ext_ref[0:POOL_HALO, :] = jnp.zeros((POOL_HALO, POOL_WIDTH), F32)

    @pl.when(i > 0)
    def _():
        ext_ref[0:POOL_HALO, :] = ext_ref[ts:ts + POOL_HALO, :]

    ext_ref[POOL_HALO:, :] = u_ref[0]
    pos1 = (i * ts + 1 + lax.broadcasted_iota(jnp.int32, (ts, GROUP_DIM), 0)).astype(F32)
    for g, w in enumerate(POOL_WINDOWS):
        cols = slice(g * GROUP_DIM, (g + 1) * GROUP_DIM)
        u = ext_ref[POOL_HALO:, cols]
        win = u
        for k in range(1, w):
            win = win + ext_ref[POOL_HALO - k:POOL_HALO - k + ts, cols]
        pooled = win / jnp.minimum(pos1, float(w)) - u
        y = jnp.dot(pooled.astype(BF16), w_ref[g], preferred_element_type=F32) * ps_ref[g]
        o_ref[0, :, cols] = y.astype(o_ref.dtype)


def _pool(proj3, w_pool, pool_scale, ts):
    b, s, _ = proj3.shape
    kern = functools.partial(_pool_kernel, ts=ts)
    return pl.pallas_call(
        kern,
        grid=(b, s // ts),
        in_specs=[
            pl.BlockSpec((1, ts, POOL_WIDTH), lambda bi, i: (bi, i, 0)),
            pl.BlockSpec((POOL_GROUPS, GROUP_DIM, GROUP_DIM), lambda bi, i: (0, 0, 0)),
            pl.BlockSpec((POOL_GROUPS, 1, GROUP_DIM), lambda bi, i: (0, 0, 0)),
        ],
        out_specs=pl.BlockSpec((1, ts, POOL_WIDTH), lambda bi, i: (bi, i, 0)),
        out_shape=jax.ShapeDtypeStruct((b, s, POOL_WIDTH), BF16),
        scratch_shapes=[pltpu.VMEM((ts + POOL_HALO, POOL_WIDTH), F32)],
        compiler_params=_cparams(("arbitrary", "arbitrary")),
        name="pool_mixer",
    )(proj3, w_pool, pool_scale.reshape(POOL_GROUPS, 1, GROUP_DIM))


def _sb_cumsum_matrix(tk):
    j = np.arange(tk)[:, None]
    s = np.arange(tk)[None, :]
    one = np.concatenate([(j >= s).astype(np.float32), np.ones((tk, tk), np.float32)], axis=1)
    return jnp.asarray(np.concatenate([one, one], axis=0), dtype=BF16)


def _sb_kernel(q_ref, k_ref, v_ref, w2_ref, o_ref, acc_ref, run_ref, *, tq, tk):
    i = pl.program_id(2)
    w2 = w2_ref[...]
    n_sub = tq // tk
    acc_ref[...] = jnp.zeros((tq, GROUP_DIM), F32)
    run_ref[...] = jnp.zeros((tq, tk), F32)
    row = lax.broadcasted_iota(jnp.int32, (tq, tk), 0)
    col = lax.broadcasted_iota(jnp.int32, (tq, tk), 1)
    diag_mask = col < (row & (tk - 1))
    sub = lambda r: slice(r * tk, (r + 1) * tk)

    def key_blocks(lag):
        blocks = []
        for r in range(n_sub):
            jb = i * n_sub + r - lag
            start = pl.multiple_of(jnp.maximum(jb, 0) * tk, tk)
            blocks.append((start, jnp.where(jb >= 0, 0.0, SB_MASKED_BITS)))
        return blocks

    def scores(blocks, diagonal):
        z = jnp.concatenate([_nt_dot(q_ref[0, sub(r), :], k_ref[0, pl.ds(start, tk), :])
                             for r, (start, _) in enumerate(blocks)], axis=0)
        neg_abs = lax.bitcast_convert_type(
            lax.bitcast_convert_type(z, jnp.uint32) | jnp.uint32(0x80000000), F32)
        neg_log_keep = jnp.maximum(z, 0.0) + jnp.log(1.0 + jnp.exp2(neg_abs)) * LOG2E
        if diagonal:
            neg_log_keep = jnp.where(diag_mask, neg_log_keep, 0.0)
            z = jnp.where(diag_mask, z, -SB_MASKED_BITS)
        hi = lax.bitcast_convert_type(
            lax.bitcast_convert_type(neg_log_keep, jnp.uint32) & jnp.uint32(0xFFFF0000), F32)
        lo = neg_log_keep - hi
        c = jnp.dot(jnp.concatenate([hi.astype(BF16), lo.astype(BF16)], axis=1), w2, preferred_element_type=F32)
        return z, c

    def accumulate(blocks, z, c, diagonal):
        run = run_ref[...]
        if not diagonal:
            run = run + jnp.concatenate([jnp.full((tk, tk), pen, F32) for _, pen in blocks], axis=0)
        a = jnp.exp2(z - c[:, :tk] - run).astype(BF16)
        for r, (start, _) in enumerate(blocks):
            acc_ref[sub(r), :] += jnp.dot(a[sub(r), :], v_ref[0, pl.ds(start, tk), :], preferred_element_type=F32)
        run_ref[...] = run + c[:, tk:]

    lags = [key_blocks(lag) for lag in range(SB_STATIC_LAGS)]
    staged = [scores(blocks, lag == 0) for lag, blocks in enumerate(lags)]
    for lag, (blocks, (z, c)) in enumerate(zip(lags, staged)):
        accumulate(blocks, z, c, lag == 0)

    def cond(carry):
        lag, min_run = carry
        return jnp.logical_and(lag <= i * n_sub + n_sub - 1, min_run < SB_DEAD_BITS)

    def body(carry):
        lag, _ = carry
        blocks = key_blocks(lag)
        z, c = scores(blocks, False)
        accumulate(blocks, z, c, False)
        return lag + 1, jnp.min(run_ref[...])

    lax.while_loop(cond, body, (jnp.int32(SB_STATIC_LAGS), jnp.min(run_ref[...])))
    o_ref[0] = acc_ref[...].astype(o_ref.dtype)


def _stick_breaking(qkv, tq, tk):
    b, s, _ = qkv.shape
    kern = functools.partial(_sb_kernel, tq=tq, tk=tk)
    return pl.pallas_call(
        kern,
        grid=(b, SB_HEADS, s // tq),
        in_specs=[
            pl.BlockSpec((1, tq, GROUP_DIM), lambda bi, h, i: (bi, i, h)),
            pl.BlockSpec((1, s, GROUP_DIM), lambda bi, h, i: (bi, 0, SB_HEADS + h)),
            pl.BlockSpec((1, s, GROUP_DIM), lambda bi, h, i: (bi, 0, 2 * SB_HEADS + h)),
            pl.BlockSpec((2 * tk, 2 * tk), lambda bi, h, i: (0, 0)),
        ],
        out_specs=pl.BlockSpec((1, tq, GROUP_DIM), lambda bi, h, i: (bi, i, h)),
        out_shape=jax.ShapeDtypeStruct((b, s, SB_WIDTH), BF16),
        scratch_shapes=[pltpu.VMEM((tq, GROUP_DIM), F32), pltpu.VMEM((tq, tk), F32)],
        compiler_params=_cparams(("arbitrary", "arbitrary", "arbitrary")),
        name="stick_breaking",
    )(qkv, qkv, qkv, _sb_cumsum_matrix(tk))


def _gla_constants():
    c = GLA_CHUNK
    t = np.arange(c)[:, None]
    j = np.arange(c)[None, :]
    dst = np.concatenate([j <= t, np.ones((8, c), bool)], axis=0).astype(np.float32)
    s = j
    masks = []
    for l in range(GLA_LEVELS):
        masks.append(((t ^ s) >> l == 1) & (t > s))
    masks.append(t == s)
    return jnp.asarray(dst, dtype=BF16), jnp.asarray(np.stack(masks).astype(np.float32))


def _split3(x):
    x1 = x.astype(BF16)
    r = x - x1.astype(F32)
    x2 = r.astype(BF16)
    x3 = (r - x2.astype(F32)).astype(BF16)
    return x1, x2, x3


def _gla_kernel(q_ref, k_ref, v_ref, r_ref, a_ref, wa_ref, ba_ref, og_ref, dst_ref, msk_ref,
                o_ref, st_ref):
    @pl.when(pl.program_id(1) == 0)
    def _():
        st_ref[...] = jnp.zeros(st_ref.shape, F32)

    for cc in range(q_ref.shape[1] // GLA_CHUNK):
        _gla_chunk(slice(cc * GLA_CHUNK, (cc + 1) * GLA_CHUNK), q_ref, k_ref, v_ref, r_ref, a_ref,
                   wa_ref, ba_ref, og_ref, dst_ref, msk_ref, o_ref, st_ref)


def _gla_chunk(rows, q_ref, k_ref, v_ref, r_ref, a_ref, wa_ref, ba_ref, og_ref, dst_ref, msk_ref,
               o_ref, st_ref):
    c = GLA_CHUNK
    x = jnp.dot(a_ref[0, rows, :].astype(BF16), wa_ref[...], preferred_element_type=F32) + ba_ref[...]
    log_sig = jnp.minimum(x, 0.0) - jnp.log(1.0 + jnp.exp(-jnp.abs(x)))
    g = log_sig * (LOG2E / GLA_TAU)
    dst = dst_ref[...]
    g1, g2, g3 = _split3(g)
    sums = (jnp.dot(dst, g1, preferred_element_type=F32)
            + jnp.dot(dst, g2, preferred_element_type=F32)
            + jnp.dot(dst, g3, preferred_element_type=F32))
    prefix = sums[:c]
    total = sums[c:c + 1]
    e_in = prefix
    e_out = total - prefix

    row = lax.broadcasted_iota(jnp.int32, g.shape, 0)
    pos = row & 3
    g_prev = pltpu.roll(g, 1, 0)
    g_next = pltpu.roll(g, c - 1, 0)
    e_lvl = [jnp.where((row & 1) == 1, g, 0.0),
             jnp.where(pos == 0, g_next, jnp.where(pos == 1, 0.0, jnp.where(pos == 2, g, g_prev + g)))]
    for l in range(2, GLA_LEVELS):
        m = 1 << l
        blocks = prefix.reshape(c // (2 * m), 2 * m, GLA_WIDTH)
        e_lvl.append((-jnp.abs(blocks - blocks[:, m - 1:m, :])).reshape(c, GLA_WIDTH))

    for h in range(GLA_HEADS):
        cols = slice(h * GROUP_DIM, (h + 1) * GROUP_DIM)
        q = q_ref[0, rows, cols] * (GROUP_DIM ** -0.5)
        k = k_ref[0, rows, cols]
        v = v_ref[0, rows, cols].astype(BF16)
        scores = msk_ref[GLA_LEVELS] * _nt_dot(q.astype(BF16), k.astype(BF16))
        for l in range(GLA_LEVELS):
            xl = jnp.exp2(e_lvl[l][:, cols])
            scores = scores + msk_ref[l] * _nt_dot((q * xl).astype(BF16), (k * xl).astype(BF16))
        o = jnp.dot(scores.astype(BF16), v, preferred_element_type=F32)
        x_in = jnp.exp2(e_in[:, cols])
        st = st_ref[h]
        o = o + _nt_dot((q * x_in).astype(BF16), st.astype(BF16))
        x_out = jnp.exp2(e_out[:, cols])
        x_all = jnp.exp2(total[:, cols])
        vt = v_ref[0, rows, cols].T.astype(BF16)
        st_ref[h] = st * x_all + jnp.dot(vt, (k * x_out).astype(BF16), preferred_element_type=F32)
        ms = jnp.mean(o * o, axis=-1, keepdims=True)
        o = o * lax.rsqrt(ms + RMS_EPS) * og_ref[...]
        o_ref[0, rows, cols] = (o * _silu(r_ref[0, rows, cols])).astype(o_ref.dtype)


def _gla(proj3, w_a2p, b_a2, out_gain, chunks_per_step):
    b, s, _ = proj3.shape
    c = GLA_CHUNK * chunks_per_step
    dst, masks = _gla_constants()
    wide = lambda blk: pl.BlockSpec((1, c, GLA_WIDTH), lambda bi, ci: (bi, ci, blk))
    const2 = lambda shape: pl.BlockSpec(shape, lambda bi, ci: (0, 0))
    a_blk = (POOL_WIDTH + 4 * GLA_WIDTH) // GROUP_DIM
    return pl.pallas_call(
        _gla_kernel,
        grid=(b, s // c),
        in_specs=[
            wide(1), wide(2), wide(3), wide(4),
            pl.BlockSpec((1, c, GROUP_DIM), lambda bi, ci: (bi, ci, a_blk)),
            const2((GROUP_DIM, GLA_WIDTH)),
            const2((1, GLA_WIDTH)),
            const2((1, GROUP_DIM)),
            const2(dst.shape),
            pl.BlockSpec(masks.shape, lambda bi, ci: (0, 0, 0)),
        ],
        out_specs=pl.BlockSpec((1, c, GLA_WIDTH), lambda bi, ci: (bi, ci, 0)),
        out_shape=jax.ShapeDtypeStruct((b, s, GLA_WIDTH), BF16),
        scratch_shapes=[pltpu.VMEM((GLA_HEADS, GROUP_DIM, GROUP_DIM), F32)],
        compiler_params=_cparams(("arbitrary", "arbitrary")),
        name="gla",
    )(proj3, proj3, proj3, proj3, proj3, w_a2p, b_a2, out_gain, dst, masks)


def _merge_kernel(h_ref, yp_ref, ys_ref, yg_ref, wgp_ref, wgs_ref, wgg_ref, wp_ref, ws_ref, wg_ref, o_ref):
    h = h_ref[...]

    def branch(wgate_ref, y_ref, w_ref):
        gate = jax.nn.sigmoid(jnp.dot(h, wgate_ref[...], preferred_element_type=F32))
        return gate * jnp.dot(y_ref[...], w_ref[...], preferred_element_type=F32)

    merged = branch(wgp_ref, yp_ref, wp_ref) + branch(wgs_ref, ys_ref, ws_ref) + branch(wgg_ref, yg_ref, wg_ref)
    o_ref[...] = merged.astype(o_ref.dtype)


def _merge(h, y_pool, y_sb, y_gla, w_gates, w_br_pool, w_br_sb, w_br_gla, layer, *, tm, tn):
    m, d = h.shape
    nj = d // tn
    row = lambda width: pl.BlockSpec((tm, width), lambda i, j: (i, 0))
    gate_w = lambda br: pl.BlockSpec((None, d, tn), lambda i, j: (layer, 0, br * nj + j))
    br_w = lambda width: pl.BlockSpec((None, width, tn), lambda i, j: (layer, 0, j))
    return pl.pallas_call(
        _merge_kernel,
        grid=(m // tm, nj),
        in_specs=[
            row(d), row(POOL_WIDTH), row(SB_WIDTH), row(GLA_WIDTH),
            gate_w(0), gate_w(1), gate_w(2),
            br_w(POOL_WIDTH), br_w(SB_WIDTH), br_w(GLA_WIDTH),
        ],
        out_specs=pl.BlockSpec((tm, tn), lambda i, j: (i, j)),
        out_shape=jax.ShapeDtypeStruct((m, d), BF16),
        compiler_params=_cparams(("arbitrary", "arbitrary")),
        name="gated_merge",
    )(h, y_pool, y_sb, y_gla, w_gates, w_gates, w_gates, w_br_pool, w_br_sb, w_br_gla)


def _resproj_kernel(a_ref, w_ref, x_ref, ga_ref, g_ref, sc_ref, sh_ref, o_ref, h_ref):
    for r in range(a_ref.shape[0] // ROW_CHUNK):
        rows = slice(r * ROW_CHUNK, (r + 1) * ROW_CHUNK)
        y = jnp.dot(a_ref[rows, :], w_ref[...], preferred_element_type=F32)
        x_new = x_ref[rows, :] + ga_ref[0] * y
        o_ref[rows, :] = x_new
        h_ref[rows, :] = _modulated_norm(x_new, g_ref[...], sc_ref[0], sh_ref[0]).astype(BF16)


def _resproj(a, w, layer, x2, gate, gain, scale, shift, *, seq, tm):
    m, kdim = a.shape
    d = w.shape[2]
    tpb = seq // tm
    row = lambda width: pl.BlockSpec((tm, width), lambda i: (i, 0))
    mod = pl.BlockSpec((1, 1, d), lambda i: (i // tpb, 0, 0))
    return pl.pallas_call(
        _resproj_kernel,
        grid=(m // tm,),
        in_specs=[row(kdim), pl.BlockSpec((None, kdim, d), lambda i: (layer, 0, 0)), row(d), mod,
                  pl.BlockSpec((1, d), lambda i: (0, 0)), mod, mod],
        out_specs=[row(d), row(d)],
        out_shape=[jax.ShapeDtypeStruct((m, d), F32), jax.ShapeDtypeStruct((m, d), BF16)],
        compiler_params=_cparams(("arbitrary",)),
        name="residual_proj",
    )(a, w, x2, gate, gain, scale, shift)


def _swiglu_kernel(*refs, emit_next, x_chunks):
    if emit_next:
        h_ref, x_ref, ga_ref, wg_ref, wu_ref, wd_ref, g_ref, sc_ref, sh_ref, o_ref, hn_ref = refs
    else:
        h_ref, x_ref, ga_ref, wg_ref, wu_ref, wd_ref, o_ref = refs
    f = pl.program_id(1)
    last = pl.num_programs(1) - 1
    xw = x_ref.shape[1]

    def gated_partial(rows):
        h = h_ref[rows, :]
        gate = jnp.dot(h, wg_ref[...], preferred_element_type=F32)
        up = jnp.dot(h, wu_ref[...], preferred_element_type=F32)
        act = (_silu(gate) * up).astype(BF16)
        return ga_ref[0] * jnp.dot(act, wd_ref[...], preferred_element_type=F32)

    row_chunks = [slice(r * ROW_CHUNK, (r + 1) * ROW_CHUNK) for r in range(h_ref.shape[0] // ROW_CHUNK)]

    @pl.when(f == 0)
    def _():
        for rows in row_chunks:
            o_ref[rows, :] = gated_partial(rows)

    @pl.when(jnp.logical_and(f > 0, f < last))
    def _():
        for rows in row_chunks:
            o_ref[rows, :] += gated_partial(rows)

    for c in range(x_chunks):
        @pl.when(f == c)
        def _():
            o_ref[:, c * xw:(c + 1) * xw] += x_ref[...]

    @pl.when(f == last)
    def _():
        for rows in row_chunks:
            x_new = o_ref[rows, :] + gated_partial(rows)
            o_ref[rows, :] = x_new
            if emit_next:
                hn_ref[rows, :] = _modulated_norm(x_new, g_ref[...], sc_ref[0], sh_ref[0]).astype(BF16)


def _swiglu(h, x2, gate, w_gate, w_up, w_down, layer, next_norm, *, seq, tm, tf):
    m, d = x2.shape
    dff = w_gate.shape[2]
    tpb = seq // tm
    n_steps = dff // tf
    x_chunks = 8
    assert x_chunks < n_steps and d % (x_chunks * GROUP_DIM) == 0
    emit_next = next_norm is not None
    row = pl.BlockSpec((tm, d), lambda i, f: (i, 0))
    mod = pl.BlockSpec((1, 1, d), lambda i, f: (i // tpb, 0, 0))
    in_specs = [row, pl.BlockSpec((tm, d // x_chunks), lambda i, f: (i, jnp.minimum(f, x_chunks - 1))), mod,
                pl.BlockSpec((None, d, tf), lambda i, f: (layer, 0, f)),
                pl.BlockSpec((None, d, tf), lambda i, f: (layer, 0, f)),
                pl.BlockSpec((None, tf, d), lambda i, f: (layer, f, 0))]
    args = [h, x2, gate, w_gate, w_up, w_down]
    out_specs = [row]
    out_shape = [jax.ShapeDtypeStruct((m, d), F32)]
    if emit_next:
        in_specs += [pl.BlockSpec((1, d), lambda i, f: (0, 0)), mod, mod]
        args += list(next_norm)
        out_specs.append(row)
        out_shape.append(jax.ShapeDtypeStruct((m, d), BF16))
    outs = pl.pallas_call(
        functools.partial(_swiglu_kernel, emit_next=emit_next, x_chunks=x_chunks),
        grid=(m // tm, n_steps),
        in_specs=in_specs,
        out_specs=out_specs,
        out_shape=out_shape,
        compiler_params=_cparams(("arbitrary", "arbitrary")),
        name="swiglu",
    )(*args)
    return (outs[0], outs[1]) if emit_next else (outs[0], None)


class _Tiles(NamedTuple):
    rows: int
    rows_f32: int
    pool_rows: int
    qkv_cols: int
    merge_cols: int
    ff_cols: int
    gla_chunks: int


def _tile(n, pref):
    t = min(n, pref)
    assert n % t == 0, (n, t)
    return t


def _tiles(seq):
    return _Tiles(rows=_tile(seq, 1024), rows_f32=_tile(seq, 512), pool_rows=_tile(seq, 2048),
                  qkv_cols=1024, merge_cols=512, ff_cols=512,
                  gla_chunks=_tile(seq // GLA_CHUNK, 4))


def _mod_vectors(mod_l):
    d = D_MODEL
    return [mod_l[:, None, k * d:(k + 1) * d] for k in range(6)]


class _Weights(NamedTuple):
    f32cols: jax.Array
    sb: jax.Array
    gates: jax.Array
    br_pool: jax.Array
    br_sb: jax.Array
    br_gla: jax.Array
    out: jax.Array
    ff_gate: jax.Array
    ff_up: jax.Array
    ff_down: jax.Array


def _layer(x2, h1, mod_l, next_norm, batch, seq, layer, w, w_pool, pool_scale, sb_q_gain, sb_k_gain,
           gla_w_a2, gla_b_a2, gla_out_gain, g_norm2):
    d = D_MODEL
    t = _tiles(seq)
    _, _, ga1, sh2, sc2, ga2 = _mod_vectors(mod_l)

    proj = _proj(h1, w.f32cols, layer, jnp.ones((1, F32_COLS), F32), tm=t.rows_f32, tn=F32_COLS,
                 n_norm_tiles=0, out_dtype=F32)
    q_scale = LOG2E / math.sqrt(GROUP_DIM)
    colgain = jnp.concatenate([jnp.tile(sb_q_gain * q_scale, SB_HEADS), jnp.tile(sb_k_gain, SB_HEADS),
                               jnp.ones((SB_WIDTH,), F32)]).reshape(1, 3 * SB_WIDTH)
    assert (2 * SB_WIDTH) % t.qkv_cols == 0
    qkv = _proj(h1, w.sb, layer, colgain, tm=t.rows, tn=t.qkv_cols, n_norm_tiles=2 * SB_WIDTH // t.qkv_cols,
                out_dtype=BF16)

    proj3 = proj.reshape(batch, seq, F32_COLS)
    y_pool = _pool(proj3, w_pool.astype(BF16), pool_scale, ts=t.pool_rows)
    y_sb = _stick_breaking(qkv.reshape(batch, seq, 3 * SB_WIDTH), tq=t.rows, tk=GROUP_DIM)
    w_a2p = jnp.pad(gla_w_a2, ((0, GROUP_DIM - GLA_RANK), (0, 0))).astype(BF16)
    y_gla = _gla(proj3, w_a2p, gla_b_a2.reshape(1, GLA_WIDTH), gla_out_gain.reshape(1, GROUP_DIM),
                 chunks_per_step=t.gla_chunks)

    m = batch * seq
    merged = _merge(h1, y_pool.reshape(m, POOL_WIDTH), y_sb.reshape(m, SB_WIDTH), y_gla.reshape(m, GLA_WIDTH),
                    w.gates, w.br_pool, w.br_sb, w.br_gla, layer, tm=t.rows, tn=t.merge_cols)
    x2, h2 = _resproj(merged, w.out, layer, x2, ga1, g_norm2.reshape(1, d), sc2, sh2, seq=seq, tm=t.rows_f32)
    return _swiglu(h2, x2, ga2, w.ff_gate, w.ff_up, w.ff_down, layer, next_norm, seq=seq, tm=t.rows,
                   tf=t.ff_cols)


def kernel(x, c, w_ada, b_ada, g_norm1, w_in, w_pool, pool_scale, sb_q_gain, sb_k_gain, gla_w_a2, gla_b_a2, gla_out_gain, w_br_pool, w_br_sb, w_br_gla, w_out, g_norm2, w_ff_gate, w_ff_up, w_ff_down):
    batch, seq, d = x.shape
    depth = w_ada.shape[0]
    c_pad = jnp.pad(c, ((0, 8 - batch % 8 if batch % 8 else 0), (0, 0)))
    mod = _ada(c_pad, w_ada, b_ada)[:, :batch]
    x2 = x.reshape(batch * seq, d)
    w = _Weights(*_split_w_in(w_in), *[a.astype(BF16) for a in (w_br_pool, w_br_sb, w_br_gla, w_out,
                                                               w_ff_gate, w_ff_up, w_ff_down)])

    def first_norm(l):
        sh1, sc1 = _mod_vectors(mod[l])[:2]
        return g_norm1[l].reshape(1, d), sc1, sh1

    h1 = _norm(x2, *first_norm(0), seq=seq, tm=_tiles(seq).rows_f32)
    for l in range(depth):
        next_norm = first_norm(l + 1) if l + 1 < depth else None
        x2, h1 = _layer(x2, h1, mod[l], next_norm, batch, seq, l, w, w_pool[l], pool_scale[l],
                        sb_q_gain[l], sb_k_gain[l], gla_w_a2[l], gla_b_a2[l], gla_out_gain[l], g_norm2[l])
    return x2.reshape(batch, seq, d)
```

```python
import functools
import math
from typing import NamedTuple

import numpy as np
import jax
import jax.numpy as jnp
from jax import lax
from jax.experimental import pallas as pl
from jax.experimental.pallas import tpu as pltpu

F32 = jnp.float32
BF16 = jnp.bfloat16

D_MODEL = 2048
POOL_WINDOWS = (2, 4, 8, 16)
POOL_GROUPS = 4
GROUP_DIM = 128
POOL_WIDTH = POOL_GROUPS * GROUP_DIM
SB_HEADS = 8
SB_WIDTH = SB_HEADS * GROUP_DIM
GLA_HEADS = 4
GLA_WIDTH = GLA_HEADS * GROUP_DIM
GLA_RANK = 16
GLA_TAU = 16.0
GLA_CHUNK = 128
GLA_LEVELS = 7
N_BRANCH = 3
D_FF = 5632
RMS_EPS = 1e-6
LOG2E = 1.4426950408889634
POOL_HALO = 16
SB_DEAD_BITS = 160.0
SB_MASKED_BITS = 1.0e4
SB_STATIC_LAGS = 3

VMEM_LIMIT = 56 * 1024 * 1024
MXU_WIDTH = 256
ROW_CHUNK = 256


def _cparams(sem):
    return pltpu.CompilerParams(dimension_semantics=sem, vmem_limit_bytes=VMEM_LIMIT)


def _nt_dot(a, b):
    return lax.dot_general(a, b, (((1,), (1,)), ((), ())), preferred_element_type=F32)


def _silu(x):
    return x * jax.nn.sigmoid(x)


def _ada_kernel(c_ref, w_ref, b_ref, o_ref):
    c = c_ref[...]
    a = _silu(c).astype(BF16)
    o_ref[0] = jnp.dot(a, w_ref[0].astype(BF16), preferred_element_type=F32) + b_ref[0]


def _ada(c_pad, w_ada, b_ada, tn=1024):
    depth, d, n = w_ada.shape
    rows = c_pad.shape[0]
    return pl.pallas_call(
        _ada_kernel,
        grid=(depth, n // tn),
        in_specs=[
            pl.BlockSpec((rows, d), lambda l, j: (0, 0)),
            pl.BlockSpec((1, d, tn), lambda l, j: (l, 0, j)),
            pl.BlockSpec((1, 1, tn), lambda l, j: (l, 0, j)),
        ],
        out_specs=pl.BlockSpec((1, rows, tn), lambda l, j: (l, 0, j)),
        out_shape=jax.ShapeDtypeStruct((depth, rows, n), F32),
        compiler_params=_cparams(("arbitrary", "arbitrary")),
        name="ada_modulation",
    )(c_pad, w_ada, b_ada.reshape(depth, 1, n))


W_IN_SB = POOL_WIDTH
W_IN_GLA = W_IN_SB + 3 * SB_WIDTH
W_IN_A = W_IN_GLA + 4 * GLA_WIDTH
W_IN_GATE = W_IN_A + GLA_RANK
W_IN_COLS = W_IN_GATE + N_BRANCH * D_MODEL
A_LOW_COL = POOL_WIDTH + 4 * GLA_WIDTH
F32_COLS = A_LOW_COL + MXU_WIDTH


def _split_w_in_kernel(w_ref, f_ref, sb_ref, g_ref):
    f_ref[0, :, :POOL_WIDTH] = w_ref[0, :, :W_IN_SB]
    f_ref[0, :, POOL_WIDTH:A_LOW_COL] = w_ref[0, :, W_IN_GLA:W_IN_A]
    f_ref[0, :, A_LOW_COL:] = jnp.zeros((f_ref.shape[1], MXU_WIDTH), BF16)
    f_ref[0, :, A_LOW_COL:A_LOW_COL + GLA_RANK] = w_ref[0, :, W_IN_A:W_IN_GATE]
    sb_ref[0] = w_ref[0, :, W_IN_SB:W_IN_GLA]
    g_ref[0] = w_ref[0, :, W_IN_GATE:]


def _split_w_in(w_in, tr=512):
    depth, d, n = w_in.shape
    assert n == W_IN_COLS and w_in.dtype == BF16
    widths = (F32_COLS, 3 * SB_WIDTH, N_BRANCH * D_MODEL)
    return pl.pallas_call(
        _split_w_in_kernel,
        grid=(depth, d // tr),
        in_specs=[pl.BlockSpec((1, tr, n), lambda l, i: (l, i, 0))],
        out_specs=[pl.BlockSpec((1, tr, w), lambda l, i: (l, i, 0)) for w in widths],
        out_shape=[jax.ShapeDtypeStruct((depth, d, w), BF16) for w in widths],
        compiler_params=_cparams(("arbitrary", "arbitrary")),
        name="split_w_in",
    )(w_in)


def _modulated_norm(x, gain, scale, shift):
    ms = jnp.mean(x * x, axis=-1, keepdims=True)
    y = x * lax.rsqrt(ms + RMS_EPS) * gain
    return y * (1.0 + scale) + shift


def _group_rmsnorm(acc, colgain):
    out = []
    for c in range(acc.shape[1] // GROUP_DIM):
        sl = slice(c * GROUP_DIM, (c + 1) * GROUP_DIM)
        blk = acc[:, sl]
        ms = jnp.mean(blk * blk, axis=-1, keepdims=True)
        out.append(blk * lax.rsqrt(ms + RMS_EPS) * colgain[:, sl])
    return jnp.concatenate(out, axis=1)


def _norm_kernel(x_ref, g_ref, sc_ref, sh_ref, h_ref):
    h_ref[...] = _modulated_norm(x_ref[...], g_ref[...], sc_ref[0], sh_ref[0]).astype(BF16)


def _norm(x2, gain, scale, shift, *, seq, tm):
    m, d = x2.shape
    tpb = seq // tm
    mod = pl.BlockSpec((1, 1, d), lambda i: (i // tpb, 0, 0))
    return pl.pallas_call(
        _norm_kernel,
        grid=(m // tm,),
        in_specs=[pl.BlockSpec((tm, d), lambda i: (i, 0)), pl.BlockSpec((1, d), lambda i: (0, 0)), mod, mod],
        out_specs=pl.BlockSpec((tm, d), lambda i: (i, 0)),
        out_shape=jax.ShapeDtypeStruct((m, d), BF16),
        compiler_params=_cparams(("arbitrary",)),
        name="modulated_norm",
    )(x2, gain, scale, shift)


def _proj_kernel(h_ref, w_ref, cg_ref, o_ref, *, n_norm_tiles, tn):
    j = pl.program_id(1)
    h = h_ref[...]

    def tile(normed):
        for c in range(tn // MXU_WIDTH):
            cols = slice(c * MXU_WIDTH, (c + 1) * MXU_WIDTH)
            acc = jnp.dot(h, w_ref[:, cols], preferred_element_type=F32)
            if normed:
                acc = _group_rmsnorm(acc, cg_ref[:, cols])
            o_ref[:, cols] = acc.astype(o_ref.dtype)

    if n_norm_tiles == 0:
        tile(False)
    else:
        pl.when(j < n_norm_tiles)(lambda: tile(True))
        pl.when(j >= n_norm_tiles)(lambda: tile(False))


def _proj(h, w, layer, colgain, *, tm, tn, n_norm_tiles, out_dtype):
    m, d = h.shape
    n = w.shape[2]
    kern = functools.partial(_proj_kernel, n_norm_tiles=n_norm_tiles, tn=tn)
    return pl.pallas_call(
        kern,
        grid=(m // tm, n // tn),
        in_specs=[
            pl.BlockSpec((tm, d), lambda i, j: (i, 0)),
            pl.BlockSpec((None, d, tn), lambda i, j: (layer, 0, j)),
            pl.BlockSpec((1, tn), lambda i, j: (0, j)),
        ],
        out_specs=pl.BlockSpec((tm, tn), lambda i, j: (i, j)),
        out_shape=jax.ShapeDtypeStruct((m, n), out_dtype),
        compiler_params=_cparams(("arbitrary", "arbitrary")),
        name="in_proj",
    )(h, w, colgain)


def _pool_kernel(u_ref, w_ref, ps_ref, o_ref, ext_ref, *, ts):
    i = pl.program_id(1)

    @pl.when(i == 0)
    def _():
        ext_ref[0:POOL_HALO, :] = jnp.zeros((POOL_HALO, POOL_WIDTH), F32)

    @pl.when(i > 0)
    def _():
        ext_ref[0:POOL_HALO, :] = ext_ref[ts:ts + POOL_HALO, :]

    ext_ref[POOL_HALO:, :] = u_ref[0]
    pos1 = (i * ts + 1 + lax.broadcasted_iota(jnp.int32, (ts, GROUP_DIM), 0)).astype(F32)
    for g, w in enumerate(POOL_WINDOWS):
        cols = slice(g * GROUP_DIM, (g + 1) * GROUP_DIM)
        u = ext_ref[POOL_HALO:, cols]
        win = u
        for k in range(1, w):
            win = win + ext_ref[POOL_HALO - k:POOL_HALO - k + ts, cols]
        pooled = win / jnp.minimum(pos1, float(w)) - u
        y = jnp.dot(pooled.astype(BF16), w_ref[g], preferred_element_type=F32) * ps_ref[g]
        o_ref[0, :, cols] = y.astype(o_ref.dtype)


def _pool(proj3, w_pool, pool_scale, ts):
    b, s, _ = proj3.shape
    kern = functools.partial(_pool_kernel, ts=ts)
    return pl.pallas_call(
        kern,
        grid=(b, s // ts),
        in_specs=[
            pl.BlockSpec((1, ts, POOL_WIDTH), lambda bi, i: (bi, i, 0)),
            pl.BlockSpec((POOL_GROUPS, GROUP_DIM, GROUP_DIM), lambda bi, i: (0, 0, 0)),
            pl.BlockSpec((POOL_GROUPS, 1, GROUP_DIM), lambda bi, i: (0, 0, 0)),
        ],
        out_specs=pl.BlockSpec((1, ts, POOL_WIDTH), lambda bi, i: (bi, i, 0)),
        out_shape=jax.ShapeDtypeStruct((b, s, POOL_WIDTH), BF16),
        scratch_shapes=[pltpu.VMEM((ts + POOL_HALO, POOL_WIDTH), F32)],
        compiler_params=_cparams(("arbitrary", "arbitrary")),
        name="pool_mixer",
    )(proj3, w_pool, pool_scale.reshape(POOL_GROUPS, 1, GROUP_DIM))


def _sb_cumsum_matrix(tk):
    j = np.arange(tk)[:, None]
    s = np.arange(tk)[None, :]
    one = np.concatenate([(j >= s).astype(np.float32), np.ones((tk, tk), np.float32)], axis=1)
    return jnp.asarray(np.concatenate([one, one], axis=0), dtype=BF16)


def _sb_kernel(q_ref, k_ref, v_ref, w2_ref, o_ref, acc_ref, run_ref, *, tq, tk):
    i = pl.program_id(2)
    w2 = w2_ref[...]
    n_sub = tq // tk
    heads = [slice(hh * GROUP_DIM, (hh + 1) * GROUP_DIM) for hh in range(q_ref.shape[2] // GROUP_DIM)]
    acc_ref[...] = jnp.zeros(acc_ref.shape, F32)
    run_ref[...] = jnp.zeros(run_ref.shape, F32)
    row = lax.broadcasted_iota(jnp.int32, (tq, tk), 0)
    col = lax.broadcasted_iota(jnp.int32, (tq, tk), 1)
    diag_mask = col < (row & (tk - 1))
    sub = lambda r: slice(r * tk, (r + 1) * tk)

    def key_blocks(lag):
        blocks = []
        for r in range(n_sub):
            jb = i * n_sub + r - lag
            start = pl.multiple_of(jnp.maximum(jb, 0) * tk, tk)
            blocks.append((start, jnp.where(jb >= 0, 0.0, SB_MASKED_BITS)))
        return blocks

    def scores(cols, blocks, diagonal):
        z = jnp.concatenate([_nt_dot(q_ref[0, sub(r), cols], k_ref[0, pl.ds(start, tk), cols])
                             for r, (start, _) in enumerate(blocks)], axis=0)
        neg_abs = lax.bitcast_convert_type(
            lax.bitcast_convert_type(z, jnp.uint32) | jnp.uint32(0x80000000), F32)
        neg_log_keep = jnp.maximum(z, 0.0) + jnp.log(1.0 + jnp.exp2(neg_abs)) * LOG2E
        if diagonal:
            neg_log_keep = jnp.where(diag_mask, neg_log_keep, 0.0)
            z = jnp.where(diag_mask, z, -SB_MASKED_BITS)
        hi = lax.bitcast_convert_type(
            lax.bitcast_convert_type(neg_log_keep, jnp.uint32) & jnp.uint32(0xFFFF0000), F32)
        lo = neg_log_keep - hi
        c = jnp.dot(jnp.concatenate([hi.astype(BF16), lo.astype(BF16)], axis=1), w2, preferred_element_type=F32)
        return z, c

    def accumulate(hh, cols, blocks, z, c, diagonal):
        run = run_ref[hh]
        if not diagonal:
            run = run + jnp.concatenate([jnp.full((tk, tk), pen, F32) for _, pen in blocks], axis=0)
        a = jnp.exp2(z - c[:, :tk] - run).astype(BF16)
        for r, (start, _) in enumerate(blocks):
            acc_ref[hh, sub(r), :] += jnp.dot(a[sub(r), :], v_ref[0, pl.ds(start, tk), cols],
                                              preferred_element_type=F32)
        run_ref[hh] = run + c[:, tk:]

    lags = [key_blocks(lag) for lag in range(SB_STATIC_LAGS)]
    staged = [[scores(cols, blocks, lag == 0) for lag, blocks in enumerate(lags)] for cols in heads]
    for lag, blocks in enumerate(lags):
        for hh, cols in enumerate(heads):
            accumulate(hh, cols, blocks, *staged[hh][lag], lag == 0)

    for hh, cols in enumerate(heads):
        def cond(carry):
            lag, min_run = carry
            return jnp.logical_and(lag <= i * n_sub + n_sub - 1, min_run < SB_DEAD_BITS)

        def body(carry, hh=hh, cols=cols):
            lag, _ = carry
            blocks = key_blocks(lag)
            z, c = scores(cols, blocks, False)
            accumulate(hh, cols, blocks, z, c, False)
            return lag + 1, jnp.min(run_ref[hh])

        lax.while_loop(cond, body, (jnp.int32(SB_STATIC_LAGS), jnp.min(run_ref[hh])))
        o_ref[0, :, cols] = acc_ref[hh].astype(o_ref.dtype)


def _stick_breaking(qkv, tq, tk, heads_per_step):
    b, s, _ = qkv.shape
    hw = heads_per_step * GROUP_DIM
    groups = SB_HEADS // heads_per_step
    kern = functools.partial(_sb_kernel, tq=tq, tk=tk)
    return pl.pallas_call(
        kern,
        grid=(b, groups, s // tq),
        in_specs=[
            pl.BlockSpec((1, tq, hw), lambda bi, h, i: (bi, i, h)),
            pl.BlockSpec((1, s, hw), lambda bi, h, i: (bi, 0, groups + h)),
            pl.BlockSpec((1, s, hw), lambda bi, h, i: (bi, 0, 2 * groups + h)),
            pl.BlockSpec((2 * tk, 2 * tk), lambda bi, h, i: (0, 0)),
        ],
        out_specs=pl.BlockSpec((1, tq, hw), lambda bi, h, i: (bi, i, h)),
        out_shape=jax.ShapeDtypeStruct((b, s, SB_WIDTH), BF16),
        scratch_shapes=[pltpu.VMEM((heads_per_step, tq, GROUP_DIM), F32),
                        pltpu.VMEM((heads_per_step, tq, tk), F32)],
        compiler_params=_cparams(("arbitrary", "arbitrary", "arbitrary")),
        name="stick_breaking",
    )(qkv, qkv, qkv, _sb_cumsum_matrix(tk))


def _gla_constants():
    c = GLA_CHUNK
    t = np.arange(c)[:, None]
    j = np.arange(c)[None, :]
    dst = np.concatenate([j <= t, np.ones((8, c), bool)], axis=0).astype(np.float32)
    s = j
    masks = []
    for l in range(GLA_LEVELS):
        masks.append(((t ^ s) >> l == 1) & (t > s))
    masks.append(t == s)
    return jnp.asarray(dst, dtype=BF16), jnp.asarray(np.stack(masks).astype(np.float32))


def _split3(x):
    x1 = x.astype(BF16)
    r = x - x1.astype(F32)
    x2 = r.astype(BF16)
    x3 = (r - x2.astype(F32)).astype(BF16)
    return x1, x2, x3


def _gla_kernel(q_ref, k_ref, v_ref, r_ref, a_ref, wa_ref, ba_ref, og_ref, dst_ref, msk_ref,
                o_ref, st_ref):
    @pl.when(pl.program_id(1) == 0)
    def _():
        st_ref[...] = jnp.zeros(st_ref.shape, F32)

    for cc in range(q_ref.shape[1] // GLA_CHUNK):
        _gla_chunk(slice(cc * GLA_CHUNK, (cc + 1) * GLA_CHUNK), q_ref, k_ref, v_ref, r_ref, a_ref,
                   wa_ref, ba_ref, og_ref, dst_ref, msk_ref, o_ref, st_ref)


def _gla_chunk(rows, q_ref, k_ref, v_ref, r_ref, a_ref, wa_ref, ba_ref, og_ref, dst_ref, msk_ref,
               o_ref, st_ref):
    c = GLA_CHUNK
    x = jnp.dot(a_ref[0, rows, :].astype(BF16), wa_ref[...], preferred_element_type=F32) + ba_ref[...]
    log_sig = jnp.minimum(x, 0.0) - jnp.log(1.0 + jnp.exp(-jnp.abs(x)))
    g = log_sig * (LOG2E / GLA_TAU)
    dst = dst_ref[...]
    g1, g2, g3 = _split3(g)
    sums = (jnp.dot(dst, g1, preferred_element_type=F32)
            + jnp.dot(dst, g2, preferred_element_type=F32)
            + jnp.dot(dst, g3, preferred_element_type=F32))
    prefix = sums[:c]
    total = sums[c:c + 1]
    e_in = prefix
    e_out = total - prefix

    row = lax.broadcasted_iota(jnp.int32, g.shape, 0)
    pos = row & 3
    g_prev = pltpu.roll(g, 1, 0)
    g_next = pltpu.roll(g, c - 1, 0)
    e_lvl = [jnp.where((row & 1) == 1, g, 0.0),
             jnp.where(pos == 0, g_next, jnp.where(pos == 1, 0.0, jnp.where(pos == 2, g, g_prev + g)))]
    for l in range(2, GLA_LEVELS):
        m = 1 << l
        blocks = prefix.reshape(c // (2 * m), 2 * m, GLA_WIDTH)
        e_lvl.append((-jnp.abs(blocks - blocks[:, m - 1:m, :])).reshape(c, GLA_WIDTH))

    for h in range(GLA_HEADS):
        cols = slice(h * GROUP_DIM, (h + 1) * GROUP_DIM)
        q = q_ref[0, rows, cols] * (GROUP_DIM ** -0.5)
        k = k_ref[0, rows, cols]
        v = v_ref[0, rows, cols].astype(BF16)
        scores = msk_ref[GLA_LEVELS] * _nt_dot(q.astype(BF16), k.astype(BF16))
        for l in range(GLA_LEVELS):
            xl = jnp.exp2(e_lvl[l][:, cols])
            scores = scores + msk_ref[l] * _nt_dot((q * xl).astype(BF16), (k * xl).astype(BF16))
        o = jnp.dot(scores.astype(BF16), v, preferred_element_type=F32)
        x_in = jnp.exp2(e_in[:, cols])
        st = st_ref[h]
        o = o + _nt_dot((q * x_in).astype(BF16), st.astype(BF16))
        x_out = jnp.exp2(e_out[:, cols])
        x_all = jnp.exp2(total[:, cols])
        vt = v_ref[0, rows, cols].T.astype(BF16)
        st_ref[h] = st * x_all + jnp.dot(vt, (k * x_out).astype(BF16), preferred_element_type=F32)
        ms = jnp.mean(o * o, axis=-1, keepdims=True)
        o = o * lax.rsqrt(ms + RMS_EPS) * og_ref[...]
        o_ref[0, rows, cols] = (o * _silu(r_ref[0, rows, cols])).astype(o_ref.dtype)


def _gla(proj3, w_a2p, b_a2, out_gain, chunks_per_step):
    b, s, _ = proj3.shape
    c = GLA_CHUNK * chunks_per_step
    dst, masks = _gla_constants()
    wide = lambda blk: pl.BlockSpec((1, c, GLA_WIDTH), lambda bi, ci: (bi, ci, blk))
    const2 = lambda shape: pl.BlockSpec(shape, lambda bi, ci: (0, 0))
    a_blk = (POOL_WIDTH + 4 * GLA_WIDTH) // GROUP_DIM
    return pl.pallas_call(
        _gla_kernel,
        grid=(b, s // c),
        in_specs=[
            wide(1), wide(2), wide(3), wide(4),
            pl.BlockSpec((1, c, GROUP_DIM), lambda bi, ci: (bi, ci, a_blk)),
            const2((GROUP_DIM, GLA_WIDTH)),
            const2((1, GLA_WIDTH)),
            const2((1, GROUP_DIM)),
            const2(dst.shape),
            pl.BlockSpec(masks.shape, lambda bi, ci: (0, 0, 0)),
        ],
        out_specs=pl.BlockSpec((1, c, GLA_WIDTH), lambda bi, ci: (bi, ci, 0)),
        out_shape=jax.ShapeDtypeStruct((b, s, GLA_WIDTH), BF16),
        scratch_shapes=[pltpu.VMEM((GLA_HEADS, GROUP_DIM, GROUP_DIM), F32)],
        compiler_params=_cparams(("arbitrary", "arbitrary")),
        name="gla",
    )(proj3, proj3, proj3, proj3, proj3, w_a2p, b_a2, out_gain, dst, masks)


def _merge_kernel(h_ref, yp_ref, ys_ref, yg_ref, wgp_ref, wgs_ref, wgg_ref, wp_ref, ws_ref, wg_ref, o_ref):
    h = h_ref[...]

    def branch(wgate_ref, y_ref, w_ref):
        gate = jax.nn.sigmoid(jnp.dot(h, wgate_ref[...], preferred_element_type=F32))
        return gate * jnp.dot(y_ref[...], w_ref[...], preferred_element_type=F32)

    merged = branch(wgp_ref, yp_ref, wp_ref) + branch(wgs_ref, ys_ref, ws_ref) + branch(wgg_ref, yg_ref, wg_ref)
    o_ref[...] = merged.astype(o_ref.dtype)


def _merge(h, y_pool, y_sb, y_gla, w_gates, w_br_pool, w_br_sb, w_br_gla, layer, *, tm, tn):
    m, d = h.shape
    nj = d // tn
    row = lambda width: pl.BlockSpec((tm, width), lambda i, j: (i, 0))
    gate_w = lambda br: pl.BlockSpec((None, d, tn), lambda i, j: (layer, 0, br * nj + j))
    br_w = lambda width: pl.BlockSpec((None, width, tn), lambda i, j: (layer, 0, j))
    return pl.pallas_call(
        _merge_kernel,
        grid=(m // tm, nj),
        in_specs=[
            row(d), row(POOL_WIDTH), row(SB_WIDTH), row(GLA_WIDTH),
            gate_w(0), gate_w(1), gate_w(2),
            br_w(POOL_WIDTH), br_w(SB_WIDTH), br_w(GLA_WIDTH),
        ],
        out_specs=pl.BlockSpec((tm, tn), lambda i, j: (i, j)),
        out_shape=jax.ShapeDtypeStruct((m, d), BF16),
        compiler_params=_cparams(("arbitrary", "arbitrary")),
        name="gated_merge",
    )(h, y_pool, y_sb, y_gla, w_gates, w_gates, w_gates, w_br_pool, w_br_sb, w_br_gla)


def _resproj_kernel(a_ref, w_ref, x_ref, ga_ref, g_ref, sc_ref, sh_ref, o_ref, h_ref):
    for r in range(a_ref.shape[0] // ROW_CHUNK):
        rows = slice(r * ROW_CHUNK, (r + 1) * ROW_CHUNK)
        y = jnp.dot(a_ref[rows, :], w_ref[...], preferred_element_type=F32)
        x_new = x_ref[rows, :] + ga_ref[0] * y
        o_ref[rows, :] = x_new
        h_ref[rows, :] = _modulated_norm(x_new, g_ref[...], sc_ref[0], sh_ref[0]).astype(BF16)


def _resproj(a, w, layer, x2, gate, gain, scale, shift, *, seq, tm):
    m, kdim = a.shape
    d = w.shape[2]
    tpb = seq // tm
    row = lambda width: pl.BlockSpec((tm, width), lambda i: (i, 0))
    mod = pl.BlockSpec((1, 1, d), lambda i: (i // tpb, 0, 0))
    return pl.pallas_call(
        _resproj_kernel,
        grid=(m // tm,),
        in_specs=[row(kdim), pl.BlockSpec((None, kdim, d), lambda i: (layer, 0, 0)), row(d), mod,
                  pl.BlockSpec((1, d), lambda i: (0, 0)), mod, mod],
        out_specs=[row(d), row(d)],
        out_shape=[jax.ShapeDtypeStruct((m, d), F32), jax.ShapeDtypeStruct((m, d), BF16)],
        compiler_params=_cparams(("arbitrary",)),
        name="residual_proj",
    )(a, w, x2, gate, gain, scale, shift)


def _swiglu_kernel(*refs, emit_next, x_chunks):
    if emit_next:
        h_ref, x_ref, ga_ref, wg_ref, wu_ref, wd_ref, g_ref, sc_ref, sh_ref, o_ref, hn_ref = refs
    else:
        h_ref, x_ref, ga_ref, wg_ref, wu_ref, wd_ref, o_ref = refs
    f = pl.program_id(1)
    last = pl.num_programs(1) - 1
    xw = x_ref.shape[1]

    def gated_partial(rows):
        h = h_ref[rows, :]
        gate = jnp.dot(h, wg_ref[...], preferred_element_type=F32)
        up = jnp.dot(h, wu_ref[...], preferred_element_type=F32)
        act = (_silu(gate) * up).astype(BF16)
        return ga_ref[0] * jnp.dot(act, wd_ref[...], preferred_element_type=F32)

    row_chunks = [slice(r * ROW_CHUNK, (r + 1) * ROW_CHUNK) for r in range(h_ref.shape[0] // ROW_CHUNK)]

    @pl.when(f == 0)
    def _():
        for rows in row_chunks:
            o_ref[rows, :] = gated_partial(rows)

    @pl.when(jnp.logical_and(f > 0, f < last))
    def _():
        for rows in row_chunks:
            o_ref[rows, :] += gated_partial(rows)

    for c in range(x_chunks):
        @pl.when(f == c)
        def _():
            o_ref[:, c * xw:(c + 1) * xw] += x_ref[...]

    @pl.when(f == last)
    def _():
        for rows in row_chunks:
            x_new = o_ref[rows, :] + gated_partial(rows)
            o_ref[rows, :] = x_new
            if emit_next:
                hn_ref[rows, :] = _modulated_norm(x_new, g_ref[...], sc_ref[0], sh_ref[0]).astype(BF16)


def _swiglu(h, x2, gate, w_gate, w_up, w_down, layer, next_norm, *, seq, tm, tf):
    m, d = x2.shape
    dff = w_gate.shape[2]
    tpb = seq // tm
    n_steps = dff // tf
    x_chunks = 8
    assert x_chunks < n_steps and d % (x_chunks * GROUP_DIM) == 0
    emit_next = next_norm is not None
    row = pl.BlockSpec((tm, d), lambda i, f: (i, 0))
    mod = pl.BlockSpec((1, 1, d), lambda i, f: (i // tpb, 0, 0))
    in_specs = [row, pl.BlockSpec((tm, d // x_chunks), lambda i, f: (i, jnp.minimum(f, x_chunks - 1))), mod,
                pl.BlockSpec((None, d, tf), lambda i, f: (layer, 0, f)),
                pl.BlockSpec((None, d, tf), lambda i, f: (layer, 0, f)),
                pl.BlockSpec((None, tf, d), lambda i, f: (layer, f, 0))]
    args = [h, x2, gate, w_gate, w_up, w_down]
    out_specs = [row]
    out_shape = [jax.ShapeDtypeStruct((m, d), F32)]
    if emit_next:
        in_specs += [pl.BlockSpec((1, d), lambda i, f: (0, 0)), mod, mod]
        args += list(next_norm)
        out_specs.append(row)
        out_shape.append(jax.ShapeDtypeStruct((m, d), BF16))
    outs = pl.pallas_call(
        functools.partial(_swiglu_kernel, emit_next=emit_next, x_chunks=x_chunks),
        grid=(m // tm, n_steps),
        in_specs=in_specs,
        out_specs=out_specs,
        out_shape=out_shape,
        compiler_params=_cparams(("arbitrary", "arbitrary")),
        name="swiglu",
    )(*args)
    return (outs[0], outs[1]) if emit_next else (outs[0], None)


class _Tiles(NamedTuple):
    rows: int
    rows_f32: int
    pool_rows: int
    qkv_cols: int
    merge_cols: int
    ff_cols: int
    gla_chunks: int
    sb_heads: int


def _tile(n, pref):
    t = min(n, pref)
    assert n % t == 0, (n, t)
    return t


def _tiles(seq):
    return _Tiles(rows=_tile(seq, 1024), rows_f32=_tile(seq, 512), pool_rows=_tile(seq, 2048),
                  qkv_cols=1024, merge_cols=512, ff_cols=512,
                  gla_chunks=_tile(seq // GLA_CHUNK, 4), sb_heads=2)


def _mod_vectors(mod_l):
    d = D_MODEL
    return [mod_l[:, None, k * d:(k + 1) * d] for k in range(6)]


class _Weights(NamedTuple):
    f32cols: jax.Array
    sb: jax.Array
    gates: jax.Array
    br_pool: jax.Array
    br_sb: jax.Array
    br_gla: jax.Array
    out: jax.Array
    ff_gate: jax.Array
    ff_up: jax.Array
    ff_down: jax.Array


def _layer(x2, h1, mod_l, next_norm, batch, seq, layer, w, w_pool, pool_scale, sb_q_gain, sb_k_gain,
           gla_w_a2, gla_b_a2, gla_out_gain, g_norm2):
    d = D_MODEL
    t = _tiles(seq)
    _, _, ga1, sh2, sc2, ga2 = _mod_vectors(mod_l)

    proj = _proj(h1, w.f32cols, layer, jnp.ones((1, F32_COLS), F32), tm=t.rows_f32, tn=F32_COLS,
                 n_norm_tiles=0, out_dtype=F32)
    q_scale = LOG2E / math.sqrt(GROUP_DIM)
    colgain = jnp.concatenate([jnp.tile(sb_q_gain * q_scale, SB_HEADS), jnp.tile(sb_k_gain, SB_HEADS),
                               jnp.ones((SB_WIDTH,), F32)]).reshape(1, 3 * SB_WIDTH)
    assert (2 * SB_WIDTH) % t.qkv_cols == 0
    qkv = _proj(h1, w.sb, layer, colgain, tm=t.rows, tn=t.qkv_cols, n_norm_tiles=2 * SB_WIDTH // t.qkv_cols,
                out_dtype=BF16)

    proj3 = proj.reshape(batch, seq, F32_COLS)
    y_pool = _pool(proj3, w_pool.astype(BF16), pool_scale, ts=t.pool_rows)
    y_sb = _stick_breaking(qkv.reshape(batch, seq, 3 * SB_WIDTH), tq=t.rows, tk=GROUP_DIM,
                           heads_per_step=t.sb_heads)
    w_a2p = jnp.pad(gla_w_a2, ((0, GROUP_DIM - GLA_RANK), (0, 0))).astype(BF16)
    y_gla = _gla(proj3, w_a2p, gla_b_a2.reshape(1, GLA_WIDTH), gla_out_gain.reshape(1, GROUP_DIM),
                 chunks_per_step=t.gla_chunks)

    m = batch * seq
    merged = _merge(h1, y_pool.reshape(m, POOL_WIDTH), y_sb.reshape(m, SB_WIDTH), y_gla.reshape(m, GLA_WIDTH),
                    w.gates, w.br_pool, w.br_sb, w.br_gla, layer, tm=t.rows, tn=t.merge_cols)
    x2, h2 = _resproj(merged, w.out, layer, x2, ga1, g_norm2.reshape(1, d), sc2, sh2, seq=seq, tm=t.rows_f32)
    return _swiglu(h2, x2, ga2, w.ff_gate, w.ff_up, w.ff_down, layer, next_norm, seq=seq, tm=t.rows,
                   tf=t.ff_cols)


def kernel(x, c, w_ada, b_ada, g_norm1, w_in, w_pool, pool_scale, sb_q_gain, sb_k_gain, gla_w_a2, gla_b_a2, gla_out_gain, w_br_pool, w_br_sb, w_br_gla, w_out, g_norm2, w_ff_gate, w_ff_up, w_ff_down):
    batch, seq, d = x.shape
    depth = w_ada.shape[0]
    c_pad = jnp.pad(c, ((0, 8 - batch % 8 if batch % 8 else 0), (0, 0)))
    mod = _ada(c_pad, w_ada, b_ada)[:, :batch]
    x2 = x.reshape(batch * seq, d)
    w = _Weights(*_split_w_in(w_in.astype(BF16)), *[a.astype(BF16) for a in (w_br_pool, w_br_sb, w_br_gla, w_out,
                                                               w_ff_gate, w_ff_up, w_ff_down)])

    def first_norm(l):
        sh1, sc1 = _mod_vectors(mod[l])[:2]
        return g_norm1[l].reshape(1, d), sc1, sh1

    h1 = _norm(x2, *first_norm(0), seq=seq, tm=_tiles(seq).rows_f32)
    for l in range(depth):
        next_norm = first_norm(l + 1) if l + 1 < depth else None
        x2, h1 = _layer(x2, h1, mod[l], next_norm, batch, seq, l, w, w_pool[l], pool_scale[l],
                        sb_q_gain[l], sb_k_gain[l], gla_w_a2[l], gla_b_a2[l], gla_out_gain[l], g_norm2[l])
    return x2.reshape(batch, seq, d)
```

```python
import functools
import math
from typing import NamedTuple

import numpy as np
import jax
import jax.numpy as jnp
from jax import lax
from jax.experimental import pallas as pl
from jax.experimental.pallas import tpu as pltpu

F32 = jnp.float32
BF16 = jnp.bfloat16

D_MODEL = 2048
POOL_WINDOWS = (2, 4, 8, 16)
POOL_GROUPS = 4
GROUP_DIM = 128
POOL_WIDTH = POOL_GROUPS * GROUP_DIM
SB_HEADS = 8
SB_WIDTH = SB_HEADS * GROUP_DIM
GLA_HEADS = 4
GLA_WIDTH = GLA_HEADS * GROUP_DIM
GLA_RANK = 16
GLA_TAU = 16.0
GLA_CHUNK = 128
GLA_LEVELS = 7
N_BRANCH = 3
D_FF = 5632
RMS_EPS = 1e-6
LOG2E = 1.4426950408889634
POOL_HALO = 16
SB_DEAD_BITS = 160.0
SB_MASKED_BITS = 1.0e4
SB_STATIC_LAGS = 3

VMEM_LIMIT = 56 * 1024 * 1024
MXU_WIDTH = 256
ROW_CHUNK = 256


def _cparams(sem):
    return pltpu.CompilerParams(dimension_semantics=sem, vmem_limit_bytes=VMEM_LIMIT)


def _nt_dot(a, b):
    return lax.dot_general(a, b, (((1,), (1,)), ((), ())), preferred_element_type=F32)


def _silu(x):
    return x * jax.nn.sigmoid(x)


def _ada_kernel(c_ref, w_ref, b_ref, o_ref):
    c = c_ref[...]
    a = _silu(c).astype(BF16)
    o_ref[0] = jnp.dot(a, w_ref[0].astype(BF16), preferred_element_type=F32) + b_ref[0]


def _ada(c_pad, w_ada, b_ada, tn=1024):
    depth, d, n = w_ada.shape
    rows = c_pad.shape[0]
    return pl.pallas_call(
        _ada_kernel,
        grid=(depth, n // tn),
        in_specs=[
            pl.BlockSpec((rows, d), lambda l, j: (0, 0)),
            pl.BlockSpec((1, d, tn), lambda l, j: (l, 0, j)),
            pl.BlockSpec((1, 1, tn), lambda l, j: (l, 0, j)),
        ],
        out_specs=pl.BlockSpec((1, rows, tn), lambda l, j: (l, 0, j)),
        out_shape=jax.ShapeDtypeStruct((depth, rows, n), F32),
        compiler_params=_cparams(("arbitrary", "arbitrary")),
        name="ada_modulation",
    )(c_pad, w_ada, b_ada.reshape(depth, 1, n))


W_IN_SB = POOL_WIDTH
W_IN_GLA = W_IN_SB + 3 * SB_WIDTH
W_IN_A = W_IN_GLA + 4 * GLA_WIDTH
W_IN_GATE = W_IN_A + GLA_RANK
W_IN_COLS = W_IN_GATE + N_BRANCH * D_MODEL
A_LOW_COL = POOL_WIDTH + 4 * GLA_WIDTH
F32_COLS = A_LOW_COL + MXU_WIDTH


def _split_w_in_kernel(w_ref, f_ref, sb_ref, g_ref):
    f_ref[0, :, :POOL_WIDTH] = w_ref[0, :, :W_IN_SB]
    f_ref[0, :, POOL_WIDTH:A_LOW_COL] = w_ref[0, :, W_IN_GLA:W_IN_A]
    f_ref[0, :, A_LOW_COL:] = jnp.zeros((f_ref.shape[1], MXU_WIDTH), BF16)
    f_ref[0, :, A_LOW_COL:A_LOW_COL + GLA_RANK] = w_ref[0, :, W_IN_A:W_IN_GATE]
    sb_ref[0] = w_ref[0, :, W_IN_SB:W_IN_GLA]
    g_ref[0] = w_ref[0, :, W_IN_GATE:W_IN_COLS]


def _split_w_in(w_in, tr=512):
    depth, d, n = w_in.shape
    assert n == W_IN_COLS + (-W_IN_COLS) % GROUP_DIM and w_in.dtype == BF16
    widths = (F32_COLS, 3 * SB_WIDTH, N_BRANCH * D_MODEL)
    return pl.pallas_call(
        _split_w_in_kernel,
        grid=(depth, d // tr),
        in_specs=[pl.BlockSpec((1, tr, n), lambda l, i: (l, i, 0))],
        out_specs=[pl.BlockSpec((1, tr, w), lambda l, i: (l, i, 0)) for w in widths],
        out_shape=[jax.ShapeDtypeStruct((depth, d, w), BF16) for w in widths],
        compiler_params=_cparams(("arbitrary", "arbitrary")),
        name="split_w_in",
    )(w_in)


def _modulated_norm(x, gain, scale, shift):
    ms = jnp.mean(x * x, axis=-1, keepdims=True)
    y = x * lax.rsqrt(ms + RMS_EPS) * gain
    return y * (1.0 + scale) + shift


def _group_rmsnorm(acc, colgain):
    out = []
    for c in range(acc.shape[1] // GROUP_DIM):
        sl = slice(c * GROUP_DIM, (c + 1) * GROUP_DIM)
        blk = acc[:, sl]
        ms = jnp.mean(blk * blk, axis=-1, keepdims=True)
        out.append(blk * lax.rsqrt(ms + RMS_EPS) * colgain[:, sl])
    return jnp.concatenate(out, axis=1)


def _norm_kernel(x_ref, g_ref, sc_ref, sh_ref, h_ref):
    h_ref[...] = _modulated_norm(x_ref[...], g_ref[...], sc_ref[0], sh_ref[0]).astype(BF16)


def _norm(x2, gain, scale, shift, *, seq, tm):
    m, d = x2.shape
    tpb = seq // tm
    mod = pl.BlockSpec((1, 1, d), lambda i: (i // tpb, 0, 0))
    return pl.pallas_call(
        _norm_kernel,
        grid=(m // tm,),
        in_specs=[pl.BlockSpec((tm, d), lambda i: (i, 0)), pl.BlockSpec((1, d), lambda i: (0, 0)), mod, mod],
        out_specs=pl.BlockSpec((tm, d), lambda i: (i, 0)),
        out_shape=jax.ShapeDtypeStruct((m, d), BF16),
        compiler_params=_cparams(("arbitrary",)),
        name="modulated_norm",
    )(x2, gain, scale, shift)


def _proj_kernel(h_ref, w_ref, cg_ref, o_ref, *, n_norm_tiles, tn):
    j = pl.program_id(1)
    h = h_ref[...]

    def tile(normed):
        for c in range(tn // MXU_WIDTH):
            cols = slice(c * MXU_WIDTH, (c + 1) * MXU_WIDTH)
            acc = jnp.dot(h, w_ref[:, cols], preferred_element_type=F32)
            if normed:
                acc = _group_rmsnorm(acc, cg_ref[:, cols])
            o_ref[:, cols] = acc.astype(o_ref.dtype)

    if n_norm_tiles == 0:
        tile(False)
    else:
        pl.when(j < n_norm_tiles)(lambda: tile(True))
        pl.when(j >= n_norm_tiles)(lambda: tile(False))


def _proj(h, w, layer, colgain, *, tm, tn, n_norm_tiles, out_dtype):
    m, d = h.shape
    n = w.shape[2]
    kern = functools.partial(_proj_kernel, n_norm_tiles=n_norm_tiles, tn=tn)
    return pl.pallas_call(
        kern,
        grid=(m // tm, n // tn),
        in_specs=[
            pl.BlockSpec((tm, d), lambda i, j: (i, 0)),
            pl.BlockSpec((None, d, tn), lambda i, j: (layer, 0, j)),
            pl.BlockSpec((1, tn), lambda i, j: (0, j)),
        ],
        out_specs=pl.BlockSpec((tm, tn), lambda i, j: (i, j)),
        out_shape=jax.ShapeDtypeStruct((m, n), out_dtype),
        compiler_params=_cparams(("arbitrary", "arbitrary")),
        name="in_proj",
    )(h, w, colgain)


def _pool_kernel(u_ref, w_ref, ps_ref, o_ref, ext_ref, *, ts):
    i = pl.program_id(1)

    @pl.when(i == 0)
    def _():
        ext_ref[0:POOL_HALO, :] = jnp.zeros((POOL_HALO, POOL_WIDTH), F32)

    @pl.when(i > 0)
    def _():
        ext_ref[0:POOL_HALO, :] = ext_ref[ts:ts + POOL_HALO, :]

    ext_ref[POOL_HALO:, :] = u_ref[0]
    pos1 = (i * ts + 1 + lax.broadcasted_iota(jnp.int32, (ts, GROUP_DIM), 0)).astype(F32)
    for g, w in enumerate(POOL_WINDOWS):
        cols = slice(g * GROUP_DIM, (g + 1) * GROUP_DIM)
        u = ext_ref[POOL_HALO:, cols]
        win = u
        for k in range(1, w):
            win = win + ext_ref[POOL_HALO - k:POOL_HALO - k + ts, cols]
        pooled = win / jnp.minimum(pos1, float(w)) - u
        y = jnp.dot(pooled.astype(BF16), w_ref[g], preferred_element_type=F32) * ps_ref[g]
        o_ref[0, :, cols] = y.astype(o_ref.dtype)


def _pool(proj3, w_pool, pool_scale, ts):
    b, s, _ = proj3.shape
    kern = functools.partial(_pool_kernel, ts=ts)
    return pl.pallas_call(
        kern,
        grid=(b, s // ts),
        in_specs=[
            pl.BlockSpec((1, ts, POOL_WIDTH), lambda bi, i: (bi, i, 0)),
            pl.BlockSpec((POOL_GROUPS, GROUP_DIM, GROUP_DIM), lambda bi, i: (0, 0, 0)),
            pl.BlockSpec((POOL_GROUPS, 1, GROUP_DIM), lambda bi, i: (0, 0, 0)),
        ],
        out_specs=pl.BlockSpec((1, ts, POOL_WIDTH), lambda bi, i: (bi, i, 0)),
        out_shape=jax.ShapeDtypeStruct((b, s, POOL_WIDTH), BF16),
        scratch_shapes=[pltpu.VMEM((ts + POOL_HALO, POOL_WIDTH), F32)],
        compiler_params=_cparams(("arbitrary", "arbitrary")),
        name="pool_mixer",
    )(proj3, w_pool, pool_scale.reshape(POOL_GROUPS, 1, GROUP_DIM))


def _sb_cumsum_matrix(tk):
    j = np.arange(tk)[:, None]
    s = np.arange(tk)[None, :]
    one = np.concatenate([(j >= s).astype(np.float32), np.ones((tk, tk), np.float32)], axis=1)
    return jnp.asarray(np.concatenate([one, one], axis=0), dtype=BF16)


def _sb_kernel(q_ref, k_ref, v_ref, w2_ref, o_ref, acc_ref, run_ref, *, tq, tk):
    i = pl.program_id(2)
    w2 = w2_ref[...]
    n_sub = tq // tk
    heads = [slice(hh * GROUP_DIM, (hh + 1) * GROUP_DIM) for hh in range(q_ref.shape[2] // GROUP_DIM)]
    acc_ref[...] = jnp.zeros(acc_ref.shape, F32)
    run_ref[...] = jnp.zeros(run_ref.shape, F32)
    row = lax.broadcasted_iota(jnp.int32, (tq, tk), 0)
    col = lax.broadcasted_iota(jnp.int32, (tq, tk), 1)
    diag_mask = col < (row & (tk - 1))
    sub = lambda r: slice(r * tk, (r + 1) * tk)

    def key_blocks(lag):
        blocks = []
        for r in range(n_sub):
            jb = i * n_sub + r - lag
            start = pl.multiple_of(jnp.maximum(jb, 0) * tk, tk)
            blocks.append((start, jnp.where(jb >= 0, 0.0, SB_MASKED_BITS)))
        return blocks

    def scores(cols, blocks, diagonal):
        z = jnp.concatenate([_nt_dot(q_ref[0, sub(r), cols], k_ref[0, pl.ds(start, tk), cols])
                             for r, (start, _) in enumerate(blocks)], axis=0)
        neg_abs = lax.bitcast_convert_type(
            lax.bitcast_convert_type(z, jnp.uint32) | jnp.uint32(0x80000000), F32)
        neg_log_keep = jnp.maximum(z, 0.0) + jnp.log(1.0 + jnp.exp2(neg_abs)) * LOG2E
        if diagonal:
            neg_log_keep = jnp.where(diag_mask, neg_log_keep, 0.0)
            z = jnp.where(diag_mask, z, -SB_MASKED_BITS)
        hi = lax.bitcast_convert_type(
            lax.bitcast_convert_type(neg_log_keep, jnp.uint32) & jnp.uint32(0xFFFF0000), F32)
        lo = neg_log_keep - hi
        c = jnp.dot(jnp.concatenate([hi.astype(BF16), lo.astype(BF16)], axis=1), w2, preferred_element_type=F32)
        return z, c

    def accumulate(hh, cols, blocks, z, c, diagonal):
        run = run_ref[hh]
        if not diagonal:
            run = run + jnp.concatenate([jnp.full((tk, tk), pen, F32) for _, pen in blocks], axis=0)
        a = jnp.exp2(z - c[:, :tk] - run).astype(BF16)
        for r, (start, _) in enumerate(blocks):
            acc_ref[hh, sub(r), :] += jnp.dot(a[sub(r), :], v_ref[0, pl.ds(start, tk), cols],
                                              preferred_element_type=F32)
        run_ref[hh] = run + c[:, tk:]

    lags = [key_blocks(lag) for lag in range(SB_STATIC_LAGS)]
    staged = [[scores(cols, blocks, lag == 0) for lag, blocks in enumerate(lags)] for cols in heads]
    for lag, blocks in enumerate(lags):
        for hh, cols in enumerate(heads):
            accumulate(hh, cols, blocks, *staged[hh][lag], lag == 0)

    for hh, cols in enumerate(heads):
        def cond(carry):
            lag, min_run = carry
            return jnp.logical_and(lag <= i * n_sub + n_sub - 1, min_run < SB_DEAD_BITS)

        def body(carry, hh=hh, cols=cols):
            lag, _ = carry
            blocks = key_blocks(lag)
            z, c = scores(cols, blocks, False)
            accumulate(hh, cols, blocks, z, c, False)
            return lag + 1, jnp.min(run_ref[hh])

        lax.while_loop(cond, body, (jnp.int32(SB_STATIC_LAGS), jnp.min(run_ref[hh])))
        o_ref[0, :, cols] = acc_ref[hh].astype(o_ref.dtype)


def _stick_breaking(qkv, tq, tk, heads_per_step):
    b, s, _ = qkv.shape
    hw = heads_per_step * GROUP_DIM
    groups = SB_HEADS // heads_per_step
    kern = functools.partial(_sb_kernel, tq=tq, tk=tk)
    return pl.pallas_call(
        kern,
        grid=(b, groups, s // tq),
        in_specs=[
            pl.BlockSpec((1, tq, hw), lambda bi, h, i: (bi, i, h)),
            pl.BlockSpec((1, s, hw), lambda bi, h, i: (bi, 0, groups + h)),
            pl.BlockSpec((1, s, hw), lambda bi, h, i: (bi, 0, 2 * groups + h)),
            pl.BlockSpec((2 * tk, 2 * tk), lambda bi, h, i: (0, 0)),
        ],
        out_specs=pl.BlockSpec((1, tq, hw), lambda bi, h, i: (bi, i, h)),
        out_shape=jax.ShapeDtypeStruct((b, s, SB_WIDTH), BF16),
        scratch_shapes=[pltpu.VMEM((heads_per_step, tq, GROUP_DIM), F32),
                        pltpu.VMEM((heads_per_step, tq, tk), F32)],
        compiler_params=_cparams(("arbitrary", "arbitrary", "arbitrary")),
        name="stick_breaking",
    )(qkv, qkv, qkv, _sb_cumsum_matrix(tk))


def _gla_constants():
    c = GLA_CHUNK
    t = np.arange(c)[:, None]
    j = np.arange(c)[None, :]
    dst = np.concatenate([j <= t, np.ones((8, c), bool)], axis=0).astype(np.float32)
    s = j
    masks = []
    for l in range(GLA_LEVELS):
        masks.append(((t ^ s) >> l == 1) & (t > s))
    masks.append(t == s)
    return jnp.asarray(dst, dtype=BF16), jnp.asarray(np.stack(masks).astype(np.float32))


def _split3(x):
    x1 = x.astype(BF16)
    r = x - x1.astype(F32)
    x2 = r.astype(BF16)
    x3 = (r - x2.astype(F32)).astype(BF16)
    return x1, x2, x3


def _gla_kernel(q_ref, k_ref, v_ref, r_ref, a_ref, wa_ref, ba_ref, og_ref, dst_ref, msk_ref,
                o_ref, st_ref):
    @pl.when(pl.program_id(1) == 0)
    def _():
        st_ref[...] = jnp.zeros(st_ref.shape, F32)

    for cc in range(q_ref.shape[1] // GLA_CHUNK):
        _gla_chunk(slice(cc * GLA_CHUNK, (cc + 1) * GLA_CHUNK), q_ref, k_ref, v_ref, r_ref, a_ref,
                   wa_ref, ba_ref, og_ref, dst_ref, msk_ref, o_ref, st_ref)


def _gla_chunk(rows, q_ref, k_ref, v_ref, r_ref, a_ref, wa_ref, ba_ref, og_ref, dst_ref, msk_ref,
               o_ref, st_ref):
    c = GLA_CHUNK
    x = jnp.dot(a_ref[0, rows, :].astype(BF16), wa_ref[...], preferred_element_type=F32) + ba_ref[...]
    log_sig = jnp.minimum(x, 0.0) - jnp.log(1.0 + jnp.exp(-jnp.abs(x)))
    g = log_sig * (LOG2E / GLA_TAU)
    dst = dst_ref[...]
    g1, g2, g3 = _split3(g)
    sums = (jnp.dot(dst, g1, preferred_element_type=F32)
            + jnp.dot(dst, g2, preferred_element_type=F32)
            + jnp.dot(dst, g3, preferred_element_type=F32))
    prefix = sums[:c]
    total = sums[c:c + 1]
    e_in = prefix
    e_out = total - prefix

    row = lax.broadcasted_iota(jnp.int32, g.shape, 0)
    pos = row & 3
    g_prev = pltpu.roll(g, 1, 0)
    g_next = pltpu.roll(g, c - 1, 0)
    e_lvl = [jnp.where((row & 1) == 1, g, 0.0),
             jnp.where(pos == 0, g_next, jnp.where(pos == 1, 0.0, jnp.where(pos == 2, g, g_prev + g)))]
    for l in range(2, GLA_LEVELS):
        m = 1 << l
        blocks = prefix.reshape(c // (2 * m), 2 * m, GLA_WIDTH)
        e_lvl.append((-jnp.abs(blocks - blocks[:, m - 1:m, :])).reshape(c, GLA_WIDTH))

    for h in range(GLA_HEADS):
        cols = slice(h * GROUP_DIM, (h + 1) * GROUP_DIM)
        q = q_ref[0, rows, cols] * (GROUP_DIM ** -0.5)
        k = k_ref[0, rows, cols]
        v = v_ref[0, rows, cols].astype(BF16)
        scores = msk_ref[GLA_LEVELS] * _nt_dot(q.astype(BF16), k.astype(BF16))
        for l in range(GLA_LEVELS):
            xl = jnp.exp2(e_lvl[l][:, cols])
            scores = scores + msk_ref[l] * _nt_dot((q * xl).astype(BF16), (k * xl).astype(BF16))
        o = jnp.dot(scores.astype(BF16), v, preferred_element_type=F32)
        x_in = jnp.exp2(e_in[:, cols])
        st = st_ref[h]
        o = o + _nt_dot((q * x_in).astype(BF16), st.astype(BF16))
        x_out = jnp.exp2(e_out[:, cols])
        x_all = jnp.exp2(total[:, cols])
        vt = v_ref[0, rows, cols].T.astype(BF16)
        st_ref[h] = st * x_all + jnp.dot(vt, (k * x_out).astype(BF16), preferred_element_type=F32)
        ms = jnp.mean(o * o, axis=-1, keepdims=True)
        o = o * lax.rsqrt(ms + RMS_EPS) * og_ref[...]
        o_ref[0, rows, cols] = (o * _silu(r_ref[0, rows, cols])).astype(o_ref.dtype)


def _gla(proj3, w_a2p, b_a2, out_gain, chunks_per_step):
    b, s, _ = proj3.shape
    c = GLA_CHUNK * chunks_per_step
    dst, masks = _gla_constants()
    wide = lambda blk: pl.BlockSpec((1, c, GLA_WIDTH), lambda bi, ci: (bi, ci, blk))
    const2 = lambda shape: pl.BlockSpec(shape, lambda bi, ci: (0, 0))
    a_blk = (POOL_WIDTH + 4 * GLA_WIDTH) // GROUP_DIM
    return pl.pallas_call(
        _gla_kernel,
        grid=(b, s // c),
        in_specs=[
            wide(1), wide(2), wide(3), wide(4),
            pl.BlockSpec((1, c, GROUP_DIM), lambda bi, ci: (bi, ci, a_blk)),
            const2((GROUP_DIM, GLA_WIDTH)),
            const2((1, GLA_WIDTH)),
            const2((1, GROUP_DIM)),
            const2(dst.shape),
            pl.BlockSpec(masks.shape, lambda bi, ci: (0, 0, 0)),
        ],
        out_specs=pl.BlockSpec((1, c, GLA_WIDTH), lambda bi, ci: (bi, ci, 0)),
        out_shape=jax.ShapeDtypeStruct((b, s, GLA_WIDTH), BF16),
        scratch_shapes=[pltpu.VMEM((GLA_HEADS, GROUP_DIM, GROUP_DIM), F32)],
        compiler_params=_cparams(("arbitrary", "arbitrary")),
        name="gla",
    )(proj3, proj3, proj3, proj3, proj3, w_a2p, b_a2, out_gain, dst, masks)


def _merge_kernel(h_ref, yp_ref, ys_ref, yg_ref, wgp_ref, wgs_ref, wgg_ref, wp_ref, ws_ref, wg_ref, o_ref):
    h = h_ref[...]

    def branch(wgate_ref, y_ref, w_ref):
        gate = jax.nn.sigmoid(jnp.dot(h, wgate_ref[...], preferred_element_type=F32))
        return gate * jnp.dot(y_ref[...], w_ref[...], preferred_element_type=F32)

    merged = branch(wgp_ref, yp_ref, wp_ref) + branch(wgs_ref, ys_ref, ws_ref) + branch(wgg_ref, yg_ref, wg_ref)
    o_ref[...] = merged.astype(o_ref.dtype)


def _merge(h, y_pool, y_sb, y_gla, w_gates, w_br_pool, w_br_sb, w_br_gla, layer, *, tm, tn):
    m, d = h.shape
    nj = d // tn
    row = lambda width: pl.BlockSpec((tm, width), lambda i, j: (i, 0))
    gate_w = lambda br: pl.BlockSpec((None, d, tn), lambda i, j: (layer, 0, br * nj + j))
    br_w = lambda width: pl.BlockSpec((None, width, tn), lambda i, j: (layer, 0, j))
    return pl.pallas_call(
        _merge_kernel,
        grid=(m // tm, nj),
        in_specs=[
            row(d), row(POOL_WIDTH), row(SB_WIDTH), row(GLA_WIDTH),
            gate_w(0), gate_w(1), gate_w(2),
            br_w(POOL_WIDTH), br_w(SB_WIDTH), br_w(GLA_WIDTH),
        ],
        out_specs=pl.BlockSpec((tm, tn), lambda i, j: (i, j)),
        out_shape=jax.ShapeDtypeStruct((m, d), BF16),
        compiler_params=_cparams(("arbitrary", "arbitrary")),
        name="gated_merge",
    )(h, y_pool, y_sb, y_gla, w_gates, w_gates, w_gates, w_br_pool, w_br_sb, w_br_gla)


def _resproj_kernel(a_ref, w_ref, x_ref, ga_ref, g_ref, sc_ref, sh_ref, o_ref, h_ref):
    for r in range(a_ref.shape[0] // ROW_CHUNK):
        rows = slice(r * ROW_CHUNK, (r + 1) * ROW_CHUNK)
        y = jnp.dot(a_ref[rows, :], w_ref[...], preferred_element_type=F32)
        x_new = x_ref[rows, :] + ga_ref[0] * y
        o_ref[rows, :] = x_new
        h_ref[rows, :] = _modulated_norm(x_new, g_ref[...], sc_ref[0], sh_ref[0]).astype(BF16)


def _resproj(a, w, layer, x2, gate, gain, scale, shift, *, seq, tm):
    m, kdim = a.shape
    d = w.shape[2]
    tpb = seq // tm
    row = lambda width: pl.BlockSpec((tm, width), lambda i: (i, 0))
    mod = pl.BlockSpec((1, 1, d), lambda i: (i // tpb, 0, 0))
    return pl.pallas_call(
        _resproj_kernel,
        grid=(m // tm,),
        in_specs=[row(kdim), pl.BlockSpec((None, kdim, d), lambda i: (layer, 0, 0)), row(d), mod,
                  pl.BlockSpec((1, d), lambda i: (0, 0)), mod, mod],
        out_specs=[row(d), row(d)],
        out_shape=[jax.ShapeDtypeStruct((m, d), F32), jax.ShapeDtypeStruct((m, d), BF16)],
        compiler_params=_cparams(("arbitrary",)),
        name="residual_proj",
    )(a, w, x2, gate, gain, scale, shift)


def _swiglu_kernel(*refs, emit_next, x_chunks):
    if emit_next:
        h_ref, x_ref, ga_ref, wg_ref, wu_ref, wd_ref, g_ref, sc_ref, sh_ref, o_ref, hn_ref = refs
    else:
        h_ref, x_ref, ga_ref, wg_ref, wu_ref, wd_ref, o_ref = refs
    f = pl.program_id(1)
    last = pl.num_programs(1) - 1
    xr = x_ref.shape[0]

    def gated_partial(rows):
        h = h_ref[rows, :]
        gate = jnp.dot(h, wg_ref[...], preferred_element_type=F32)
        up = jnp.dot(h, wu_ref[...], preferred_element_type=F32)
        act = (_silu(gate) * up).astype(BF16)
        return ga_ref[0] * jnp.dot(act, wd_ref[...], preferred_element_type=F32)

    row_chunks = [slice(r * ROW_CHUNK, (r + 1) * ROW_CHUNK) for r in range(h_ref.shape[0] // ROW_CHUNK)]

    @pl.when(f == 0)
    def _():
        for rows in row_chunks:
            o_ref[rows, :] = gated_partial(rows)
        o_ref[0:xr, :] += x_ref[...]

    @pl.when(jnp.logical_and(f > 0, f < last))
    def _():
        x_rows = pl.ds(pl.multiple_of(jnp.minimum(f, x_chunks - 1) * xr, xr), xr)
        o_ref[x_rows, :] += jnp.where(f < x_chunks, x_ref[...], 0.0)
        for rows in row_chunks:
            o_ref[rows, :] += gated_partial(rows)

    @pl.when(f == last)
    def _():
        for rows in row_chunks:
            x_new = o_ref[rows, :] + gated_partial(rows)
            o_ref[rows, :] = x_new
            if emit_next:
                hn_ref[rows, :] = _modulated_norm(x_new, g_ref[...], sc_ref[0], sh_ref[0]).astype(BF16)


def _swiglu(h, x2, gate, w_gate, w_up, w_down, layer, next_norm, *, seq, tm, tf):
    m, d = x2.shape
    dff = w_gate.shape[2]
    tpb = seq // tm
    n_steps = dff // tf
    x_chunks = 8
    assert x_chunks < n_steps and tm % (x_chunks * 8) == 0
    emit_next = next_norm is not None
    row = pl.BlockSpec((tm, d), lambda i, f: (i, 0))
    mod = pl.BlockSpec((1, 1, d), lambda i, f: (i // tpb, 0, 0))
    in_specs = [row,
                pl.BlockSpec((tm // x_chunks, d), lambda i, f: (i * x_chunks + jnp.minimum(f, x_chunks - 1), 0)),
                mod,
                pl.BlockSpec((None, d, tf), lambda i, f: (layer, 0, f)),
                pl.BlockSpec((None, d, tf), lambda i, f: (layer, 0, f)),
                pl.BlockSpec((None, tf, d), lambda i, f: (layer, f, 0))]
    args = [h, x2, gate, w_gate, w_up, w_down]
    out_specs = [row]
    out_shape = [jax.ShapeDtypeStruct((m, d), F32)]
    if emit_next:
        in_specs += [pl.BlockSpec((1, d), lambda i, f: (0, 0)), mod, mod]
        args += list(next_norm)
        out_specs.append(row)
        out_shape.append(jax.ShapeDtypeStruct((m, d), BF16))
    outs = pl.pallas_call(
        functools.partial(_swiglu_kernel, emit_next=emit_next, x_chunks=x_chunks),
        grid=(m // tm, n_steps),
        in_specs=in_specs,
        out_specs=out_specs,
        out_shape=out_shape,
        compiler_params=_cparams(("arbitrary", "arbitrary")),
        name="swiglu",
    )(*args)
    return (outs[0], outs[1]) if emit_next else (outs[0], None)


class _Tiles(NamedTuple):
    rows: int
    rows_f32: int
    pool_rows: int
    qkv_cols: int
    merge_cols: int
    ff_cols: int
    gla_chunks: int
    sb_heads: int


def _tile(n, pref):
    t = min(n, pref)
    assert n % t == 0, (n, t)
    return t


def _tiles(seq):
    return _Tiles(rows=_tile(seq, 1024), rows_f32=_tile(seq, 512), pool_rows=_tile(seq, 2048),
                  qkv_cols=1024, merge_cols=512, ff_cols=512,
                  gla_chunks=_tile(seq // GLA_CHUNK, 4), sb_heads=2)


def _mod_vectors(mod_l):
    d = D_MODEL
    return [mod_l[:, None, k * d:(k + 1) * d] for k in range(6)]


class _Weights(NamedTuple):
    f32cols: jax.Array
    sb: jax.Array
    gates: jax.Array
    br_pool: jax.Array
    br_sb: jax.Array
    br_gla: jax.Array
    out: jax.Array
    ff_gate: jax.Array
    ff_up: jax.Array
    ff_down: jax.Array


def _layer(x2, h1, mod_l, next_norm, batch, seq, layer, w, w_pool, pool_scale, sb_q_gain, sb_k_gain,
           gla_w_a2, gla_b_a2, gla_out_gain, g_norm2):
    d = D_MODEL
    t = _tiles(seq)
    _, _, ga1, sh2, sc2, ga2 = _mod_vectors(mod_l)

    proj = _proj(h1, w.f32cols, layer, jnp.ones((1, F32_COLS), F32), tm=t.rows_f32, tn=F32_COLS,
                 n_norm_tiles=0, out_dtype=F32)
    q_scale = LOG2E / math.sqrt(GROUP_DIM)
    colgain = jnp.concatenate([jnp.tile(sb_q_gain * q_scale, SB_HEADS), jnp.tile(sb_k_gain, SB_HEADS),
                               jnp.ones((SB_WIDTH,), F32)]).reshape(1, 3 * SB_WIDTH)
    assert (2 * SB_WIDTH) % t.qkv_cols == 0
    qkv = _proj(h1, w.sb, layer, colgain, tm=t.rows, tn=t.qkv_cols, n_norm_tiles=2 * SB_WIDTH // t.qkv_cols,
                out_dtype=BF16)

    proj3 = proj.reshape(batch, seq, F32_COLS)
    y_pool = _pool(proj3, w_pool.astype(BF16), pool_scale, ts=t.pool_rows)
    y_sb = _stick_breaking(qkv.reshape(batch, seq, 3 * SB_WIDTH), tq=t.rows, tk=GROUP_DIM,
                           heads_per_step=t.sb_heads)
    w_a2p = jnp.pad(gla_w_a2, ((0, GROUP_DIM - GLA_RANK), (0, 0))).astype(BF16)
    y_gla = _gla(proj3, w_a2p, gla_b_a2.reshape(1, GLA_WIDTH), gla_out_gain.reshape(1, GROUP_DIM),
                 chunks_per_step=t.gla_chunks)

    m = batch * seq
    merged = _merge(h1, y_pool.reshape(m, POOL_WIDTH), y_sb.reshape(m, SB_WIDTH), y_gla.reshape(m, GLA_WIDTH),
                    w.gates, w.br_pool, w.br_sb, w.br_gla, layer, tm=t.rows, tn=t.merge_cols)
    x2, h2 = _resproj(merged, w.out, layer, x2, ga1, g_norm2.reshape(1, d), sc2, sh2, seq=seq, tm=t.rows_f32)
    return _swiglu(h2, x2, ga2, w.ff_gate, w.ff_up, w.ff_down, layer, next_norm, seq=seq, tm=t.rows,
                   tf=t.ff_cols)


def kernel(x, c, w_ada, b_ada, g_norm1, w_in, w_pool, pool_scale, sb_q_gain, sb_k_gain, gla_w_a2, gla_b_a2, gla_out_gain, w_br_pool, w_br_sb, w_br_gla, w_out, g_norm2, w_ff_gate, w_ff_up, w_ff_down):
    batch, seq, d = x.shape
    depth = w_ada.shape[0]
    c_pad = jnp.pad(c, ((0, 8 - batch % 8 if batch % 8 else 0), (0, 0)))
    mod = _ada(c_pad, w_ada, b_ada)[:, :batch]
    x2 = x.reshape(batch * seq, d)
    w_in_bf16 = jnp.pad(w_in.astype(BF16), ((0, 0), (0, 0), (0, (-W_IN_COLS) % GROUP_DIM)))
    w = _Weights(*_split_w_in(w_in_bf16), *[a.astype(BF16) for a in (w_br_pool, w_br_sb, w_br_gla, w_out,
                                                               w_ff_gate, w_ff_up, w_ff_down)])

    def first_norm(l):
        sh1, sc1 = _mod_vectors(mod[l])[:2]
        return g_norm1[l].reshape(1, d), sc1, sh1

    h1 = _norm(x2, *first_norm(0), seq=seq, tm=_tiles(seq).rows_f32)
    for l in range(depth):
        next_norm = first_norm(l + 1) if l + 1 < depth else None
        x2, h1 = _layer(x2, h1, mod[l], next_norm, batch, seq, l, w, w_pool[l], pool_scale[l],
                        sb_q_gain[l], sb_k_gain[l], gla_w_a2[l], gla_b_a2[l], gla_out_gain[l], g_norm2[l])
    return x2.reshape(batch, seq, d)
```

```python
import functools
import math
from typing import NamedTuple

import numpy as np
import jax
import jax.numpy as jnp
from jax import lax
from jax.experimental import pallas as pl
from jax.experimental.pallas import tpu as pltpu

F32 = jnp.float32
BF16 = jnp.bfloat16

D_MODEL = 2048
POOL_WINDOWS = (2, 4, 8, 16)
POOL_GROUPS = 4
GROUP_DIM = 128
POOL_WIDTH = POOL_GROUPS * GROUP_DIM
SB_HEADS = 8
SB_WIDTH = SB_HEADS * GROUP_DIM
GLA_HEADS = 4
GLA_WIDTH = GLA_HEADS * GROUP_DIM
GLA_RANK = 16
GLA_TAU = 16.0
GLA_CHUNK = 128
GLA_LEVELS = 7
N_BRANCH = 3
D_FF = 5632
RMS_EPS = 1e-6
LOG2E = 1.4426950408889634
POOL_HALO = 16
SB_DEAD_BITS = 160.0
SB_MASKED_BITS = 1.0e4
SB_STATIC_LAGS = 3

VMEM_LIMIT = 56 * 1024 * 1024
MXU_WIDTH = 256
ROW_CHUNK = 256


def _cparams(sem):
    return pltpu.CompilerParams(dimension_semantics=sem, vmem_limit_bytes=VMEM_LIMIT)


def _nt_dot(a, b):
    return lax.dot_general(a, b, (((1,), (1,)), ((), ())), preferred_element_type=F32)


def _silu(x):
    return x * jax.nn.sigmoid(x)


def _ada_kernel(c_ref, w_ref, b_ref, o_ref):
    c = c_ref[...]
    a = _silu(c).astype(BF16)
    o_ref[0] = jnp.dot(a, w_ref[0].astype(BF16), preferred_element_type=F32) + b_ref[0]


def _ada(c_pad, w_ada, b_ada, tn=1024):
    depth, d, n = w_ada.shape
    rows = c_pad.shape[0]
    return pl.pallas_call(
        _ada_kernel,
        grid=(depth, n // tn),
        in_specs=[
            pl.BlockSpec((rows, d), lambda l, j: (0, 0)),
            pl.BlockSpec((1, d, tn), lambda l, j: (l, 0, j)),
            pl.BlockSpec((1, 1, tn), lambda l, j: (l, 0, j)),
        ],
        out_specs=pl.BlockSpec((1, rows, tn), lambda l, j: (l, 0, j)),
        out_shape=jax.ShapeDtypeStruct((depth, rows, n), F32),
        compiler_params=_cparams(("arbitrary", "arbitrary")),
        name="ada_modulation",
    )(c_pad, w_ada, b_ada.reshape(depth, 1, n))


W_IN_SB = POOL_WIDTH
W_IN_GLA = W_IN_SB + 3 * SB_WIDTH
W_IN_A = W_IN_GLA + 4 * GLA_WIDTH
W_IN_GATE = W_IN_A + GLA_RANK
W_IN_COLS = W_IN_GATE + N_BRANCH * D_MODEL
A_LOW_COL = POOL_WIDTH + 4 * GLA_WIDTH
F32_COLS = A_LOW_COL + MXU_WIDTH


def _split_w_in_kernel(w_ref, f_ref, sb_ref, g_ref):
    f_ref[0, :, :POOL_WIDTH] = w_ref[0, :, :W_IN_SB]
    f_ref[0, :, POOL_WIDTH:A_LOW_COL] = w_ref[0, :, W_IN_GLA:W_IN_A]
    f_ref[0, :, A_LOW_COL:] = jnp.zeros((f_ref.shape[1], MXU_WIDTH), BF16)
    f_ref[0, :, A_LOW_COL:A_LOW_COL + GLA_RANK] = w_ref[0, :, W_IN_A:W_IN_GATE]
    sb_ref[0] = w_ref[0, :, W_IN_SB:W_IN_GLA]
    g_ref[0] = w_ref[0, :, W_IN_GATE:]


def _split_w_in(w_in, tr=512):
    depth, d, n = w_in.shape
    assert n == W_IN_COLS and w_in.dtype == BF16
    widths = (F32_COLS, 3 * SB_WIDTH, N_BRANCH * D_MODEL)
    return pl.pallas_call(
        _split_w_in_kernel,
        grid=(depth, d // tr),
        in_specs=[pl.BlockSpec((1, tr, n), lambda l, i: (l, i, 0))],
        out_specs=[pl.BlockSpec((1, tr, w), lambda l, i: (l, i, 0)) for w in widths],
        out_shape=[jax.ShapeDtypeStruct((depth, d, w), BF16) for w in widths],
        compiler_params=_cparams(("arbitrary", "arbitrary")),
        name="split_w_in",
    )(w_in)


def _modulated_norm(x, gain, scale, shift):
    ms = jnp.mean(x * x, axis=-1, keepdims=True)
    y = x * lax.rsqrt(ms + RMS_EPS) * gain
    return y * (1.0 + scale) + shift


def _group_rmsnorm(acc, colgain):
    out = []
    for c in range(acc.shape[1] // GROUP_DIM):
        sl = slice(c * GROUP_DIM, (c + 1) * GROUP_DIM)
        blk = acc[:, sl]
        ms = jnp.mean(blk * blk, axis=-1, keepdims=True)
        out.append(blk * lax.rsqrt(ms + RMS_EPS) * colgain[:, sl])
    return jnp.concatenate(out, axis=1)


def _norm_kernel(x_ref, g_ref, sc_ref, sh_ref, h_ref):
    h_ref[...] = _modulated_norm(x_ref[...], g_ref[...], sc_ref[0], sh_ref[0]).astype(BF16)


def _norm(x2, gain, scale, shift, *, seq, tm):
    m, d = x2.shape
    tpb = seq // tm
    mod = pl.BlockSpec((1, 1, d), lambda i: (i // tpb, 0, 0))
    return pl.pallas_call(
        _norm_kernel,
        grid=(m // tm,),
        in_specs=[pl.BlockSpec((tm, d), lambda i: (i, 0)), pl.BlockSpec((1, d), lambda i: (0, 0)), mod, mod],
        out_specs=pl.BlockSpec((tm, d), lambda i: (i, 0)),
        out_shape=jax.ShapeDtypeStruct((m, d), BF16),
        compiler_params=_cparams(("arbitrary",)),
        name="modulated_norm",
    )(x2, gain, scale, shift)


def _proj_kernel(h_ref, w_ref, cg_ref, o_ref, *, n_norm_tiles, tn):
    j = pl.program_id(1)
    h = h_ref[...]

    def tile(normed):
        for c in range(tn // MXU_WIDTH):
            cols = slice(c * MXU_WIDTH, (c + 1) * MXU_WIDTH)
            acc = jnp.dot(h, w_ref[:, cols], preferred_element_type=F32)
            if normed:
                acc = _group_rmsnorm(acc, cg_ref[:, cols])
            o_ref[:, cols] = acc.astype(o_ref.dtype)

    if n_norm_tiles == 0:
        tile(False)
    else:
        pl.when(j < n_norm_tiles)(lambda: tile(True))
        pl.when(j >= n_norm_tiles)(lambda: tile(False))


def _proj(h, w, layer, colgain, *, tm, tn, n_norm_tiles, out_dtype):
    m, d = h.shape
    n = w.shape[2]
    kern = functools.partial(_proj_kernel, n_norm_tiles=n_norm_tiles, tn=tn)
    return pl.pallas_call(
        kern,
        grid=(m // tm, n // tn),
        in_specs=[
            pl.BlockSpec((tm, d), lambda i, j: (i, 0)),
            pl.BlockSpec((None, d, tn), lambda i, j: (layer, 0, j)),
            pl.BlockSpec((1, tn), lambda i, j: (0, j)),
        ],
        out_specs=pl.BlockSpec((tm, tn), lambda i, j: (i, j)),
        out_shape=jax.ShapeDtypeStruct((m, n), out_dtype),
        compiler_params=_cparams(("arbitrary", "arbitrary")),
        name="in_proj",
    )(h, w, colgain)


def _pool_kernel(u_ref, w_ref, ps_ref, o_ref, ext_ref, *, ts):
    i = pl.program_id(1)

    @pl.when(i == 0)
    def _():
        ext_ref[0:POOL_HALO, :] = jnp.zeros((POOL_HALO, POOL_WIDTH), F32)

    @pl.when(i > 0)
    def _():
        ext_ref[0:POOL_HALO, :] = ext_ref[ts:ts + POOL_HALO, :]

    ext_ref[POOL_HALO:, :] = u_ref[0]
    pos1 = (i * ts + 1 + lax.broadcasted_iota(jnp.int32, (ts, GROUP_DIM), 0)).astype(F32)
    for g, w in enumerate(POOL_WINDOWS):
        cols = slice(g * GROUP_DIM, (g + 1) * GROUP_DIM)
        u = ext_ref[POOL_HALO:, cols]
        win = u
        for k in range(1, w):
            win = win + ext_ref[POOL_HALO - k:POOL_HALO - k + ts, cols]
        pooled = win / jnp.minimum(pos1, float(w)) - u
        y = jnp.dot(pooled.astype(BF16), w_ref[g], preferred_element_type=F32) * ps_ref[g]
        o_ref[0, :, cols] = y.astype(o_ref.dtype)


def _pool(proj3, w_pool, pool_scale, ts):
    b, s, _ = proj3.shape
    kern = functools.partial(_pool_kernel, ts=ts)
    return pl.pallas_call(
        kern,
        grid=(b, s // ts),
        in_specs=[
            pl.BlockSpec((1, ts, POOL_WIDTH), lambda bi, i: (bi, i, 0)),
            pl.BlockSpec((POOL_GROUPS, GROUP_DIM, GROUP_DIM), lambda bi, i: (0, 0, 0)),
            pl.BlockSpec((POOL_GROUPS, 1, GROUP_DIM), lambda bi, i: (0, 0, 0)),
        ],
        out_specs=pl.BlockSpec((1, ts, POOL_WIDTH), lambda bi, i: (bi, i, 0)),
        out_shape=jax.ShapeDtypeStruct((b, s, POOL_WIDTH), BF16),
        scratch_shapes=[pltpu.VMEM((ts + POOL_HALO, POOL_WIDTH), F32)],
        compiler_params=_cparams(("arbitrary", "arbitrary")),
        name="pool_mixer",
    )(proj3, w_pool, pool_scale.reshape(POOL_GROUPS, 1, GROUP_DIM))


def _sb_cumsum_matrix(tk):
    j = np.arange(tk)[:, None]
    s = np.arange(tk)[None, :]
    one = np.concatenate([(j >= s).astype(np.float32), np.ones((tk, tk), np.float32)], axis=1)
    return jnp.asarray(np.concatenate([one, one], axis=0), dtype=BF16)


def _sb_kernel(q_ref, k_ref, v_ref, w2_ref, o_ref, acc_ref, run_ref, *, tq, tk):
    i = pl.program_id(2)
    w2 = w2_ref[...]
    n_sub = tq // tk
    heads = [slice(hh * GROUP_DIM, (hh + 1) * GROUP_DIM) for hh in range(q_ref.shape[2] // GROUP_DIM)]
    acc_ref[...] = jnp.zeros(acc_ref.shape, F32)
    run_ref[...] = jnp.zeros(run_ref.shape, F32)
    row = lax.broadcasted_iota(jnp.int32, (tq, tk), 0)
    col = lax.broadcasted_iota(jnp.int32, (tq, tk), 1)
    diag_mask = col < (row & (tk - 1))
    sub = lambda r: slice(r * tk, (r + 1) * tk)

    def key_blocks(lag):
        blocks = []
        for r in range(n_sub):
            jb = i * n_sub + r - lag
            start = pl.multiple_of(jnp.maximum(jb, 0) * tk, tk)
            blocks.append((start, jnp.where(jb >= 0, 0.0, SB_MASKED_BITS)))
        return blocks

    def scores(cols, blocks, diagonal):
        z = jnp.concatenate([_nt_dot(q_ref[0, sub(r), cols], k_ref[0, pl.ds(start, tk), cols])
                             for r, (start, _) in enumerate(blocks)], axis=0)
        neg_abs = lax.bitcast_convert_type(
            lax.bitcast_convert_type(z, jnp.uint32) | jnp.uint32(0x80000000), F32)
        neg_log_keep = jnp.maximum(z, 0.0) + jnp.log(1.0 + jnp.exp2(neg_abs)) * LOG2E
        if diagonal:
            neg_log_keep = jnp.where(diag_mask, neg_log_keep, 0.0)
            z = jnp.where(diag_mask, z, -SB_MASKED_BITS)
        hi = lax.bitcast_convert_type(
            lax.bitcast_convert_type(neg_log_keep, jnp.uint32) & jnp.uint32(0xFFFF0000), F32)
        lo = neg_log_keep - hi
        c = jnp.dot(jnp.concatenate([hi.astype(BF16), lo.astype(BF16)], axis=1), w2, preferred_element_type=F32)
        return z, c

    def accumulate(hh, cols, blocks, z, c, diagonal):
        run = run_ref[hh]
        if not diagonal:
            run = run + jnp.concatenate([jnp.full((tk, tk), pen, F32) for _, pen in blocks], axis=0)
        a = jnp.exp2(z - c[:, :tk] - run).astype(BF16)
        for r, (start, _) in enumerate(blocks):
            acc_ref[hh, sub(r), :] += jnp.dot(a[sub(r), :], v_ref[0, pl.ds(start, tk), cols],
                                              preferred_element_type=F32)
        run_ref[hh] = run + c[:, tk:]

    lags = [key_blocks(lag) for lag in range(SB_STATIC_LAGS)]
    staged = [[scores(cols, blocks, lag == 0) for lag, blocks in enumerate(lags)] for cols in heads]
    for lag, blocks in enumerate(lags):
        for hh, cols in enumerate(heads):
            accumulate(hh, cols, blocks, *staged[hh][lag], lag == 0)

    for hh, cols in enumerate(heads):
        def cond(carry):
            lag, min_run = carry
            return jnp.logical_and(lag <= i * n_sub + n_sub - 1, min_run < SB_DEAD_BITS)

        def body(carry, hh=hh, cols=cols):
            lag, _ = carry
            blocks = key_blocks(lag)
            z, c = scores(cols, blocks, False)
            accumulate(hh, cols, blocks, z, c, False)
            return lag + 1, jnp.min(run_ref[hh])

        lax.while_loop(cond, body, (jnp.int32(SB_STATIC_LAGS), jnp.min(run_ref[hh])))
        o_ref[0, :, cols] = acc_ref[hh].astype(o_ref.dtype)


def _stick_breaking(qkv, tq, tk, heads_per_step):
    b, s, _ = qkv.shape
    hw = heads_per_step * GROUP_DIM
    groups = SB_HEADS // heads_per_step
    kern = functools.partial(_sb_kernel, tq=tq, tk=tk)
    return pl.pallas_call(
        kern,
        grid=(b, groups, s // tq),
        in_specs=[
            pl.BlockSpec((1, tq, hw), lambda bi, h, i: (bi, i, h)),
            pl.BlockSpec((1, s, hw), lambda bi, h, i: (bi, 0, groups + h)),
            pl.BlockSpec((1, s, hw), lambda bi, h, i: (bi, 0, 2 * groups + h)),
            pl.BlockSpec((2 * tk, 2 * tk), lambda bi, h, i: (0, 0)),
        ],
        out_specs=pl.BlockSpec((1, tq, hw), lambda bi, h, i: (bi, i, h)),
        out_shape=jax.ShapeDtypeStruct((b, s, SB_WIDTH), BF16),
        scratch_shapes=[pltpu.VMEM((heads_per_step, tq, GROUP_DIM), F32),
                        pltpu.VMEM((heads_per_step, tq, tk), F32)],
        compiler_params=_cparams(("arbitrary", "arbitrary", "arbitrary")),
        name="stick_breaking",
    )(qkv, qkv, qkv, _sb_cumsum_matrix(tk))


def _gla_constants():
    c = GLA_CHUNK
    t = np.arange(c)[:, None]
    j = np.arange(c)[None, :]
    dst = np.concatenate([j <= t, np.ones((8, c), bool)], axis=0).astype(np.float32)
    s = j
    masks = []
    for l in range(GLA_LEVELS):
        masks.append(((t ^ s) >> l == 1) & (t > s))
    masks.append(t == s)
    return jnp.asarray(dst, dtype=BF16), jnp.asarray(np.stack(masks).astype(np.float32))


def _split3(x):
    x1 = x.astype(BF16)
    r = x - x1.astype(F32)
    x2 = r.astype(BF16)
    x3 = (r - x2.astype(F32)).astype(BF16)
    return x1, x2, x3


def _gla_kernel(q_ref, k_ref, v_ref, r_ref, a_ref, wa_ref, ba_ref, og_ref, dst_ref, msk_ref,
                o_ref, st_ref):
    @pl.when(pl.program_id(1) == 0)
    def _():
        st_ref[...] = jnp.zeros(st_ref.shape, F32)

    for cc in range(q_ref.shape[1] // GLA_CHUNK):
        _gla_chunk(slice(cc * GLA_CHUNK, (cc + 1) * GLA_CHUNK), q_ref, k_ref, v_ref, r_ref, a_ref,
                   wa_ref, ba_ref, og_ref, dst_ref, msk_ref, o_ref, st_ref)


def _gla_chunk(rows, q_ref, k_ref, v_ref, r_ref, a_ref, wa_ref, ba_ref, og_ref, dst_ref, msk_ref,
               o_ref, st_ref):
    c = GLA_CHUNK
    x = jnp.dot(a_ref[0, rows, :].astype(BF16), wa_ref[...], preferred_element_type=F32) + ba_ref[...]
    log_sig = jnp.minimum(x, 0.0) - jnp.log(1.0 + jnp.exp(-jnp.abs(x)))
    g = log_sig * (LOG2E / GLA_TAU)
    dst = dst_ref[...]
    g1, g2, g3 = _split3(g)
    sums = (jnp.dot(dst, g1, preferred_element_type=F32)
            + jnp.dot(dst, g2, preferred_element_type=F32)
            + jnp.dot(dst, g3, preferred_element_type=F32))
    prefix = sums[:c]
    total = sums[c:c + 1]
    e_in = prefix
    e_out = total - prefix

    row = lax.broadcasted_iota(jnp.int32, g.shape, 0)
    pos = row & 3
    g_prev = pltpu.roll(g, 1, 0)
    g_next = pltpu.roll(g, c - 1, 0)
    e_lvl = [jnp.where((row & 1) == 1, g, 0.0),
             jnp.where(pos == 0, g_next, jnp.where(pos == 1, 0.0, jnp.where(pos == 2, g, g_prev + g)))]
    for l in range(2, GLA_LEVELS):
        m = 1 << l
        blocks = prefix.reshape(c // (2 * m), 2 * m, GLA_WIDTH)
        e_lvl.append((-jnp.abs(blocks - blocks[:, m - 1:m, :])).reshape(c, GLA_WIDTH))

    for h in range(GLA_HEADS):
        cols = slice(h * GROUP_DIM, (h + 1) * GROUP_DIM)
        q = q_ref[0, rows, cols] * (GROUP_DIM ** -0.5)
        k = k_ref[0, rows, cols]
        v = v_ref[0, rows, cols].astype(BF16)
        scores = msk_ref[GLA_LEVELS] * _nt_dot(q.astype(BF16), k.astype(BF16))
        for l in range(GLA_LEVELS):
            xl = jnp.exp2(e_lvl[l][:, cols])
            scores = scores + msk_ref[l] * _nt_dot((q * xl).astype(BF16), (k * xl).astype(BF16))
        o = jnp.dot(scores.astype(BF16), v, preferred_element_type=F32)
        x_in = jnp.exp2(e_in[:, cols])
        st = st_ref[h]
        o = o + _nt_dot((q * x_in).astype(BF16), st.astype(BF16))
        x_out = jnp.exp2(e_out[:, cols])
        x_all = jnp.exp2(total[:, cols])
        vt = v_ref[0, rows, cols].T.astype(BF16)
        st_ref[h] = st * x_all + jnp.dot(vt, (k * x_out).astype(BF16), preferred_element_type=F32)
        ms = jnp.mean(o * o, axis=-1, keepdims=True)
        o = o * lax.rsqrt(ms + RMS_EPS) * og_ref[...]
        o_ref[0, rows, cols] = (o * _silu(r_ref[0, rows, cols])).astype(o_ref.dtype)


def _gla(proj3, w_a2p, b_a2, out_gain, chunks_per_step):
    b, s, _ = proj3.shape
    c = GLA_CHUNK * chunks_per_step
    dst, masks = _gla_constants()
    wide = lambda blk: pl.BlockSpec((1, c, GLA_WIDTH), lambda bi, ci: (bi, ci, blk))
    const2 = lambda shape: pl.BlockSpec(shape, lambda bi, ci: (0, 0))
    a_blk = (POOL_WIDTH + 4 * GLA_WIDTH) // GROUP_DIM
    return pl.pallas_call(
        _gla_kernel,
        grid=(b, s // c),
        in_specs=[
            wide(1), wide(2), wide(3), wide(4),
            pl.BlockSpec((1, c, GROUP_DIM), lambda bi, ci: (bi, ci, a_blk)),
            const2((GROUP_DIM, GLA_WIDTH)),
            const2((1, GLA_WIDTH)),
            const2((1, GROUP_DIM)),
            const2(dst.shape),
            pl.BlockSpec(masks.shape, lambda bi, ci: (0, 0, 0)),
        ],
        out_specs=pl.BlockSpec((1, c, GLA_WIDTH), lambda bi, ci: (bi, ci, 0)),
        out_shape=jax.ShapeDtypeStruct((b, s, GLA_WIDTH), BF16),
        scratch_shapes=[pltpu.VMEM((GLA_HEADS, GROUP_DIM, GROUP_DIM), F32)],
        compiler_params=_cparams(("arbitrary", "arbitrary")),
        name="gla",
    )(proj3, proj3, proj3, proj3, proj3, w_a2p, b_a2, out_gain, dst, masks)


def _merge_kernel(h_ref, yp_ref, ys_ref, yg_ref, wgp_ref, wgs_ref, wgg_ref, wp_ref, ws_ref, wg_ref, o_ref):
    h = h_ref[...]

    def branch(wgate_ref, y_ref, w_ref):
        gate = jax.nn.sigmoid(jnp.dot(h, wgate_ref[...], preferred_element_type=F32))
        return gate * jnp.dot(y_ref[...], w_ref[...], preferred_element_type=F32)

    merged = branch(wgp_ref, yp_ref, wp_ref) + branch(wgs_ref, ys_ref, ws_ref) + branch(wgg_ref, yg_ref, wg_ref)
    o_ref[...] = merged.astype(o_ref.dtype)


def _merge(h, y_pool, y_sb, y_gla, w_gates, w_br_pool, w_br_sb, w_br_gla, layer, *, tm, tn):
    m, d = h.shape
    nj = d // tn
    row = lambda width: pl.BlockSpec((tm, width), lambda i, j: (i, 0))
    gate_w = lambda br: pl.BlockSpec((None, d, tn), lambda i, j: (layer, 0, br * nj + j))
    br_w = lambda width: pl.BlockSpec((None, width, tn), lambda i, j: (layer, 0, j))
    return pl.pallas_call(
        _merge_kernel,
        grid=(m // tm, nj),
        in_specs=[
            row(d), row(POOL_WIDTH), row(SB_WIDTH), row(GLA_WIDTH),
            gate_w(0), gate_w(1), gate_w(2),
            br_w(POOL_WIDTH), br_w(SB_WIDTH), br_w(GLA_WIDTH),
        ],
        out_specs=pl.BlockSpec((tm, tn), lambda i, j: (i, j)),
        out_shape=jax.ShapeDtypeStruct((m, d), BF16),
        compiler_params=_cparams(("arbitrary", "arbitrary")),
        name="gated_merge",
    )(h, y_pool, y_sb, y_gla, w_gates, w_gates, w_gates, w_br_pool, w_br_sb, w_br_gla)


def _resproj_kernel(a_ref, w_ref, x_ref, ga_ref, g_ref, sc_ref, sh_ref, o_ref, h_ref):
    for r in range(a_ref.shape[0] // ROW_CHUNK):
        rows = slice(r * ROW_CHUNK, (r + 1) * ROW_CHUNK)
        y = jnp.dot(a_ref[rows, :], w_ref[...], preferred_element_type=F32)
        x_new = x_ref[rows, :] + ga_ref[0] * y
        o_ref[rows, :] = x_new
        h_ref[rows, :] = _modulated_norm(x_new, g_ref[...], sc_ref[0], sh_ref[0]).astype(BF16)


def _resproj(a, w, layer, x2, gate, gain, scale, shift, *, seq, tm):
    m, kdim = a.shape
    d = w.shape[2]
    tpb = seq // tm
    row = lambda width: pl.BlockSpec((tm, width), lambda i: (i, 0))
    mod = pl.BlockSpec((1, 1, d), lambda i: (i // tpb, 0, 0))
    return pl.pallas_call(
        _resproj_kernel,
        grid=(m // tm,),
        in_specs=[row(kdim), pl.BlockSpec((None, kdim, d), lambda i: (layer, 0, 0)), row(d), mod,
                  pl.BlockSpec((1, d), lambda i: (0, 0)), mod, mod],
        out_specs=[row(d), row(d)],
        out_shape=[jax.ShapeDtypeStruct((m, d), F32), jax.ShapeDtypeStruct((m, d), BF16)],
        compiler_params=_cparams(("arbitrary",)),
        name="residual_proj",
    )(a, w, x2, gate, gain, scale, shift)


def _swiglu_kernel(*refs, emit_next, x_chunks):
    if emit_next:
        h_ref, x_ref, ga_ref, wg_ref, wu_ref, wd_ref, g_ref, sc_ref, sh_ref, o_ref, hn_ref = refs
    else:
        h_ref, x_ref, ga_ref, wg_ref, wu_ref, wd_ref, o_ref = refs
    f = pl.program_id(1)
    last = pl.num_programs(1) - 1
    xw = x_ref.shape[1]

    def gated_partial(rows):
        h = h_ref[rows, :]
        gate = jnp.dot(h, wg_ref[...], preferred_element_type=F32)
        up = jnp.dot(h, wu_ref[...], preferred_element_type=F32)
        act = (_silu(gate) * up).astype(BF16)
        return ga_ref[0] * jnp.dot(act, wd_ref[...], preferred_element_type=F32)

    row_chunks = [slice(r * ROW_CHUNK, (r + 1) * ROW_CHUNK) for r in range(h_ref.shape[0] // ROW_CHUNK)]

    x_cols = [slice(c * xw, (c + 1) * xw) for c in range(x_chunks)]

    @pl.when(f == 0)
    def _():
        for rows in row_chunks:
            part = gated_partial(rows)
            o_ref[rows, x_cols[0]] = part[:, x_cols[0]] + x_ref[rows, :]
            o_ref[rows, xw:] = part[:, xw:]

    @pl.when(jnp.logical_and(f > 0, f < x_chunks))
    def _():
        for rows in row_chunks:
            part = gated_partial(rows)
            x_chunk = x_ref[rows, :]
            o_ref[rows, x_cols[0]] += part[:, x_cols[0]]
            for c in range(1, x_chunks):
                o_ref[rows, x_cols[c]] += part[:, x_cols[c]] + jnp.where(f == c, x_chunk, 0.0)

    @pl.when(jnp.logical_and(f >= x_chunks, f < last))
    def _():
        for rows in row_chunks:
            o_ref[rows, :] += gated_partial(rows)

    @pl.when(f == last)
    def _():
        for rows in row_chunks:
            x_new = o_ref[rows, :] + gated_partial(rows)
            o_ref[rows, :] = x_new
            if emit_next:
                hn_ref[rows, :] = _modulated_norm(x_new, g_ref[...], sc_ref[0], sh_ref[0]).astype(BF16)


def _swiglu(h, x2, gate, w_gate, w_up, w_down, layer, next_norm, *, seq, tm, tf):
    m, d = x2.shape
    dff = w_gate.shape[2]
    tpb = seq // tm
    n_steps = dff // tf
    x_chunks = 8
    assert x_chunks < n_steps and d % (x_chunks * GROUP_DIM) == 0
    emit_next = next_norm is not None
    row = pl.BlockSpec((tm, d), lambda i, f: (i, 0))
    mod = pl.BlockSpec((1, 1, d), lambda i, f: (i // tpb, 0, 0))
    in_specs = [row, pl.BlockSpec((tm, d // x_chunks), lambda i, f: (i, jnp.minimum(f, x_chunks - 1))), mod,
                pl.BlockSpec((None, d, tf), lambda i, f: (layer, 0, f)),
                pl.BlockSpec((None, d, tf), lambda i, f: (layer, 0, f)),
                pl.BlockSpec((None, tf, d), lambda i, f: (layer, f, 0))]
    args = [h, x2, gate, w_gate, w_up, w_down]
    out_specs = [row]
    out_shape = [jax.ShapeDtypeStruct((m, d), F32)]
    if emit_next:
        in_specs += [pl.BlockSpec((1, d), lambda i, f: (0, 0)), mod, mod]
        args += list(next_norm)
        out_specs.append(row)
        out_shape.append(jax.ShapeDtypeStruct((m, d), BF16))
    outs = pl.pallas_call(
        functools.partial(_swiglu_kernel, emit_next=emit_next, x_chunks=x_chunks),
        grid=(m // tm, n_steps),
        in_specs=in_specs,
        out_specs=out_specs,
        out_shape=out_shape,
        compiler_params=_cparams(("arbitrary", "arbitrary")),
        name="swiglu",
    )(*args)
    return (outs[0], outs[1]) if emit_next else (outs[0], None)


class _Tiles(NamedTuple):
    rows: int
    rows_f32: int
    pool_rows: int
    qkv_cols: int
    merge_cols: int
    ff_cols: int
    gla_chunks: int
    sb_heads: int


def _tile(n, pref):
    t = min(n, pref)
    assert n % t == 0, (n, t)
    return t


def _tiles(seq):
    return _Tiles(rows=_tile(seq, 1024), rows_f32=_tile(seq, 512), pool_rows=_tile(seq, 2048),
                  qkv_cols=1024, merge_cols=512, ff_cols=512,
                  gla_chunks=_tile(seq // GLA_CHUNK, 4), sb_heads=2)


def _mod_vectors(mod_l):
    d = D_MODEL
    return [mod_l[:, None, k * d:(k + 1) * d] for k in range(6)]


class _Weights(NamedTuple):
    f32cols: jax.Array
    sb: jax.Array
    gates: jax.Array
    br_pool: jax.Array
    br_sb: jax.Array
    br_gla: jax.Array
    out: jax.Array
    ff_gate: jax.Array
    ff_up: jax.Array
    ff_down: jax.Array


def _layer(x2, h1, mod_l, next_norm, batch, seq, layer, w, w_pool, pool_scale, sb_q_gain, sb_k_gain,
           gla_w_a2, gla_b_a2, gla_out_gain, g_norm2):
    d = D_MODEL
    t = _tiles(seq)
    _, _, ga1, sh2, sc2, ga2 = _mod_vectors(mod_l)

    proj = _proj(h1, w.f32cols, layer, jnp.ones((1, F32_COLS), F32), tm=t.rows_f32, tn=F32_COLS,
                 n_norm_tiles=0, out_dtype=F32)
    q_scale = LOG2E / math.sqrt(GROUP_DIM)
    colgain = jnp.concatenate([jnp.tile(sb_q_gain * q_scale, SB_HEADS), jnp.tile(sb_k_gain, SB_HEADS),
                               jnp.ones((SB_WIDTH,), F32)]).reshape(1, 3 * SB_WIDTH)
    assert (2 * SB_WIDTH) % t.qkv_cols == 0
    qkv = _proj(h1, w.sb, layer, colgain, tm=t.rows, tn=t.qkv_cols, n_norm_tiles=2 * SB_WIDTH // t.qkv_cols,
                out_dtype=BF16)

    proj3 = proj.reshape(batch, seq, F32_COLS)
    y_pool = _pool(proj3, w_pool.astype(BF16), pool_scale, ts=t.pool_rows)
    y_sb = _stick_breaking(qkv.reshape(batch, seq, 3 * SB_WIDTH), tq=t.rows, tk=GROUP_DIM,
                           heads_per_step=t.sb_heads)
    w_a2p = jnp.pad(gla_w_a2, ((0, GROUP_DIM - GLA_RANK), (0, 0))).astype(BF16)
    y_gla = _gla(proj3, w_a2p, gla_b_a2.reshape(1, GLA_WIDTH), gla_out_gain.reshape(1, GROUP_DIM),
                 chunks_per_step=t.gla_chunks)

    m = batch * seq
    merged = _merge(h1, y_pool.reshape(m, POOL_WIDTH), y_sb.reshape(m, SB_WIDTH), y_gla.reshape(m, GLA_WIDTH),
                    w.gates, w.br_pool, w.br_sb, w.br_gla, layer, tm=t.rows, tn=t.merge_cols)
    x2, h2 = _resproj(merged, w.out, layer, x2, ga1, g_norm2.reshape(1, d), sc2, sh2, seq=seq, tm=t.rows_f32)
    return _swiglu(h2, x2, ga2, w.ff_gate, w.ff_up, w.ff_down, layer, next_norm, seq=seq, tm=t.rows,
                   tf=t.ff_cols)


def kernel(x, c, w_ada, b_ada, g_norm1, w_in, w_pool, pool_scale, sb_q_gain, sb_k_gain, gla_w_a2, gla_b_a2, gla_out_gain, w_br_pool, w_br_sb, w_br_gla, w_out, g_norm2, w_ff_gate, w_ff_up, w_ff_down):
    batch, seq, d = x.shape
    depth = w_ada.shape[0]
    c_pad = jnp.pad(c, ((0, 8 - batch % 8 if batch % 8 else 0), (0, 0)))
    mod = _ada(c_pad, w_ada, b_ada)[:, :batch]
    x2 = x.reshape(batch * seq, d)
    w = _Weights(*_split_w_in(w_in.astype(BF16)), *[a.astype(BF16) for a in (w_br_pool, w_br_sb, w_br_gla, w_out,
                                                               w_ff_gate, w_ff_up, w_ff_down)])

    def first_norm(l):
        sh1, sc1 = _mod_vectors(mod[l])[:2]
        return g_norm1[l].reshape(1, d), sc1, sh1

    h1 = _norm(x2, *first_norm(0), seq=seq, tm=_tiles(seq).rows_f32)
    for l in range(depth):
        next_norm = first_norm(l + 1) if l + 1 < depth else None
        x2, h1 = _layer(x2, h1, mod[l], next_norm, batch, seq, l, w, w_pool[l], pool_scale[l],
                        sb_q_gain[l], sb_k_gain[l], gla_w_a2[l], gla_b_a2[l], gla_out_gain[l], g_norm2[l])
    return x2.reshape(batch, seq, d)
```

```python
import functools
import math
from typing import NamedTuple

import numpy as np
import jax
import jax.numpy as jnp
from jax import lax
from jax.experimental import pallas as pl
from jax.experimental.pallas import tpu as pltpu

F32 = jnp.float32
BF16 = jnp.bfloat16

D_MODEL = 2048
POOL_WINDOWS = (2, 4, 8, 16)
POOL_GROUPS = 4
GROUP_DIM = 128
POOL_WIDTH = POOL_GROUPS * GROUP_DIM
SB_HEADS = 8
SB_WIDTH = SB_HEADS * GROUP_DIM
GLA_HEADS = 4
GLA_WIDTH = GLA_HEADS * GROUP_DIM
GLA_RANK = 16
GLA_TAU = 16.0
GLA_CHUNK = 128
GLA_LEVELS = 7
N_BRANCH = 3
D_FF = 5632
RMS_EPS = 1e-6
LOG2E = 1.4426950408889634
POOL_HALO = 16
SB_DEAD_BITS = 160.0
SB_MASKED_BITS = 1.0e4
SB_STATIC_LAGS = 3

VMEM_LIMIT = 56 * 1024 * 1024
MXU_WIDTH = 256
ROW_CHUNK = 256


def _cparams(sem):
    return pltpu.CompilerParams(dimension_semantics=sem, vmem_limit_bytes=VMEM_LIMIT)


def _nt_dot(a, b):
    return lax.dot_general(a, b, (((1,), (1,)), ((), ())), preferred_element_type=F32)


def _silu(x):
    return x * jax.nn.sigmoid(x)


def _ada_kernel(c_ref, w_ref, b_ref, o_ref):
    c = c_ref[...]
    a = _silu(c).astype(BF16)
    o_ref[0] = jnp.dot(a, w_ref[0].astype(BF16), preferred_element_type=F32) + b_ref[0]


def _ada(c_pad, w_ada, b_ada, tn=1024):
    depth, d, n = w_ada.shape
    rows = c_pad.shape[0]
    return pl.pallas_call(
        _ada_kernel,
        grid=(depth, n // tn),
        in_specs=[
            pl.BlockSpec((rows, d), lambda l, j: (0, 0)),
            pl.BlockSpec((1, d, tn), lambda l, j: (l, 0, j)),
            pl.BlockSpec((1, 1, tn), lambda l, j: (l, 0, j)),
        ],
        out_specs=pl.BlockSpec((1, rows, tn), lambda l, j: (l, 0, j)),
        out_shape=jax.ShapeDtypeStruct((depth, rows, n), F32),
        compiler_params=_cparams(("arbitrary", "arbitrary")),
        name="ada_modulation",
    )(c_pad, w_ada, b_ada.reshape(depth, 1, n))


W_IN_SB = POOL_WIDTH
W_IN_GLA = W_IN_SB + 3 * SB_WIDTH
W_IN_A = W_IN_GLA + 4 * GLA_WIDTH
W_IN_GATE = W_IN_A + GLA_RANK
W_IN_COLS = W_IN_GATE + N_BRANCH * D_MODEL
A_LOW_COL = POOL_WIDTH + 4 * GLA_WIDTH
F32_COLS = A_LOW_COL + MXU_WIDTH


def _split_w_in_kernel(w_ref, f_ref, sb_ref, g_ref):
    f_ref[0, :, :POOL_WIDTH] = w_ref[0, :, :W_IN_SB]
    f_ref[0, :, POOL_WIDTH:A_LOW_COL] = w_ref[0, :, W_IN_GLA:W_IN_A]
    f_ref[0, :, A_LOW_COL:] = jnp.zeros((f_ref.shape[1], MXU_WIDTH), BF16)
    f_ref[0, :, A_LOW_COL:A_LOW_COL + GLA_RANK] = w_ref[0, :, W_IN_A:W_IN_GATE]
    sb_ref[0] = w_ref[0, :, W_IN_SB:W_IN_GLA]
    g_ref[0] = w_ref[0, :, W_IN_GATE:]


def _split_w_in(w_in, tr=512):
    depth, d, n = w_in.shape
    assert n == W_IN_COLS and w_in.dtype == BF16
    widths = (F32_COLS, 3 * SB_WIDTH, N_BRANCH * D_MODEL)
    return pl.pallas_call(
        _split_w_in_kernel,
        grid=(depth, d // tr),
        in_specs=[pl.BlockSpec((1, tr, n), lambda l, i: (l, i, 0))],
        out_specs=[pl.BlockSpec((1, tr, w), lambda l, i: (l, i, 0)) for w in widths],
        out_shape=[jax.ShapeDtypeStruct((depth, d, w), BF16) for w in widths],
        compiler_params=_cparams(("arbitrary", "arbitrary")),
        name="split_w_in",
    )(w_in)


def _modulated_norm(x, gain, scale, shift):
    ms = jnp.mean(x * x, axis=-1, keepdims=True)
    y = x * lax.rsqrt(ms + RMS_EPS) * gain
    return y * (1.0 + scale) + shift


def _group_rmsnorm(acc, colgain):
    out = []
    for c in range(acc.shape[1] // GROUP_DIM):
        sl = slice(c * GROUP_DIM, (c + 1) * GROUP_DIM)
        blk = acc[:, sl]
        ms = jnp.mean(blk * blk, axis=-1, keepdims=True)
        out.append(blk * lax.rsqrt(ms + RMS_EPS) * colgain[:, sl])
    return jnp.concatenate(out, axis=1)


def _norm_kernel(x_ref, g_ref, sc_ref, sh_ref, h_ref):
    h_ref[...] = _modulated_norm(x_ref[...], g_ref[...], sc_ref[0], sh_ref[0]).astype(BF16)


def _norm(x2, gain, scale, shift, *, seq, tm):
    m, d = x2.shape
    tpb = seq // tm
    mod = pl.BlockSpec((1, 1, d), lambda i: (i // tpb, 0, 0))
    return pl.pallas_call(
        _norm_kernel,
        grid=(m // tm,),
        in_specs=[pl.BlockSpec((tm, d), lambda i: (i, 0)), pl.BlockSpec((1, d), lambda i: (0, 0)), mod, mod],
        out_specs=pl.BlockSpec((tm, d), lambda i: (i, 0)),
        out_shape=jax.ShapeDtypeStruct((m, d), BF16),
        compiler_params=_cparams(("arbitrary",)),
        name="modulated_norm",
    )(x2, gain, scale, shift)


def _proj_kernel(h_ref, w_ref, cg_ref, o_ref, *, n_norm_tiles, tn):
    j = pl.program_id(1)
    h = h_ref[...]

    def tile(normed):
        for c in range(tn // MXU_WIDTH):
            cols = slice(c * MXU_WIDTH, (c + 1) * MXU_WIDTH)
            acc = jnp.dot(h, w_ref[:, cols], preferred_element_type=F32)
            if normed:
                acc = _group_rmsnorm(acc, cg_ref[:, cols])
            o_ref[:, cols] = acc.astype(o_ref.dtype)

    if n_norm_tiles == 0:
        tile(False)
    else:
        pl.when(j < n_norm_tiles)(lambda: tile(True))
        pl.when(j >= n_norm_tiles)(lambda: tile(False))


def _proj(h, w, layer, colgain, *, tm, tn, n_norm_tiles, out_dtype):
    m, d = h.shape
    n = w.shape[2]
    kern = functools.partial(_proj_kernel, n_norm_tiles=n_norm_tiles, tn=tn)
    return pl.pallas_call(
        kern,
        grid=(m // tm, n // tn),
        in_specs=[
            pl.BlockSpec((tm, d), lambda i, j: (i, 0)),
            pl.BlockSpec((None, d, tn), lambda i, j: (layer, 0, j)),
            pl.BlockSpec((1, tn), lambda i, j: (0, j)),
        ],
        out_specs=pl.BlockSpec((tm, tn), lambda i, j: (i, j)),
        out_shape=jax.ShapeDtypeStruct((m, n), out_dtype),
        compiler_params=_cparams(("arbitrary", "arbitrary")),
        name="in_proj",
    )(h, w, colgain)


def _pool_kernel(u_ref, w_ref, ps_ref, o_ref, ext_ref, *, ts):
    i = pl.program_id(1)

    @pl.when(i == 0)
    def _():
        ext_ref[0:POOL_HALO, :] = jnp.zeros((POOL_HALO, POOL_WIDTH), F32)

    @pl.when(i > 0)
    def _():
        ext_ref[0:POOL_HALO, :] = ext_ref[ts:ts + POOL_HALO, :]

    ext_ref[POOL_HALO:, :] = u_ref[0]
    pos1 = (i * ts + 1 + lax.broadcasted_iota(jnp.int32, (ts, GROUP_DIM), 0)).astype(F32)
    for g, w in enumerate(POOL_WINDOWS):
        cols = slice(g * GROUP_DIM, (g + 1) * GROUP_DIM)
        u = ext_ref[POOL_HALO:, cols]
        win = u
        for k in range(1, w):
            win = win + ext_ref[POOL_HALO - k:POOL_HALO - k + ts, cols]
        pooled = win / jnp.minimum(pos1, float(w)) - u
        y = jnp.dot(pooled.astype(BF16), w_ref[g], preferred_element_type=F32) * ps_ref[g]
        o_ref[0, :, cols] = y.astype(o_ref.dtype)


def _pool(proj3, w_pool, pool_scale, ts):
    b, s, _ = proj3.shape
    kern = functools.partial(_pool_kernel, ts=ts)
    return pl.pallas_call(
        kern,
        grid=(b, s // ts),
        in_specs=[
            pl.BlockSpec((1, ts, POOL_WIDTH), lambda bi, i: (bi, i, 0)),
            pl.BlockSpec((POOL_GROUPS, GROUP_DIM, GROUP_DIM), lambda bi, i: (0, 0, 0)),
            pl.BlockSpec((POOL_GROUPS, 1, GROUP_DIM), lambda bi, i: (0, 0, 0)),
        ],
        out_specs=pl.BlockSpec((1, ts, POOL_WIDTH), lambda bi, i: (bi, i, 0)),
        out_shape=jax.ShapeDtypeStruct((b, s, POOL_WIDTH), BF16),
        scratch_shapes=[pltpu.VMEM((ts + POOL_HALO, POOL_WIDTH), F32)],
        compiler_params=_cparams(("arbitrary", "arbitrary")),
        name="pool_mixer",
    )(proj3, w_pool, pool_scale.reshape(POOL_GROUPS, 1, GROUP_DIM))


def _sb_cumsum_matrix(tk):
    j = np.arange(tk)[:, None]
    s = np.arange(tk)[None, :]
    one = np.concatenate([(j >= s).astype(np.float32), np.ones((tk, tk), np.float32)], axis=1)
    return jnp.asarray(np.concatenate([one, one], axis=0), dtype=BF16)


def _sb_kernel(q_ref, k_ref, v_ref, w2_ref, o_ref, acc_ref, run_ref, *, tq, tk):
    i = pl.program_id(2)
    w2 = w2_ref[...]
    n_sub = tq // tk
    heads = [slice(hh * GROUP_DIM, (hh + 1) * GROUP_DIM) for hh in range(q_ref.shape[2] // GROUP_DIM)]
    acc_ref[...] = jnp.zeros(acc_ref.shape, F32)
    run_ref[...] = jnp.zeros(run_ref.shape, F32)
    row = lax.broadcasted_iota(jnp.int32, (tq, tk), 0)
    col = lax.broadcasted_iota(jnp.int32, (tq, tk), 1)
    diag_mask = col < (row & (tk - 1))
    sub = lambda r: slice(r * tk, (r + 1) * tk)

    def key_blocks(lag):
        blocks = []
        for r in range(n_sub):
            jb = i * n_sub + r - lag
            start = pl.multiple_of(jnp.maximum(jb, 0) * tk, tk)
            blocks.append((start, jnp.where(jb >= 0, 0.0, SB_MASKED_BITS)))
        return blocks

    def scores(cols, blocks, diagonal):
        z = jnp.concatenate([_nt_dot(q_ref[0, sub(r), cols], k_ref[0, pl.ds(start, tk), cols])
                             for r, (start, _) in enumerate(blocks)], axis=0)
        neg_abs = lax.bitcast_convert_type(
            lax.bitcast_convert_type(z, jnp.uint32) | jnp.uint32(0x80000000), F32)
        neg_log_keep = jnp.maximum(z, 0.0) + jnp.log(1.0 + jnp.exp2(neg_abs)) * LOG2E
        if diagonal:
            neg_log_keep = jnp.where(diag_mask, neg_log_keep, 0.0)
            z = jnp.where(diag_mask, z, -SB_MASKED_BITS)
        hi = lax.bitcast_convert_type(
            lax.bitcast_convert_type(neg_log_keep, jnp.uint32) & jnp.uint32(0xFFFF0000), F32)
        lo = neg_log_keep - hi
        c = jnp.dot(jnp.concatenate([hi.astype(BF16), lo.astype(BF16)], axis=1), w2, preferred_element_type=F32)
        return z, c

    def accumulate(hh, cols, blocks, z, c, diagonal):
        run = run_ref[hh]
        if not diagonal:
            run = run + jnp.concatenate([jnp.full((tk, tk), pen, F32) for _, pen in blocks], axis=0)
        a = jnp.exp2(z - c[:, :tk] - run).astype(BF16)
        for r, (start, _) in enumerate(blocks):
            acc_ref[hh, sub(r), :] += jnp.dot(a[sub(r), :], v_ref[0, pl.ds(start, tk), cols],
                                              preferred_element_type=F32)
        run_ref[hh] = run + c[:, tk:]

    lags = [key_blocks(lag) for lag in range(SB_STATIC_LAGS)]
    staged = [[scores(cols, blocks, lag == 0) for lag, blocks in enumerate(lags)] for cols in heads]
    for lag, blocks in enumerate(lags):
        for hh, cols in enumerate(heads):
            accumulate(hh, cols, blocks, *staged[hh][lag], lag == 0)

    for hh, cols in enumerate(heads):
        def cond(carry):
            lag, min_run = carry
            return jnp.logical_and(lag <= i * n_sub + n_sub - 1, min_run < SB_DEAD_BITS)

        def body(carry, hh=hh, cols=cols):
            lag, _ = carry
            blocks = key_blocks(lag)
            z, c = scores(cols, blocks, False)
            accumulate(hh, cols, blocks, z, c, False)
            return lag + 1, jnp.min(run_ref[hh])

        lax.while_loop(cond, body, (jnp.int32(SB_STATIC_LAGS), jnp.min(run_ref[hh])))
        o_ref[0, :, cols] = acc_ref[hh].astype(o_ref.dtype)


def _stick_breaking(qkv, tq, tk, heads_per_step):
    b, s, _ = qkv.shape
    hw = heads_per_step * GROUP_DIM
    groups = SB_HEADS // heads_per_step
    kern = functools.partial(_sb_kernel, tq=tq, tk=tk)
    return pl.pallas_call(
        kern,
        grid=(b, groups, s // tq),
        in_specs=[
            pl.BlockSpec((1, tq, hw), lambda bi, h, i: (bi, i, h)),
            pl.BlockSpec((1, s, hw), lambda bi, h, i: (bi, 0, groups + h)),
            pl.BlockSpec((1, s, hw), lambda bi, h, i: (bi, 0, 2 * groups + h)),
            pl.BlockSpec((2 * tk, 2 * tk), lambda bi, h, i: (0, 0)),
        ],
        out_specs=pl.BlockSpec((1, tq, hw), lambda bi, h, i: (bi, i, h)),
        out_shape=jax.ShapeDtypeStruct((b, s, SB_WIDTH), BF16),
        scratch_shapes=[pltpu.VMEM((heads_per_step, tq, GROUP_DIM), F32),
                        pltpu.VMEM((heads_per_step, tq, tk), F32)],
        compiler_params=_cparams(("arbitrary", "arbitrary", "arbitrary")),
        name="stick_breaking",
    )(qkv, qkv, qkv, _sb_cumsum_matrix(tk))


def _gla_constants():
    c = GLA_CHUNK
    t = np.arange(c)[:, None]
    j = np.arange(c)[None, :]
    dst = np.concatenate([j <= t, np.ones((8, c), bool)], axis=0).astype(np.float32)
    s = j
    masks = []
    for l in range(GLA_LEVELS):
        masks.append(((t ^ s) >> l == 1) & (t > s))
    masks.append(t == s)
    return jnp.asarray(dst, dtype=BF16), jnp.asarray(np.stack(masks).astype(np.float32))


def _split3(x):
    x1 = x.astype(BF16)
    r = x - x1.astype(F32)
    x2 = r.astype(BF16)
    x3 = (r - x2.astype(F32)).astype(BF16)
    return x1, x2, x3


def _gla_kernel(q_ref, k_ref, v_ref, r_ref, a_ref, wa_ref, ba_ref, og_ref, dst_ref, msk_ref,
                o_ref, st_ref):
    @pl.when(pl.program_id(1) == 0)
    def _():
        st_ref[...] = jnp.zeros(st_ref.shape, F32)

    for cc in range(q_ref.shape[1] // GLA_CHUNK):
        _gla_chunk(slice(cc * GLA_CHUNK, (cc + 1) * GLA_CHUNK), q_ref, k_ref, v_ref, r_ref, a_ref,
                   wa_ref, ba_ref, og_ref, dst_ref, msk_ref, o_ref, st_ref)


def _gla_chunk(rows, q_ref, k_ref, v_ref, r_ref, a_ref, wa_ref, ba_ref, og_ref, dst_ref, msk_ref,
               o_ref, st_ref):
    c = GLA_CHUNK
    x = jnp.dot(a_ref[0, rows, :].astype(BF16), wa_ref[...], preferred_element_type=F32) + ba_ref[...]
    log_sig = jnp.minimum(x, 0.0) - jnp.log(1.0 + jnp.exp(-jnp.abs(x)))
    g = log_sig * (LOG2E / GLA_TAU)
    dst = dst_ref[...]
    g1, g2, g3 = _split3(g)
    sums = (jnp.dot(dst, g1, preferred_element_type=F32)
            + jnp.dot(dst, g2, preferred_element_type=F32)
            + jnp.dot(dst, g3, preferred_element_type=F32))
    prefix = sums[:c]
    total = sums[c:c + 1]
    e_in = prefix
    e_out = total - prefix

    row = lax.broadcasted_iota(jnp.int32, g.shape, 0)
    pos = row & 3
    g_prev = pltpu.roll(g, 1, 0)
    g_next = pltpu.roll(g, c - 1, 0)
    e_lvl = [jnp.where((row & 1) == 1, g, 0.0),
             jnp.where(pos == 0, g_next, jnp.where(pos == 1, 0.0, jnp.where(pos == 2, g, g_prev + g)))]
    for l in range(2, GLA_LEVELS):
        m = 1 << l
        blocks = prefix.reshape(c // (2 * m), 2 * m, GLA_WIDTH)
        e_lvl.append((-jnp.abs(blocks - blocks[:, m - 1:m, :])).reshape(c, GLA_WIDTH))

    for h in range(GLA_HEADS):
        cols = slice(h * GROUP_DIM, (h + 1) * GROUP_DIM)
        q = q_ref[0, rows, cols] * (GROUP_DIM ** -0.5)
        k = k_ref[0, rows, cols]
        v = v_ref[0, rows, cols].astype(BF16)
        scores = msk_ref[GLA_LEVELS] * _nt_dot(q.astype(BF16), k.astype(BF16))
        for l in range(GLA_LEVELS):
            xl = jnp.exp2(e_lvl[l][:, cols])
            scores = scores + msk_ref[l] * _nt_dot((q * xl).astype(BF16), (k * xl).astype(BF16))
        o = jnp.dot(scores.astype(BF16), v, preferred_element_type=F32)
        x_in = jnp.exp2(e_in[:, cols])
        st = st_ref[h]
        o = o + _nt_dot((q * x_in).astype(BF16), st.astype(BF16))
        x_out = jnp.exp2(e_out[:, cols])
        x_all = jnp.exp2(total[:, cols])
        vt = v_ref[0, rows, cols].T.astype(BF16)
        st_ref[h] = st * x_all + jnp.dot(vt, (k * x_out).astype(BF16), preferred_element_type=F32)
        ms = jnp.mean(o * o, axis=-1, keepdims=True)
        o = o * lax.rsqrt(ms + RMS_EPS) * og_ref[...]
        o_ref[0, rows, cols] = (o * _silu(r_ref[0, rows, cols])).astype(o_ref.dtype)


def _gla(proj3, w_a2p, b_a2, out_gain, chunks_per_step):
    b, s, _ = proj3.shape
    c = GLA_CHUNK * chunks_per_step
    dst, masks = _gla_constants()
    wide = lambda blk: pl.BlockSpec((1, c, GLA_WIDTH), lambda bi, ci: (bi, ci, blk))
    const2 = lambda shape: pl.BlockSpec(shape, lambda bi, ci: (0, 0))
    a_blk = (POOL_WIDTH + 4 * GLA_WIDTH) // GROUP_DIM
    return pl.pallas_call(
        _gla_kernel,
        grid=(b, s // c),
        in_specs=[
            wide(1), wide(2), wide(3), wide(4),
            pl.BlockSpec((1, c, GROUP_DIM), lambda bi, ci: (bi, ci, a_blk)),
            const2((GROUP_DIM, GLA_WIDTH)),
            const2((1, GLA_WIDTH)),
            const2((1, GROUP_DIM)),
            const2(dst.shape),
            pl.BlockSpec(masks.shape, lambda bi, ci: (0, 0, 0)),
        ],
        out_specs=pl.BlockSpec((1, c, GLA_WIDTH), lambda bi, ci: (bi, ci, 0)),
        out_shape=jax.ShapeDtypeStruct((b, s, GLA_WIDTH), BF16),
        scratch_shapes=[pltpu.VMEM((GLA_HEADS, GROUP_DIM, GROUP_DIM), F32)],
        compiler_params=_cparams(("arbitrary", "arbitrary")),
        name="gla",
    )(proj3, proj3, proj3, proj3, proj3, w_a2p, b_a2, out_gain, dst, masks)


def _merge_kernel(h_ref, yp_ref, ys_ref, yg_ref, wgp_ref, wgs_ref, wgg_ref, wp_ref, ws_ref, wg_ref, o_ref):
    h = h_ref[...]

    def branch(wgate_ref, y_ref, w_ref):
        gate = jax.nn.sigmoid(jnp.dot(h, wgate_ref[...], preferred_element_type=F32))
        return gate * jnp.dot(y_ref[...], w_ref[...], preferred_element_type=F32)

    merged = branch(wgp_ref, yp_ref, wp_ref) + branch(wgs_ref, ys_ref, ws_ref) + branch(wgg_ref, yg_ref, wg_ref)
    o_ref[...] = merged.astype(o_ref.dtype)


def _merge(h, y_pool, y_sb, y_gla, w_gates, w_br_pool, w_br_sb, w_br_gla, layer, *, tm, tn):
    m, d = h.shape
    nj = d // tn
    row = lambda width: pl.BlockSpec((tm, width), lambda i, j: (i, 0))
    gate_w = lambda br: pl.BlockSpec((None, d, tn), lambda i, j: (layer, 0, br * nj + j))
    br_w = lambda width: pl.BlockSpec((None, width, tn), lambda i, j: (layer, 0, j))
    return pl.pallas_call(
        _merge_kernel,
        grid=(m // tm, nj),
        in_specs=[
            row(d), row(POOL_WIDTH), row(SB_WIDTH), row(GLA_WIDTH),
            gate_w(0), gate_w(1), gate_w(2),
            br_w(POOL_WIDTH), br_w(SB_WIDTH), br_w(GLA_WIDTH),
        ],
        out_specs=pl.BlockSpec((tm, tn), lambda i, j: (i, j)),
        out_shape=jax.ShapeDtypeStruct((m, d), BF16),
        compiler_params=_cparams(("arbitrary", "arbitrary")),
        name="gated_merge",
    )(h, y_pool, y_sb, y_gla, w_gates, w_gates, w_gates, w_br_pool, w_br_sb, w_br_gla)


def _resproj_kernel(a_ref, w_ref, x_ref, ga_ref, g_ref, sc_ref, sh_ref, o_ref, h_ref):
    for r in range(a_ref.shape[0] // ROW_CHUNK):
        rows = slice(r * ROW_CHUNK, (r + 1) * ROW_CHUNK)
        y = jnp.dot(a_ref[rows, :], w_ref[...], preferred_element_type=F32)
        x_new = x_ref[rows, :] + ga_ref[0] * y
        o_ref[rows, :] = x_new
        h_ref[rows, :] = _modulated_norm(x_new, g_ref[...], sc_ref[0], sh_ref[0]).astype(BF16)


def _resproj(a, w, layer, x2, gate, gain, scale, shift, *, seq, tm):
    m, kdim = a.shape
    d = w.shape[2]
    tpb = seq // tm
    row = lambda width: pl.BlockSpec((tm, width), lambda i: (i, 0))
    mod = pl.BlockSpec((1, 1, d), lambda i: (i // tpb, 0, 0))
    return pl.pallas_call(
        _resproj_kernel,
        grid=(m // tm,),
        in_specs=[row(kdim), pl.BlockSpec((None, kdim, d), lambda i: (layer, 0, 0)), row(d), mod,
                  pl.BlockSpec((1, d), lambda i: (0, 0)), mod, mod],
        out_specs=[row(d), row(d)],
        out_shape=[jax.ShapeDtypeStruct((m, d), F32), jax.ShapeDtypeStruct((m, d), BF16)],
        compiler_params=_cparams(("arbitrary",)),
        name="residual_proj",
    )(a, w, x2, gate, gain, scale, shift)


def _swiglu_kernel(*refs, emit_next, x_chunks):
    if emit_next:
        h_ref, x_ref, ga_ref, wg_ref, wu_ref, wd_ref, g_ref, sc_ref, sh_ref, o_ref, hn_ref = refs
    else:
        h_ref, x_ref, ga_ref, wg_ref, wu_ref, wd_ref, o_ref = refs
    f = pl.program_id(1)
    last = pl.num_programs(1) - 1
    xw = x_ref.shape[1]

    def gated_partial(rows):
        h = h_ref[rows, :]
        gate = jnp.dot(h, wg_ref[...], preferred_element_type=F32)
        up = jnp.dot(h, wu_ref[...], preferred_element_type=F32)
        act = (_silu(gate) * up).astype(BF16)
        return ga_ref[0] * jnp.dot(act, wd_ref[...], preferred_element_type=F32)

    row_chunks = [slice(r * ROW_CHUNK, (r + 1) * ROW_CHUNK) for r in range(h_ref.shape[0] // ROW_CHUNK)]

    @pl.when(f == 0)
    def _():
        for rows in row_chunks:
            o_ref[rows, :] = gated_partial(rows)

    @pl.when(jnp.logical_and(f > 0, f < last))
    def _():
        for rows in row_chunks:
            o_ref[rows, :] += gated_partial(rows)

    for c in range(x_chunks):
        @pl.when(f == c)
        def _():
            o_ref[:, c * xw:(c + 1) * xw] += x_ref[...]

    @pl.when(f == last)
    def _():
        for rows in row_chunks:
            x_new = o_ref[rows, :] + gated_partial(rows)
            o_ref[rows, :] = x_new
            if emit_next:
                hn_ref[rows, :] = _modulated_norm(x_new, g_ref[...], sc_ref[0], sh_ref[0]).astype(BF16)


def _swiglu(h, x2, gate, w_gate, w_up, w_down, layer, next_norm, *, seq, tm, tf):
    m, d = x2.shape
    tpb = seq // tm
    n_steps = w_gate.shape[1]
    assert w_gate.shape[2:] == (d, tf) and w_down.shape[1] == n_steps * tf
    x_chunks = 8
    assert x_chunks < n_steps and d % (x_chunks * GROUP_DIM) == 0
    emit_next = next_norm is not None
    row = pl.BlockSpec((tm, d), lambda i, f: (i, 0))
    mod = pl.BlockSpec((1, 1, d), lambda i, f: (i // tpb, 0, 0))
    in_specs = [row, pl.BlockSpec((tm, d // x_chunks), lambda i, f: (i, jnp.minimum(f, x_chunks - 1))), mod,
                pl.BlockSpec((None, None, d, tf), lambda i, f: (layer, f, 0, 0)),
                pl.BlockSpec((None, None, d, tf), lambda i, f: (layer, f, 0, 0)),
                pl.BlockSpec((None, tf, d), lambda i, f: (layer, f, 0))]
    args = [h, x2, gate, w_gate, w_up, w_down]
    out_specs = [row]
    out_shape = [jax.ShapeDtypeStruct((m, d), F32)]
    if emit_next:
        in_specs += [pl.BlockSpec((1, d), lambda i, f: (0, 0)), mod, mod]
        args += list(next_norm)
        out_specs.append(row)
        out_shape.append(jax.ShapeDtypeStruct((m, d), BF16))
    outs = pl.pallas_call(
        functools.partial(_swiglu_kernel, emit_next=emit_next, x_chunks=x_chunks),
        grid=(m // tm, n_steps),
        in_specs=in_specs,
        out_specs=out_specs,
        out_shape=out_shape,
        compiler_params=_cparams(("arbitrary", "arbitrary")),
        name="swiglu",
    )(*args)
    return (outs[0], outs[1]) if emit_next else (outs[0], None)


class _Tiles(NamedTuple):
    rows: int
    rows_f32: int
    pool_rows: int
    qkv_cols: int
    merge_cols: int
    ff_cols: int
    gla_chunks: int
    sb_heads: int


def _tile(n, pref):
    t = min(n, pref)
    assert n % t == 0, (n, t)
    return t


def _tiles(seq):
    return _Tiles(rows=_tile(seq, 1024), rows_f32=_tile(seq, 512), pool_rows=_tile(seq, 2048),
                  qkv_cols=1024, merge_cols=512, ff_cols=512,
                  gla_chunks=_tile(seq // GLA_CHUNK, 4), sb_heads=2)


def _mod_vectors(mod_l):
    d = D_MODEL
    return [mod_l[:, None, k * d:(k + 1) * d] for k in range(6)]


class _Weights(NamedTuple):
    f32cols: jax.Array
    sb: jax.Array
    gates: jax.Array
    br_pool: jax.Array
    br_sb: jax.Array
    br_gla: jax.Array
    out: jax.Array
    ff_gate: jax.Array
    ff_up: jax.Array
    ff_down: jax.Array


def _layer(x2, h1, mod_l, next_norm, batch, seq, layer, w, w_pool, pool_scale, sb_q_gain, sb_k_gain,
           gla_w_a2, gla_b_a2, gla_out_gain, g_norm2):
    d = D_MODEL
    t = _tiles(seq)
    _, _, ga1, sh2, sc2, ga2 = _mod_vectors(mod_l)

    proj = _proj(h1, w.f32cols, layer, jnp.ones((1, F32_COLS), F32), tm=t.rows_f32, tn=F32_COLS,
                 n_norm_tiles=0, out_dtype=F32)
    q_scale = LOG2E / math.sqrt(GROUP_DIM)
    colgain = jnp.concatenate([jnp.tile(sb_q_gain * q_scale, SB_HEADS), jnp.tile(sb_k_gain, SB_HEADS),
                               jnp.ones((SB_WIDTH,), F32)]).reshape(1, 3 * SB_WIDTH)
    assert (2 * SB_WIDTH) % t.qkv_cols == 0
    qkv = _proj(h1, w.sb, layer, colgain, tm=t.rows, tn=t.qkv_cols, n_norm_tiles=2 * SB_WIDTH // t.qkv_cols,
                out_dtype=BF16)

    proj3 = proj.reshape(batch, seq, F32_COLS)
    y_pool = _pool(proj3, w_pool.astype(BF16), pool_scale, ts=t.pool_rows)
    y_sb = _stick_breaking(qkv.reshape(batch, seq, 3 * SB_WIDTH), tq=t.rows, tk=GROUP_DIM,
                           heads_per_step=t.sb_heads)
    w_a2p = jnp.pad(gla_w_a2, ((0, GROUP_DIM - GLA_RANK), (0, 0))).astype(BF16)
    y_gla = _gla(proj3, w_a2p, gla_b_a2.reshape(1, GLA_WIDTH), gla_out_gain.reshape(1, GROUP_DIM),
                 chunks_per_step=t.gla_chunks)

    m = batch * seq
    merged = _merge(h1, y_pool.reshape(m, POOL_WIDTH), y_sb.reshape(m, SB_WIDTH), y_gla.reshape(m, GLA_WIDTH),
                    w.gates, w.br_pool, w.br_sb, w.br_gla, layer, tm=t.rows, tn=t.merge_cols)
    x2, h2 = _resproj(merged, w.out, layer, x2, ga1, g_norm2.reshape(1, d), sc2, sh2, seq=seq, tm=t.rows_f32)
    return _swiglu(h2, x2, ga2, w.ff_gate, w.ff_up, w.ff_down, layer, next_norm, seq=seq, tm=t.rows,
                   tf=t.ff_cols)


def kernel(x, c, w_ada, b_ada, g_norm1, w_in, w_pool, pool_scale, sb_q_gain, sb_k_gain, gla_w_a2, gla_b_a2, gla_out_gain, w_br_pool, w_br_sb, w_br_gla, w_out, g_norm2, w_ff_gate, w_ff_up, w_ff_down):
    batch, seq, d = x.shape
    depth = w_ada.shape[0]
    c_pad = jnp.pad(c, ((0, 8 - batch % 8 if batch % 8 else 0), (0, 0)))
    mod = _ada(c_pad, w_ada, b_ada)[:, :batch]
    x2 = x.reshape(batch * seq, d)
    tf = _tiles(seq).ff_cols

    def col_tiles(a):
        return a.astype(BF16).reshape(depth, d, a.shape[2] // tf, tf).transpose(0, 2, 1, 3)

    w = _Weights(*_split_w_in(w_in.astype(BF16)), *[a.astype(BF16) for a in (w_br_pool, w_br_sb, w_br_gla, w_out)],
                 col_tiles(w_ff_gate), col_tiles(w_ff_up), w_ff_down.astype(BF16))

    def first_norm(l):
        sh1, sc1 = _mod_vectors(mod[l])[:2]
        return g_norm1[l].reshape(1, d), sc1, sh1

    h1 = _norm(x2, *first_norm(0), seq=seq, tm=_tiles(seq).rows_f32)
    for l in range(depth):
        next_norm = first_norm(l + 1) if l + 1 < depth else None
        x2, h1 = _layer(x2, h1, mod[l], next_norm, batch, seq, l, w, w_pool[l], pool_scale[l],
                        sb_q_gain[l], sb_k_gain[l], gla_w_a2[l], gla_b_a2[l], gla_out_gain[l], g_norm2[l])
    return x2.reshape(batch, seq, d)
```

```python
import functools
import math
from typing import NamedTuple

import numpy as np
import jax
import jax.numpy as jnp
from jax import lax
from jax.experimental import pallas as pl
from jax.experimental.pallas import tpu as pltpu

F32 = jnp.float32
BF16 = jnp.bfloat16

D_MODEL = 2048
POOL_WINDOWS = (2, 4, 8, 16)
POOL_GROUPS = 4
GROUP_DIM = 128
POOL_WIDTH = POOL_GROUPS * GROUP_DIM
SB_HEADS = 8
SB_WIDTH = SB_HEADS * GROUP_DIM
GLA_HEADS = 4
GLA_WIDTH = GLA_HEADS * GROUP_DIM
GLA_RANK = 16
GLA_TAU = 16.0
GLA_CHUNK = 128
GLA_LEVELS = GLA_CHUNK.bit_length() - 1
assert GLA_CHUNK == 1 << GLA_LEVELS
N_BRANCH = 3
RMS_EPS = 1e-6
LOG2E = 1.4426950408889634
POOL_HALO = 16
SB_DEAD_BITS = 160.0
SB_MASKED_BITS = 1.0e4
SB_STATIC_LAGS = 3

VMEM_LIMIT = 56 * 1024 * 1024
MXU_WIDTH = 256
ROW_CHUNK = 256


def _cparams(sem):
    return pltpu.CompilerParams(dimension_semantics=sem, vmem_limit_bytes=VMEM_LIMIT)


def _nt_dot(a, b):
    return lax.dot_general(a, b, (((1,), (1,)), ((), ())), preferred_element_type=F32)


def _silu(x):
    return x * jax.nn.sigmoid(x)


def _ada_kernel(c_ref, w_ref, b_ref, o_ref):
    c = c_ref[...]
    a = _silu(c).astype(BF16)
    o_ref[0] = jnp.dot(a, w_ref[0].astype(BF16), preferred_element_type=F32) + b_ref[0]


def _ada(c_pad, w_ada, b_ada, tn=1024):
    depth, d, n = w_ada.shape
    rows = c_pad.shape[0]
    return pl.pallas_call(
        _ada_kernel,
        grid=(depth, n // tn),
        in_specs=[
            pl.BlockSpec((rows, d), lambda l, j: (0, 0)),
            pl.BlockSpec((1, d, tn), lambda l, j: (l, 0, j)),
            pl.BlockSpec((1, 1, tn), lambda l, j: (l, 0, j)),
        ],
        out_specs=pl.BlockSpec((1, rows, tn), lambda l, j: (l, 0, j)),
        out_shape=jax.ShapeDtypeStruct((depth, rows, n), F32),
        compiler_params=_cparams(("arbitrary", "arbitrary")),
        name="ada_modulation",
    )(c_pad, w_ada, b_ada.reshape(depth, 1, n))


W_IN_SB = POOL_WIDTH
W_IN_GLA = W_IN_SB + 3 * SB_WIDTH
W_IN_A = W_IN_GLA + 4 * GLA_WIDTH
W_IN_GATE = W_IN_A + GLA_RANK
W_IN_COLS = W_IN_GATE + N_BRANCH * D_MODEL
A_LOW_COL = POOL_WIDTH + 4 * GLA_WIDTH
F32_COLS = A_LOW_COL + MXU_WIDTH


def _split_w_in_kernel(w_ref, f_ref, sb_ref, g_ref):
    f_ref[0, :, :POOL_WIDTH] = w_ref[0, :, :W_IN_SB]
    f_ref[0, :, POOL_WIDTH:A_LOW_COL] = w_ref[0, :, W_IN_GLA:W_IN_A]
    f_ref[0, :, A_LOW_COL:] = jnp.zeros((f_ref.shape[1], MXU_WIDTH), BF16)
    f_ref[0, :, A_LOW_COL:A_LOW_COL + GLA_RANK] = w_ref[0, :, W_IN_A:W_IN_GATE]
    sb_ref[0] = w_ref[0, :, W_IN_SB:W_IN_GLA]
    g_ref[0] = w_ref[0, :, W_IN_GATE:]


def _split_w_in(w_in, tr=512):
    depth, d, n = w_in.shape
    assert n == W_IN_COLS and w_in.dtype == BF16
    widths = (F32_COLS, 3 * SB_WIDTH, N_BRANCH * D_MODEL)
    return pl.pallas_call(
        _split_w_in_kernel,
        grid=(depth, d // tr),
        in_specs=[pl.BlockSpec((1, tr, n), lambda l, i: (l, i, 0))],
        out_specs=[pl.BlockSpec((1, tr, w), lambda l, i: (l, i, 0)) for w in widths],
        out_shape=[jax.ShapeDtypeStruct((depth, d, w), BF16) for w in widths],
        compiler_params=_cparams(("arbitrary", "arbitrary")),
        name="split_w_in",
    )(w_in)


def _modulated_norm(x, gain, scale, shift):
    ms = jnp.mean(x * x, axis=-1, keepdims=True)
    y = x * lax.rsqrt(ms + RMS_EPS) * gain
    return y * (1.0 + scale) + shift


def _group_rmsnorm(acc, colgain):
    out = []
    for c in range(acc.shape[1] // GROUP_DIM):
        sl = slice(c * GROUP_DIM, (c + 1) * GROUP_DIM)
        blk = acc[:, sl]
        ms = jnp.mean(blk * blk, axis=-1, keepdims=True)
        out.append(blk * lax.rsqrt(ms + RMS_EPS) * colgain[:, sl])
    return jnp.concatenate(out, axis=1)


def _norm_kernel(x_ref, g_ref, sc_ref, sh_ref, h_ref):
    h_ref[...] = _modulated_norm(x_ref[...], g_ref[...], sc_ref[0], sh_ref[0]).astype(BF16)


def _norm(x2, gain, scale, shift, *, seq, tm):
    m, d = x2.shape
    tpb = seq // tm
    mod = pl.BlockSpec((1, 1, d), lambda i: (i // tpb, 0, 0))
    return pl.pallas_call(
        _norm_kernel,
        grid=(m // tm,),
        in_specs=[pl.BlockSpec((tm, d), lambda i: (i, 0)), pl.BlockSpec((1, d), lambda i: (0, 0)), mod, mod],
        out_specs=pl.BlockSpec((tm, d), lambda i: (i, 0)),
        out_shape=jax.ShapeDtypeStruct((m, d), BF16),
        compiler_params=_cparams(("arbitrary",)),
        name="modulated_norm",
    )(x2, gain, scale, shift)


def _proj_kernel(h_ref, w_ref, cg_ref, o_ref, *, n_norm_tiles, tn):
    j = pl.program_id(1)
    h = h_ref[...]

    def tile(normed):
        for c in range(tn // MXU_WIDTH):
            cols = slice(c * MXU_WIDTH, (c + 1) * MXU_WIDTH)
            acc = jnp.dot(h, w_ref[:, cols], preferred_element_type=F32)
            if normed:
                acc = _group_rmsnorm(acc, cg_ref[:, cols])
            o_ref[:, cols] = acc.astype(o_ref.dtype)

    if n_norm_tiles == 0:
        tile(False)
    else:
        pl.when(j < n_norm_tiles)(lambda: tile(True))
        pl.when(j >= n_norm_tiles)(lambda: tile(False))


def _proj(h, w, layer, colgain, *, tm, tn, n_norm_tiles, out_dtype):
    m, d = h.shape
    n = w.shape[2]
    kern = functools.partial(_proj_kernel, n_norm_tiles=n_norm_tiles, tn=tn)
    return pl.pallas_call(
        kern,
        grid=(m // tm, n // tn),
        in_specs=[
            pl.BlockSpec((tm, d), lambda i, j: (i, 0)),
            pl.BlockSpec((None, d, tn), lambda i, j: (layer, 0, j)),
            pl.BlockSpec((1, tn), lambda i, j: (0, j)),
        ],
        out_specs=pl.BlockSpec((tm, tn), lambda i, j: (i, j)),
        out_shape=jax.ShapeDtypeStruct((m, n), out_dtype),
        compiler_params=_cparams(("arbitrary", "arbitrary")),
        name="in_proj",
    )(h, w, colgain)


def _pool_kernel(u_ref, w_ref, ps_ref, o_ref, ext_ref, *, ts):
    i = pl.program_id(1)

    @pl.when(i == 0)
    def _():
        ext_ref[0:POOL_HALO, :] = jnp.zeros((POOL_HALO, POOL_WIDTH), F32)

    @pl.when(i > 0)
    def _():
        ext_ref[0:POOL_HALO, :] = ext_ref[ts:ts + POOL_HALO, :]

    ext_ref[POOL_HALO:, :] = u_ref[0]
    pos1 = (i * ts + 1 + lax.broadcasted_iota(jnp.int32, (ts, GROUP_DIM), 0)).astype(F32)
    for g, w in enumerate(POOL_WINDOWS):
        cols = slice(g * GROUP_DIM, (g + 1) * GROUP_DIM)
        u = ext_ref[POOL_HALO:, cols]
        win = u
        for k in range(1, w):
            win = win + ext_ref[POOL_HALO - k:POOL_HALO - k + ts, cols]
        pooled = win / jnp.minimum(pos1, float(w)) - u
        y = jnp.dot(pooled.astype(BF16), w_ref[g], preferred_element_type=F32) * ps_ref[g]
        o_ref[0, :, cols] = y.astype(o_ref.dtype)


def _pool(proj3, w_pool, pool_scale, ts):
    b, s, _ = proj3.shape
    kern = functools.partial(_pool_kernel, ts=ts)
    return pl.pallas_call(
        kern,
        grid=(b, s // ts),
        in_specs=[
            pl.BlockSpec((1, ts, POOL_WIDTH), lambda bi, i: (bi, i, 0)),
            pl.BlockSpec((POOL_GROUPS, GROUP_DIM, GROUP_DIM), lambda bi, i: (0, 0, 0)),
            pl.BlockSpec((POOL_GROUPS, 1, GROUP_DIM), lambda bi, i: (0, 0, 0)),
        ],
        out_specs=pl.BlockSpec((1, ts, POOL_WIDTH), lambda bi, i: (bi, i, 0)),
        out_shape=jax.ShapeDtypeStruct((b, s, POOL_WIDTH), BF16),
        scratch_shapes=[pltpu.VMEM((ts + POOL_HALO, POOL_WIDTH), F32)],
        compiler_params=_cparams(("arbitrary", "arbitrary")),
        name="pool_mixer",
    )(proj3, w_pool, pool_scale.reshape(POOL_GROUPS, 1, GROUP_DIM))


def _sb_cumsum_matrix(tk):
    j = np.arange(tk)[:, None]
    s = np.arange(tk)[None, :]
    one = np.concatenate([(j >= s).astype(np.float32), np.ones((tk, tk), np.float32)], axis=1)
    return jnp.asarray(np.concatenate([one, one], axis=0), dtype=BF16)


def _sb_kernel(q_ref, k_ref, v_ref, w2_ref, o_ref, acc_ref, run_ref, *, tq, tk):
    i = pl.program_id(2)
    w2 = w2_ref[...]
    n_sub = tq // tk
    heads = [slice(hh * GROUP_DIM, (hh + 1) * GROUP_DIM) for hh in range(q_ref.shape[2] // GROUP_DIM)]
    acc_ref[...] = jnp.zeros(acc_ref.shape, F32)
    run_ref[...] = jnp.zeros(run_ref.shape, F32)
    row = lax.broadcasted_iota(jnp.int32, (tq, tk), 0)
    col = lax.broadcasted_iota(jnp.int32, (tq, tk), 1)
    diag_mask = col < (row & (tk - 1))
    sub = lambda r: slice(r * tk, (r + 1) * tk)

    def key_blocks(lag):
        blocks = []
        for r in range(n_sub):
            jb = i * n_sub + r - lag
            start = pl.multiple_of(jnp.maximum(jb, 0) * tk, tk)
            blocks.append((start, jnp.where(jb >= 0, 0.0, SB_MASKED_BITS)))
        return blocks

    def scores(cols, blocks, diagonal):
        z = jnp.concatenate([_nt_dot(q_ref[0, sub(r), cols], k_ref[0, pl.ds(start, tk), cols])
                             for r, (start, _) in enumerate(blocks)], axis=0)
        neg_abs = lax.bitcast_convert_type(
            lax.bitcast_convert_type(z, jnp.uint32) | jnp.uint32(0x80000000), F32)
        neg_log_keep = jnp.maximum(z, 0.0) + jnp.log(1.0 + jnp.exp2(neg_abs)) * LOG2E
        if diagonal:
            neg_log_keep = jnp.where(diag_mask, neg_log_keep, 0.0)
            z = jnp.where(diag_mask, z, -SB_MASKED_BITS)
        hi = lax.bitcast_convert_type(
            lax.bitcast_convert_type(neg_log_keep, jnp.uint32) & jnp.uint32(0xFFFF0000), F32)
        lo = neg_log_keep - hi
        c = jnp.dot(jnp.concatenate([hi.astype(BF16), lo.astype(BF16)], axis=1), w2, preferred_element_type=F32)
        return z, c

    def accumulate(hh, cols, blocks, z, c, diagonal):
        run = run_ref[hh]
        if not diagonal:
            run = run + jnp.concatenate([jnp.full((tk, tk), pen, F32) for _, pen in blocks], axis=0)
        a = jnp.exp2(z - c[:, :tk] - run).astype(BF16)
        for r, (start, _) in enumerate(blocks):
            acc_ref[hh, sub(r), :] += jnp.dot(a[sub(r), :], v_ref[0, pl.ds(start, tk), cols],
                                              preferred_element_type=F32)
        run_ref[hh] = run + c[:, tk:]

    lags = [key_blocks(lag) for lag in range(SB_STATIC_LAGS)]
    staged = [[scores(cols, blocks, lag == 0) for lag, blocks in enumerate(lags)] for cols in heads]
    for lag, blocks in enumerate(lags):
        for hh, cols in enumerate(heads):
            accumulate(hh, cols, blocks, *staged[hh][lag], lag == 0)

    for hh, cols in enumerate(heads):
        def cond(carry):
            lag, min_run = carry
            return jnp.logical_and(lag <= i * n_sub + n_sub - 1, min_run < SB_DEAD_BITS)

        def body(carry, hh=hh, cols=cols):
            lag, _ = carry
            blocks = key_blocks(lag)
            z, c = scores(cols, blocks, False)
            accumulate(hh, cols, blocks, z, c, False)
            return lag + 1, jnp.min(run_ref[hh])

        lax.while_loop(cond, body, (jnp.int32(SB_STATIC_LAGS), jnp.min(run_ref[hh])))
        o_ref[0, :, cols] = acc_ref[hh].astype(o_ref.dtype)


def _stick_breaking(qkv, tq, tk, heads_per_step):
    b, s, _ = qkv.shape
    hw = heads_per_step * GROUP_DIM
    groups = SB_HEADS // heads_per_step
    kern = functools.partial(_sb_kernel, tq=tq, tk=tk)
    return pl.pallas_call(
        kern,
        grid=(b, groups, s // tq),
        in_specs=[
            pl.BlockSpec((1, tq, hw), lambda bi, h, i: (bi, i, h)),
            pl.BlockSpec((1, s, hw), lambda bi, h, i: (bi, 0, groups + h)),
            pl.BlockSpec((1, s, hw), lambda bi, h, i: (bi, 0, 2 * groups + h)),
            pl.BlockSpec((2 * tk, 2 * tk), lambda bi, h, i: (0, 0)),
        ],
        out_specs=pl.BlockSpec((1, tq, hw), lambda bi, h, i: (bi, i, h)),
        out_shape=jax.ShapeDtypeStruct((b, s, SB_WIDTH), BF16),
        scratch_shapes=[pltpu.VMEM((heads_per_step, tq, GROUP_DIM), F32),
                        pltpu.VMEM((heads_per_step, tq, tk), F32)],
        compiler_params=_cparams(("arbitrary", "arbitrary", "arbitrary")),
        name="stick_breaking",
    )(qkv, qkv, qkv, _sb_cumsum_matrix(tk))


def _gla_constants():
    c = GLA_CHUNK
    t = np.arange(c)[:, None]
    j = np.arange(c)[None, :]
    dst = np.concatenate([j <= t, np.ones((8, c), bool)], axis=0).astype(np.float32)
    s = j
    masks = []
    for l in range(GLA_LEVELS):
        masks.append(((t ^ s) >> l == 1) & (t > s))
    masks.append(t == s)
    return jnp.asarray(dst, dtype=BF16), jnp.asarray(np.stack(masks).astype(np.float32))


def _split3(x):
    x1 = x.astype(BF16)
    r = x - x1.astype(F32)
    x2 = r.astype(BF16)
    x3 = (r - x2.astype(F32)).astype(BF16)
    return x1, x2, x3


def _gla_kernel(q_ref, k_ref, v_ref, r_ref, a_ref, wa_ref, ba_ref, og_ref, dst_ref, msk_ref,
                o_ref, st_ref):
    @pl.when(pl.program_id(1) == 0)
    def _():
        st_ref[...] = jnp.zeros(st_ref.shape, F32)

    for cc in range(q_ref.shape[1] // GLA_CHUNK):
        _gla_chunk(slice(cc * GLA_CHUNK, (cc + 1) * GLA_CHUNK), q_ref, k_ref, v_ref, r_ref, a_ref,
                   wa_ref, ba_ref, og_ref, dst_ref, msk_ref, o_ref, st_ref)


def _gla_chunk(rows, q_ref, k_ref, v_ref, r_ref, a_ref, wa_ref, ba_ref, og_ref, dst_ref, msk_ref,
               o_ref, st_ref):
    c = GLA_CHUNK
    x = jnp.dot(a_ref[0, rows, :].astype(BF16), wa_ref[...], preferred_element_type=F32) + ba_ref[...]
    log_sig = jnp.minimum(x, 0.0) - jnp.log(1.0 + jnp.exp(-jnp.abs(x)))
    g = log_sig * (LOG2E / GLA_TAU)
    dst = dst_ref[...]
    g1, g2, g3 = _split3(g)
    sums = (jnp.dot(dst, g1, preferred_element_type=F32)
            + jnp.dot(dst, g2, preferred_element_type=F32)
            + jnp.dot(dst, g3, preferred_element_type=F32))
    prefix = sums[:c]
    total = sums[c:c + 1]
    e_in = prefix
    e_out = total - prefix

    row = lax.broadcasted_iota(jnp.int32, g.shape, 0)
    pos = row & 3
    g_prev = pltpu.roll(g, 1, 0)
    g_next = pltpu.roll(g, c - 1, 0)
    e_lvl = [jnp.where((row & 1) == 1, g, 0.0),
             jnp.where(pos == 0, g_next, jnp.where(pos == 1, 0.0, jnp.where(pos == 2, g, g_prev + g)))]
    for l in range(2, GLA_LEVELS):
        m = 1 << l
        blocks = prefix.reshape(c // (2 * m), 2 * m, GLA_WIDTH)
        e_lvl.append((-jnp.abs(blocks - blocks[:, m - 1:m, :])).reshape(c, GLA_WIDTH))

    for h in range(GLA_HEADS):
        cols = slice(h * GROUP_DIM, (h + 1) * GROUP_DIM)
        q = q_ref[0, rows, cols] * (GROUP_DIM ** -0.5)
        k = k_ref[0, rows, cols]
        v = v_ref[0, rows, cols].astype(BF16)
        scores = msk_ref[GLA_LEVELS] * _nt_dot(q.astype(BF16), k.astype(BF16))
        for l in range(GLA_LEVELS):
            xl = jnp.exp2(e_lvl[l][:, cols])
            scores = scores + msk_ref[l] * _nt_dot((q * xl).astype(BF16), (k * xl).astype(BF16))
        o = jnp.dot(scores.astype(BF16), v, preferred_element_type=F32)
        x_in = jnp.exp2(e_in[:, cols])
        st = st_ref[h]
        o = o + _nt_dot((q * x_in).astype(BF16), st.astype(BF16))
        x_out = jnp.exp2(e_out[:, cols])
        x_all = jnp.exp2(total[:, cols])
        vt = v_ref[0, rows, cols].T.astype(BF16)
        st_ref[h] = st * x_all + jnp.dot(vt, (k * x_out).astype(BF16), preferred_element_type=F32)
        ms = jnp.mean(o * o, axis=-1, keepdims=True)
        o = o * lax.rsqrt(ms + RMS_EPS) * og_ref[...]
        o_ref[0, rows, cols] = (o * _silu(r_ref[0, rows, cols])).astype(o_ref.dtype)


def _gla(proj3, w_a2p, b_a2, out_gain, chunks_per_step):
    b, s, _ = proj3.shape
    c = GLA_CHUNK * chunks_per_step
    dst, masks = _gla_constants()
    wide = lambda blk: pl.BlockSpec((1, c, GLA_WIDTH), lambda bi, ci: (bi, ci, blk))
    const2 = lambda shape: pl.BlockSpec(shape, lambda bi, ci: (0, 0))
    a_blk = (POOL_WIDTH + 4 * GLA_WIDTH) // GROUP_DIM
    return pl.pallas_call(
        _gla_kernel,
        grid=(b, s // c),
        in_specs=[
            wide(1), wide(2), wide(3), wide(4),
            pl.BlockSpec((1, c, GROUP_DIM), lambda bi, ci: (bi, ci, a_blk)),
            const2((GROUP_DIM, GLA_WIDTH)),
            const2((1, GLA_WIDTH)),
            const2((1, GROUP_DIM)),
            const2(dst.shape),
            pl.BlockSpec(masks.shape, lambda bi, ci: (0, 0, 0)),
        ],
        out_specs=pl.BlockSpec((1, c, GLA_WIDTH), lambda bi, ci: (bi, ci, 0)),
        out_shape=jax.ShapeDtypeStruct((b, s, GLA_WIDTH), BF16),
        scratch_shapes=[pltpu.VMEM((GLA_HEADS, GROUP_DIM, GROUP_DIM), F32)],
        compiler_params=_cparams(("arbitrary", "arbitrary")),
        name="gla",
    )(proj3, proj3, proj3, proj3, proj3, w_a2p, b_a2, out_gain, dst, masks)


def _merge_kernel(h_ref, yp_ref, ys_ref, yg_ref, wgp_ref, wgs_ref, wgg_ref, wp_ref, ws_ref, wg_ref, o_ref):
    h = h_ref[...]

    def branch(wgate_ref, y_ref, w_ref):
        gate = jax.nn.sigmoid(jnp.dot(h, wgate_ref[...], preferred_element_type=F32))
        return gate * jnp.dot(y_ref[...], w_ref[...], preferred_element_type=F32)

    merged = branch(wgp_ref, yp_ref, wp_ref) + branch(wgs_ref, ys_ref, ws_ref) + branch(wgg_ref, yg_ref, wg_ref)
    o_ref[...] = merged.astype(o_ref.dtype)


def _merge(h, y_pool, y_sb, y_gla, w_gates, w_br_pool, w_br_sb, w_br_gla, layer, *, tm, tn):
    m, d = h.shape
    nj = d // tn
    row = lambda width: pl.BlockSpec((tm, width), lambda i, j: (i, 0))
    gate_w = lambda br: pl.BlockSpec((None, d, tn), lambda i, j: (layer, 0, br * nj + j))
    br_w = lambda width: pl.BlockSpec((None, width, tn), lambda i, j: (layer, 0, j))
    return pl.pallas_call(
        _merge_kernel,
        grid=(m // tm, nj),
        in_specs=[
            row(d), row(POOL_WIDTH), row(SB_WIDTH), row(GLA_WIDTH),
            gate_w(0), gate_w(1), gate_w(2),
            br_w(POOL_WIDTH), br_w(SB_WIDTH), br_w(GLA_WIDTH),
        ],
        out_specs=pl.BlockSpec((tm, tn), lambda i, j: (i, j)),
        out_shape=jax.ShapeDtypeStruct((m, d), BF16),
        compiler_params=_cparams(("arbitrary", "arbitrary")),
        name="gated_merge",
    )(h, y_pool, y_sb, y_gla, w_gates, w_gates, w_gates, w_br_pool, w_br_sb, w_br_gla)


def _resproj_kernel(a_ref, w_ref, x_ref, ga_ref, g_ref, sc_ref, sh_ref, o_ref, h_ref):
    for r in range(a_ref.shape[0] // ROW_CHUNK):
        rows = slice(r * ROW_CHUNK, (r + 1) * ROW_CHUNK)
        y = jnp.dot(a_ref[rows, :], w_ref[...], preferred_element_type=F32)
        x_new = x_ref[rows, :] + ga_ref[0] * y
        o_ref[rows, :] = x_new
        h_ref[rows, :] = _modulated_norm(x_new, g_ref[...], sc_ref[0], sh_ref[0]).astype(BF16)


def _resproj(a, w, layer, x2, gate, gain, scale, shift, *, seq, tm):
    m, kdim = a.shape
    d = w.shape[2]
    tpb = seq // tm
    row = lambda width: pl.BlockSpec((tm, width), lambda i: (i, 0))
    mod = pl.BlockSpec((1, 1, d), lambda i: (i // tpb, 0, 0))
    return pl.pallas_call(
        _resproj_kernel,
        grid=(m // tm,),
        in_specs=[row(kdim), pl.BlockSpec((None, kdim, d), lambda i: (layer, 0, 0)), row(d), mod,
                  pl.BlockSpec((1, d), lambda i: (0, 0)), mod, mod],
        out_specs=[row(d), row(d)],
        out_shape=[jax.ShapeDtypeStruct((m, d), F32), jax.ShapeDtypeStruct((m, d), BF16)],
        compiler_params=_cparams(("arbitrary",)),
        name="residual_proj",
    )(a, w, x2, gate, gain, scale, shift)


def _swiglu_kernel(*refs, emit_next, x_chunks):
    if emit_next:
        h_ref, x_ref, ga_ref, wg_ref, wu_ref, wd_ref, g_ref, sc_ref, sh_ref, o_ref, hn_ref = refs
    else:
        h_ref, x_ref, ga_ref, wg_ref, wu_ref, wd_ref, o_ref = refs
    f = pl.program_id(1)
    last = pl.num_programs(1) - 1
    xw = x_ref.shape[1]

    def gated_partial(rows):
        h = h_ref[rows, :]
        gate = jnp.dot(h, wg_ref[...], preferred_element_type=F32)
        up = jnp.dot(h, wu_ref[...], preferred_element_type=F32)
        act = (_silu(gate) * up).astype(BF16)
        return ga_ref[0] * jnp.dot(act, wd_ref[...], preferred_element_type=F32)

    row_chunks = [slice(r * ROW_CHUNK, (r + 1) * ROW_CHUNK) for r in range(h_ref.shape[0] // ROW_CHUNK)]

    @pl.when(f == 0)
    def _():
        for rows in row_chunks:
            o_ref[rows, :] = gated_partial(rows)

    @pl.when(jnp.logical_and(f > 0, f < last))
    def _():
        for rows in row_chunks:
            o_ref[rows, :] += gated_partial(rows)

    for c in range(x_chunks):
        @pl.when(f == c)
        def _():
            o_ref[:, c * xw:(c + 1) * xw] += x_ref[...]

    @pl.when(f == last)
    def _():
        for rows in row_chunks:
            x_new = o_ref[rows, :] + gated_partial(rows)
            o_ref[rows, :] = x_new
            if emit_next:
                hn_ref[rows, :] = _modulated_norm(x_new, g_ref[...], sc_ref[0], sh_ref[0]).astype(BF16)


def _swiglu(h, x2, gate, w_gate, w_up, w_down, layer, next_norm, *, seq, tm, tf):
    m, d = x2.shape
    dff = w_gate.shape[2]
    tpb = seq // tm
    n_steps = dff // tf
    x_chunks = 8
    assert x_chunks < n_steps and d % (x_chunks * GROUP_DIM) == 0
    emit_next = next_norm is not None
    row = pl.BlockSpec((tm, d), lambda i, f: (i, 0))
    mod = pl.BlockSpec((1, 1, d), lambda i, f: (i // tpb, 0, 0))
    in_specs = [row, pl.BlockSpec((tm, d // x_chunks), lambda i, f: (i, jnp.minimum(f, x_chunks - 1))), mod,
                pl.BlockSpec((None, d, tf), lambda i, f: (layer, 0, f)),
                pl.BlockSpec((None, d, tf), lambda i, f: (layer, 0, f)),
                pl.BlockSpec((None, tf, d), lambda i, f: (layer, f, 0))]
    args = [h, x2, gate, w_gate, w_up, w_down]
    out_specs = [row]
    out_shape = [jax.ShapeDtypeStruct((m, d), F32)]
    if emit_next:
        in_specs += [pl.BlockSpec((1, d), lambda i, f: (0, 0)), mod, mod]
        args += list(next_norm)
        out_specs.append(row)
        out_shape.append(jax.ShapeDtypeStruct((m, d), BF16))
    outs = pl.pallas_call(
        functools.partial(_swiglu_kernel, emit_next=emit_next, x_chunks=x_chunks),
        grid=(m // tm, n_steps),
        in_specs=in_specs,
        out_specs=out_specs,
        out_shape=out_shape,
        compiler_params=_cparams(("arbitrary", "arbitrary")),
        name="swiglu",
    )(*args)
    return (outs[0], outs[1]) if emit_next else (outs[0], None)


class _Tiles(NamedTuple):
    rows: int
    rows_f32: int
    pool_rows: int
    qkv_cols: int
    merge_cols: int
    ff_cols: int
    gla_chunks: int
    sb_heads: int


def _tile(n, pref):
    t = min(n, pref)
    assert n % t == 0, (n, t)
    return t


def _tiles(seq):
    return _Tiles(rows=_tile(seq, 1024), rows_f32=_tile(seq, 512), pool_rows=_tile(seq, 2048),
                  qkv_cols=1024, merge_cols=512, ff_cols=512,
                  gla_chunks=_tile(seq // GLA_CHUNK, 8), sb_heads=4)


def _mod_vectors(mod_l):
    d = D_MODEL
    return [mod_l[:, None, k * d:(k + 1) * d] for k in range(6)]


class _Weights(NamedTuple):
    f32cols: jax.Array
    sb: jax.Array
    gates: jax.Array
    br_pool: jax.Array
    br_sb: jax.Array
    br_gla: jax.Array
    out: jax.Array
    ff_gate: jax.Array
    ff_up: jax.Array
    ff_down: jax.Array


def _layer(x2, h1, mod_l, next_norm, batch, seq, layer, w, w_pool, pool_scale, sb_q_gain, sb_k_gain,
           gla_w_a2, gla_b_a2, gla_out_gain, g_norm2):
    d = D_MODEL
    t = _tiles(seq)
    _, _, ga1, sh2, sc2, ga2 = _mod_vectors(mod_l)

    proj = _proj(h1, w.f32cols, layer, jnp.ones((1, F32_COLS), F32), tm=t.rows_f32, tn=F32_COLS,
                 n_norm_tiles=0, out_dtype=F32)
    q_scale = LOG2E / math.sqrt(GROUP_DIM)
    colgain = jnp.concatenate([jnp.tile(sb_q_gain * q_scale, SB_HEADS), jnp.tile(sb_k_gain, SB_HEADS),
                               jnp.ones((SB_WIDTH,), F32)]).reshape(1, 3 * SB_WIDTH)
    assert (2 * SB_WIDTH) % t.qkv_cols == 0
    qkv = _proj(h1, w.sb, layer, colgain, tm=t.rows, tn=t.qkv_cols, n_norm_tiles=2 * SB_WIDTH // t.qkv_cols,
                out_dtype=BF16)

    proj3 = proj.reshape(batch, seq, F32_COLS)
    y_pool = _pool(proj3, w_pool.astype(BF16), pool_scale, ts=t.pool_rows)
    y_sb = _stick_breaking(qkv.reshape(batch, seq, 3 * SB_WIDTH), tq=t.rows, tk=GROUP_DIM,
                           heads_per_step=t.sb_heads)
    w_a2p = jnp.pad(gla_w_a2, ((0, GROUP_DIM - GLA_RANK), (0, 0))).astype(BF16)
    y_gla = _gla(proj3, w_a2p, gla_b_a2.reshape(1, GLA_WIDTH), gla_out_gain.reshape(1, GROUP_DIM),
                 chunks_per_step=t.gla_chunks)

    m = batch * seq
    merged = _merge(h1, y_pool.reshape(m, POOL_WIDTH), y_sb.reshape(m, SB_WIDTH), y_gla.reshape(m, GLA_WIDTH),
                    w.gates, w.br_pool, w.br_sb, w.br_gla, layer, tm=t.rows, tn=t.merge_cols)
    x2, h2 = _resproj(merged, w.out, layer, x2, ga1, g_norm2.reshape(1, d), sc2, sh2, seq=seq, tm=t.rows_f32)
    return _swiglu(h2, x2, ga2, w.ff_gate, w.ff_up, w.ff_down, layer, next_norm, seq=seq, tm=t.rows,
                   tf=t.ff_cols)


def kernel(x, c, w_ada, b_ada, g_norm1, w_in, w_pool, pool_scale, sb_q_gain, sb_k_gain, gla_w_a2, gla_b_a2, gla_out_gain, w_br_pool, w_br_sb, w_br_gla, w_out, g_norm2, w_ff_gate, w_ff_up, w_ff_down):
    batch, seq, d = x.shape
    depth = w_ada.shape[0]
    c_pad = jnp.pad(c, ((0, 8 - batch % 8 if batch % 8 else 0), (0, 0)))
    mod = _ada(c_pad, w_ada, b_ada)[:, :batch]
    x2 = x.reshape(batch * seq, d)
    w = _Weights(*_split_w_in(w_in.astype(BF16)), *[a.astype(BF16) for a in (w_br_pool, w_br_sb, w_br_gla, w_out,
                                                               w_ff_gate, w_ff_up, w_ff_down)])

    def first_norm(l):
        sh1, sc1 = _mod_vectors(mod[l])[:2]
        return g_norm1[l].reshape(1, d), sc1, sh1

    h1 = _norm(x2, *first_norm(0), seq=seq, tm=_tiles(seq).rows_f32)
    for l in range(depth):
        next_norm = first_norm(l + 1) if l + 1 < depth else None
        x2, h1 = _layer(x2, h1, mod[l], next_norm, batch, seq, l, w, w_pool[l], pool_scale[l],
                        sb_q_gain[l], sb_k_gain[l], gla_w_a2[l], gla_b_a2[l], gla_out_gain[l], g_norm2[l])
    return x2.reshape(batch, seq, d)
```

```python
import functools
import math
from typing import NamedTuple

import numpy as np
import jax
import jax.numpy as jnp
from jax import lax
from jax.experimental import pallas as pl
from jax.experimental.pallas import tpu as pltpu

F32 = jnp.float32
BF16 = jnp.bfloat16

D_MODEL = 2048
POOL_WINDOWS = (2, 4, 8, 16)
POOL_GROUPS = 4
GROUP_DIM = 128
POOL_WIDTH = POOL_GROUPS * GROUP_DIM
SB_HEADS = 8
SB_WIDTH = SB_HEADS * GROUP_DIM
GLA_HEADS = 4
GLA_WIDTH = GLA_HEADS * GROUP_DIM
GLA_RANK = 16
GLA_TAU = 16.0
GLA_CHUNK = 128
GLA_LEVELS = GLA_CHUNK.bit_length() - 1
assert GLA_CHUNK == 1 << GLA_LEVELS
N_BRANCH = 3
RMS_EPS = 1e-6
LOG2E = 1.4426950408889634
POOL_HALO = 16
SB_DEAD_BITS = 160.0
SB_MASKED_BITS = 1.0e4
SB_STATIC_LAGS = 3

VMEM_LIMIT = 56 * 1024 * 1024
MXU_WIDTH = 256
ROW_CHUNK = 256


def _cparams(sem):
    return pltpu.CompilerParams(dimension_semantics=sem, vmem_limit_bytes=VMEM_LIMIT)


def _nt_dot(a, b):
    return lax.dot_general(a, b, (((1,), (1,)), ((), ())), preferred_element_type=F32)


def _silu(x):
    return x * jax.nn.sigmoid(x)


def _ada_kernel(c_ref, w_ref, b_ref, o_ref):
    c = c_ref[...]
    a = _silu(c).astype(BF16)
    o_ref[0] = jnp.dot(a, w_ref[0].astype(BF16), preferred_element_type=F32) + b_ref[0]


def _ada(c_pad, w_ada, b_ada, tn=1024):
    depth, d, n = w_ada.shape
    rows = c_pad.shape[0]
    return pl.pallas_call(
        _ada_kernel,
        grid=(depth, n // tn),
        in_specs=[
            pl.BlockSpec((rows, d), lambda l, j: (0, 0)),
            pl.BlockSpec((1, d, tn), lambda l, j: (l, 0, j)),
            pl.BlockSpec((1, 1, tn), lambda l, j: (l, 0, j)),
        ],
        out_specs=pl.BlockSpec((1, rows, tn), lambda l, j: (l, 0, j)),
        out_shape=jax.ShapeDtypeStruct((depth, rows, n), F32),
        compiler_params=_cparams(("arbitrary", "arbitrary")),
        name="ada_modulation",
    )(c_pad, w_ada, b_ada.reshape(depth, 1, n))


W_IN_SB = POOL_WIDTH
W_IN_GLA = W_IN_SB + 3 * SB_WIDTH
W_IN_A = W_IN_GLA + 4 * GLA_WIDTH
W_IN_GATE = W_IN_A + GLA_RANK
W_IN_COLS = W_IN_GATE + N_BRANCH * D_MODEL
A_LOW_COL = POOL_WIDTH + 4 * GLA_WIDTH
F32_COLS = A_LOW_COL + MXU_WIDTH


def _split_w_in_kernel(w_ref, f_ref, sb_ref, g_ref):
    f_ref[0, :, :POOL_WIDTH] = w_ref[0, :, :W_IN_SB]
    f_ref[0, :, POOL_WIDTH:A_LOW_COL] = w_ref[0, :, W_IN_GLA:W_IN_A]
    f_ref[0, :, A_LOW_COL:] = jnp.zeros((f_ref.shape[1], MXU_WIDTH), BF16)
    f_ref[0, :, A_LOW_COL:A_LOW_COL + GLA_RANK] = w_ref[0, :, W_IN_A:W_IN_GATE]
    sb_ref[0] = w_ref[0, :, W_IN_SB:W_IN_GLA]
    g_ref[0] = w_ref[0, :, W_IN_GATE:]


def _split_w_in(w_in, tr=512):
    depth, d, n = w_in.shape
    assert n == W_IN_COLS and w_in.dtype == BF16
    widths = (F32_COLS, 3 * SB_WIDTH, N_BRANCH * D_MODEL)
    return pl.pallas_call(
        _split_w_in_kernel,
        grid=(depth, d // tr),
        in_specs=[pl.BlockSpec((1, tr, n), lambda l, i: (l, i, 0))],
        out_specs=[pl.BlockSpec((1, tr, w), lambda l, i: (l, i, 0)) for w in widths],
        out_shape=[jax.ShapeDtypeStruct((depth, d, w), BF16) for w in widths],
        compiler_params=_cparams(("arbitrary", "arbitrary")),
        name="split_w_in",
    )(w_in)


def _modulated_norm(x, gain, scale, shift):
    ms = jnp.mean(x * x, axis=-1, keepdims=True)
    y = x * lax.rsqrt(ms + RMS_EPS) * gain
    return y * (1.0 + scale) + shift


def _group_rmsnorm(acc, colgain):
    out = []
    for c in range(acc.shape[1] // GROUP_DIM):
        sl = slice(c * GROUP_DIM, (c + 1) * GROUP_DIM)
        blk = acc[:, sl]
        ms = jnp.mean(blk * blk, axis=-1, keepdims=True)
        out.append(blk * lax.rsqrt(ms + RMS_EPS) * colgain[:, sl])
    return jnp.concatenate(out, axis=1)


def _norm_kernel(x_ref, g_ref, sc_ref, sh_ref, h_ref):
    h_ref[...] = _modulated_norm(x_ref[...], g_ref[...], sc_ref[0], sh_ref[0]).astype(BF16)


def _norm(x2, gain, scale, shift, *, seq, tm):
    m, d = x2.shape
    tpb = seq // tm
    mod = pl.BlockSpec((1, 1, d), lambda i: (i // tpb, 0, 0))
    return pl.pallas_call(
        _norm_kernel,
        grid=(m // tm,),
        in_specs=[pl.BlockSpec((tm, d), lambda i: (i, 0)), pl.BlockSpec((1, d), lambda i: (0, 0)), mod, mod],
        out_specs=pl.BlockSpec((tm, d), lambda i: (i, 0)),
        out_shape=jax.ShapeDtypeStruct((m, d), BF16),
        compiler_params=_cparams(("arbitrary",)),
        name="modulated_norm",
    )(x2, gain, scale, shift)


def _proj_kernel(h_ref, w_ref, cg_ref, o_ref, *, norm_cols):
    h = h_ref[...]
    for c in range(o_ref.shape[1] // MXU_WIDTH):
        cols = slice(c * MXU_WIDTH, (c + 1) * MXU_WIDTH)
        acc = jnp.dot(h, w_ref[:, cols], preferred_element_type=F32)
        if c * MXU_WIDTH < norm_cols:
            acc = _group_rmsnorm(acc, cg_ref[:, cols])
        o_ref[:, cols] = acc.astype(o_ref.dtype)


def _proj(h, w, layer, colgain, *, tm, norm_cols, out_dtype):
    m, d = h.shape
    n = w.shape[2]
    assert n % MXU_WIDTH == 0 and norm_cols % MXU_WIDTH == 0
    return pl.pallas_call(
        functools.partial(_proj_kernel, norm_cols=norm_cols),
        grid=(m // tm,),
        in_specs=[
            pl.BlockSpec((tm, d), lambda i: (i, 0)),
            pl.BlockSpec((None, d, n), lambda i: (layer, 0, 0)),
            pl.BlockSpec((1, n), lambda i: (0, 0)),
        ],
        out_specs=pl.BlockSpec((tm, n), lambda i: (i, 0)),
        out_shape=jax.ShapeDtypeStruct((m, n), out_dtype),
        compiler_params=_cparams(("arbitrary",)),
        name="in_proj",
    )(h, w, colgain)


def _pool_kernel(u_ref, w_ref, ps_ref, o_ref, ext_ref, *, ts):
    i = pl.program_id(1)

    @pl.when(i == 0)
    def _():
        ext_ref[0:POOL_HALO, :] = jnp.zeros((POOL_HALO, POOL_WIDTH), F32)

    @pl.when(i > 0)
    def _():
        ext_ref[0:POOL_HALO, :] = ext_ref[ts:ts + POOL_HALO, :]

    ext_ref[POOL_HALO:, :] = u_ref[0]
    pos1 = (i * ts + 1 + lax.broadcasted_iota(jnp.int32, (ts, GROUP_DIM), 0)).astype(F32)
    for g, w in enumerate(POOL_WINDOWS):
        cols = slice(g * GROUP_DIM, (g + 1) * GROUP_DIM)
        u = ext_ref[POOL_HALO:, cols]
        win = u
        for k in range(1, w):
            win = win + ext_ref[POOL_HALO - k:POOL_HALO - k + ts, cols]
        pooled = win / jnp.minimum(pos1, float(w)) - u
        y = jnp.dot(pooled.astype(BF16), w_ref[g], preferred_element_type=F32) * ps_ref[g]
        o_ref[0, :, cols] = y.astype(o_ref.dtype)


def _pool(proj3, w_pool, pool_scale, ts):
    b, s, _ = proj3.shape
    kern = functools.partial(_pool_kernel, ts=ts)
    return pl.pallas_call(
        kern,
        grid=(b, s // ts),
        in_specs=[
            pl.BlockSpec((1, ts, POOL_WIDTH), lambda bi, i: (bi, i, 0)),
            pl.BlockSpec((POOL_GROUPS, GROUP_DIM, GROUP_DIM), lambda bi, i: (0, 0, 0)),
            pl.BlockSpec((POOL_GROUPS, 1, GROUP_DIM), lambda bi, i: (0, 0, 0)),
        ],
        out_specs=pl.BlockSpec((1, ts, POOL_WIDTH), lambda bi, i: (bi, i, 0)),
        out_shape=jax.ShapeDtypeStruct((b, s, POOL_WIDTH), BF16),
        scratch_shapes=[pltpu.VMEM((ts + POOL_HALO, POOL_WIDTH), F32)],
        compiler_params=_cparams(("arbitrary", "arbitrary")),
        name="pool_mixer",
    )(proj3, w_pool, pool_scale.reshape(POOL_GROUPS, 1, GROUP_DIM))


def _sb_cumsum_matrix(tk):
    j = np.arange(tk)[:, None]
    s = np.arange(tk)[None, :]
    one = np.concatenate([(j >= s).astype(np.float32), np.ones((tk, tk), np.float32)], axis=1)
    return jnp.asarray(np.concatenate([one, one], axis=0), dtype=BF16)


def _sb_kernel(q_ref, k_ref, v_ref, w2_ref, o_ref, acc_ref, run_ref, *, tq, tk):
    i = pl.program_id(2)
    w2 = w2_ref[...]
    n_sub = tq // tk
    heads = [slice(hh * GROUP_DIM, (hh + 1) * GROUP_DIM) for hh in range(q_ref.shape[2] // GROUP_DIM)]
    acc_ref[...] = jnp.zeros(acc_ref.shape, F32)
    run_ref[...] = jnp.zeros(run_ref.shape, F32)
    row = lax.broadcasted_iota(jnp.int32, (tq, tk), 0)
    col = lax.broadcasted_iota(jnp.int32, (tq, tk), 1)
    diag_mask = col < (row & (tk - 1))
    sub = lambda r: slice(r * tk, (r + 1) * tk)

    def key_blocks(lag):
        blocks = []
        for r in range(n_sub):
            jb = i * n_sub + r - lag
            start = pl.multiple_of(jnp.maximum(jb, 0) * tk, tk)
            blocks.append((start, jnp.where(jb >= 0, 0.0, SB_MASKED_BITS)))
        return blocks

    def scores(cols, blocks, diagonal):
        z = jnp.concatenate([_nt_dot(q_ref[0, sub(r), cols], k_ref[0, pl.ds(start, tk), cols])
                             for r, (start, _) in enumerate(blocks)], axis=0)
        neg_abs = lax.bitcast_convert_type(
            lax.bitcast_convert_type(z, jnp.uint32) | jnp.uint32(0x80000000), F32)
        neg_log_keep = jnp.maximum(z, 0.0) + jnp.log(1.0 + jnp.exp2(neg_abs)) * LOG2E
        if diagonal:
            neg_log_keep = jnp.where(diag_mask, neg_log_keep, 0.0)
            z = jnp.where(diag_mask, z, -SB_MASKED_BITS)
        hi = lax.bitcast_convert_type(
            lax.bitcast_convert_type(neg_log_keep, jnp.uint32) & jnp.uint32(0xFFFF0000), F32)
        lo = neg_log_keep - hi
        c = jnp.dot(jnp.concatenate([hi.astype(BF16), lo.astype(BF16)], axis=1), w2, preferred_element_type=F32)
        return z, c

    def accumulate(hh, cols, blocks, z, c, diagonal):
        run = run_ref[hh]
        if not diagonal:
            run = run + jnp.concatenate([jnp.full((tk, tk), pen, F32) for _, pen in blocks], axis=0)
        a = jnp.exp2(z - c[:, :tk] - run).astype(BF16)
        for r, (start, _) in enumerate(blocks):
            acc_ref[hh, sub(r), :] += jnp.dot(a[sub(r), :], v_ref[0, pl.ds(start, tk), cols],
                                              preferred_element_type=F32)
        run_ref[hh] = run + c[:, tk:]

    lags = [key_blocks(lag) for lag in range(SB_STATIC_LAGS)]
    staged = [[scores(cols, blocks, lag == 0) for lag, blocks in enumerate(lags)] for cols in heads]
    for lag, blocks in enumerate(lags):
        for hh, cols in enumerate(heads):
            accumulate(hh, cols, blocks, *staged[hh][lag], lag == 0)

    for hh, cols in enumerate(heads):
        def cond(carry):
            lag, min_run = carry
            return jnp.logical_and(lag <= i * n_sub + n_sub - 1, min_run < SB_DEAD_BITS)

        def body(carry, hh=hh, cols=cols):
            lag, _ = carry
            blocks = key_blocks(lag)
            z, c = scores(cols, blocks, False)
            accumulate(hh, cols, blocks, z, c, False)
            return lag + 1, jnp.min(run_ref[hh])

        lax.while_loop(cond, body, (jnp.int32(SB_STATIC_LAGS), jnp.min(run_ref[hh])))
        o_ref[0, :, cols] = acc_ref[hh].astype(o_ref.dtype)


def _stick_breaking(qkv, tq, tk, heads_per_step):
    b, s, _ = qkv.shape
    hw = heads_per_step * GROUP_DIM
    groups = SB_HEADS // heads_per_step
    kern = functools.partial(_sb_kernel, tq=tq, tk=tk)
    return pl.pallas_call(
        kern,
        grid=(b, groups, s // tq),
        in_specs=[
            pl.BlockSpec((1, tq, hw), lambda bi, h, i: (bi, i, h)),
            pl.BlockSpec((1, s, hw), lambda bi, h, i: (bi, 0, groups + h)),
            pl.BlockSpec((1, s, hw), lambda bi, h, i: (bi, 0, 2 * groups + h)),
            pl.BlockSpec((2 * tk, 2 * tk), lambda bi, h, i: (0, 0)),
        ],
        out_specs=pl.BlockSpec((1, tq, hw), lambda bi, h, i: (bi, i, h)),
        out_shape=jax.ShapeDtypeStruct((b, s, SB_WIDTH), BF16),
        scratch_shapes=[pltpu.VMEM((heads_per_step, tq, GROUP_DIM), F32),
                        pltpu.VMEM((heads_per_step, tq, tk), F32)],
        compiler_params=_cparams(("arbitrary", "arbitrary", "arbitrary")),
        name="stick_breaking",
    )(qkv, qkv, qkv, _sb_cumsum_matrix(tk))


def _gla_constants():
    c = GLA_CHUNK
    t = np.arange(c)[:, None]
    j = np.arange(c)[None, :]
    dst = np.concatenate([j <= t, np.ones((8, c), bool)], axis=0).astype(np.float32)
    s = j
    masks = []
    for l in range(GLA_LEVELS):
        masks.append(((t ^ s) >> l == 1) & (t > s))
    masks.append(t == s)
    return jnp.asarray(dst, dtype=BF16), jnp.asarray(np.stack(masks).astype(np.float32))


def _split3(x):
    x1 = x.astype(BF16)
    r = x - x1.astype(F32)
    x2 = r.astype(BF16)
    x3 = (r - x2.astype(F32)).astype(BF16)
    return x1, x2, x3


def _gla_kernel(q_ref, k_ref, v_ref, r_ref, a_ref, wa_ref, ba_ref, og_ref, dst_ref, msk_ref,
                o_ref, st_ref):
    @pl.when(pl.program_id(1) == 0)
    def _():
        st_ref[...] = jnp.zeros(st_ref.shape, F32)

    for cc in range(q_ref.shape[1] // GLA_CHUNK):
        _gla_chunk(slice(cc * GLA_CHUNK, (cc + 1) * GLA_CHUNK), q_ref, k_ref, v_ref, r_ref, a_ref,
                   wa_ref, ba_ref, og_ref, dst_ref, msk_ref, o_ref, st_ref)


def _gla_chunk(rows, q_ref, k_ref, v_ref, r_ref, a_ref, wa_ref, ba_ref, og_ref, dst_ref, msk_ref,
               o_ref, st_ref):
    c = GLA_CHUNK
    x = jnp.dot(a_ref[0, rows, :].astype(BF16), wa_ref[...], preferred_element_type=F32) + ba_ref[...]
    log_sig = jnp.minimum(x, 0.0) - jnp.log(1.0 + jnp.exp(-jnp.abs(x)))
    g = log_sig * (LOG2E / GLA_TAU)
    dst = dst_ref[...]
    g1, g2, g3 = _split3(g)
    sums = (jnp.dot(dst, g1, preferred_element_type=F32)
            + jnp.dot(dst, g2, preferred_element_type=F32)
            + jnp.dot(dst, g3, preferred_element_type=F32))
    prefix = sums[:c]
    total = sums[c:c + 1]
    e_in = prefix
    e_out = total - prefix

    row = lax.broadcasted_iota(jnp.int32, g.shape, 0)
    pos = row & 3
    g_prev = pltpu.roll(g, 1, 0)
    g_next = pltpu.roll(g, c - 1, 0)
    e_lvl = [jnp.where((row & 1) == 1, g, 0.0),
             jnp.where(pos == 0, g_next, jnp.where(pos == 1, 0.0, jnp.where(pos == 2, g, g_prev + g)))]
    for l in range(2, GLA_LEVELS):
        m = 1 << l
        blocks = prefix.reshape(c // (2 * m), 2 * m, GLA_WIDTH)
        e_lvl.append((-jnp.abs(blocks - blocks[:, m - 1:m, :])).reshape(c, GLA_WIDTH))

    for h in range(GLA_HEADS):
        cols = slice(h * GROUP_DIM, (h + 1) * GROUP_DIM)
        q = q_ref[0, rows, cols] * (GROUP_DIM ** -0.5)
        k = k_ref[0, rows, cols]
        v = v_ref[0, rows, cols].astype(BF16)
        scores = msk_ref[GLA_LEVELS] * _nt_dot(q.astype(BF16), k.astype(BF16))
        for l in range(GLA_LEVELS):
            xl = jnp.exp2(e_lvl[l][:, cols])
            scores = scores + msk_ref[l] * _nt_dot((q * xl).astype(BF16), (k * xl).astype(BF16))
        o = jnp.dot(scores.astype(BF16), v, preferred_element_type=F32)
        x_in = jnp.exp2(e_in[:, cols])
        st = st_ref[h]
        o = o + _nt_dot((q * x_in).astype(BF16), st.astype(BF16))
        x_out = jnp.exp2(e_out[:, cols])
        x_all = jnp.exp2(total[:, cols])
        vt = v_ref[0, rows, cols].T.astype(BF16)
        st_ref[h] = st * x_all + jnp.dot(vt, (k * x_out).astype(BF16), preferred_element_type=F32)
        ms = jnp.mean(o * o, axis=-1, keepdims=True)
        o = o * lax.rsqrt(ms + RMS_EPS) * og_ref[...]
        o_ref[0, rows, cols] = (o * _silu(r_ref[0, rows, cols])).astype(o_ref.dtype)


def _gla(proj3, w_a2p, b_a2, out_gain, chunks_per_step):
    b, s, _ = proj3.shape
    c = GLA_CHUNK * chunks_per_step
    dst, masks = _gla_constants()
    wide = lambda blk: pl.BlockSpec((1, c, GLA_WIDTH), lambda bi, ci: (bi, ci, blk))
    const2 = lambda shape: pl.BlockSpec(shape, lambda bi, ci: (0, 0))
    a_blk = (POOL_WIDTH + 4 * GLA_WIDTH) // GROUP_DIM
    return pl.pallas_call(
        _gla_kernel,
        grid=(b, s // c),
        in_specs=[
            wide(1), wide(2), wide(3), wide(4),
            pl.BlockSpec((1, c, GROUP_DIM), lambda bi, ci: (bi, ci, a_blk)),
            const2((GROUP_DIM, GLA_WIDTH)),
            const2((1, GLA_WIDTH)),
            const2((1, GROUP_DIM)),
            const2(dst.shape),
            pl.BlockSpec(masks.shape, lambda bi, ci: (0, 0, 0)),
        ],
        out_specs=pl.BlockSpec((1, c, GLA_WIDTH), lambda bi, ci: (bi, ci, 0)),
        out_shape=jax.ShapeDtypeStruct((b, s, GLA_WIDTH), BF16),
        scratch_shapes=[pltpu.VMEM((GLA_HEADS, GROUP_DIM, GROUP_DIM), F32)],
        compiler_params=_cparams(("arbitrary", "arbitrary")),
        name="gla",
    )(proj3, proj3, proj3, proj3, proj3, w_a2p, b_a2, out_gain, dst, masks)


def _merge_kernel(h_ref, yp_ref, ys_ref, yg_ref, wgp_ref, wgs_ref, wgg_ref, wp_ref, ws_ref, wg_ref, o_ref):
    h = h_ref[...]

    def branch(wgate_ref, y_ref, w_ref):
        gate = jax.nn.sigmoid(jnp.dot(h, wgate_ref[...], preferred_element_type=F32))
        return gate * jnp.dot(y_ref[...], w_ref[...], preferred_element_type=F32)

    merged = branch(wgp_ref, yp_ref, wp_ref) + branch(wgs_ref, ys_ref, ws_ref) + branch(wgg_ref, yg_ref, wg_ref)
    o_ref[...] = merged.astype(o_ref.dtype)


def _merge(h, y_pool, y_sb, y_gla, w_gates, w_br_pool, w_br_sb, w_br_gla, layer, *, tm, tn):
    m, d = h.shape
    nj = d // tn
    row = lambda width: pl.BlockSpec((tm, width), lambda i, j: (i, 0))
    gate_w = lambda br: pl.BlockSpec((None, d, tn), lambda i, j: (layer, 0, br * nj + j))
    br_w = lambda width: pl.BlockSpec((None, width, tn), lambda i, j: (layer, 0, j))
    return pl.pallas_call(
        _merge_kernel,
        grid=(m // tm, nj),
        in_specs=[
            row(d), row(POOL_WIDTH), row(SB_WIDTH), row(GLA_WIDTH),
            gate_w(0), gate_w(1), gate_w(2),
            br_w(POOL_WIDTH), br_w(SB_WIDTH), br_w(GLA_WIDTH),
        ],
        out_specs=pl.BlockSpec((tm, tn), lambda i, j: (i, j)),
        out_shape=jax.ShapeDtypeStruct((m, d), BF16),
        compiler_params=_cparams(("arbitrary", "arbitrary")),
        name="gated_merge",
    )(h, y_pool, y_sb, y_gla, w_gates, w_gates, w_gates, w_br_pool, w_br_sb, w_br_gla)


def _resproj_kernel(a_ref, w_ref, x_ref, ga_ref, g_ref, sc_ref, sh_ref, o_ref, h_ref):
    for r in range(a_ref.shape[0] // ROW_CHUNK):
        rows = slice(r * ROW_CHUNK, (r + 1) * ROW_CHUNK)
        y = jnp.dot(a_ref[rows, :], w_ref[...], preferred_element_type=F32)
        x_new = x_ref[rows, :] + ga_ref[0] * y
        o_ref[rows, :] = x_new
        h_ref[rows, :] = _modulated_norm(x_new, g_ref[...], sc_ref[0], sh_ref[0]).astype(BF16)


def _resproj(a, w, layer, x2, gate, gain, scale, shift, *, seq, tm):
    m, kdim = a.shape
    d = w.shape[2]
    tpb = seq // tm
    row = lambda width: pl.BlockSpec((tm, width), lambda i: (i, 0))
    mod = pl.BlockSpec((1, 1, d), lambda i: (i // tpb, 0, 0))
    return pl.pallas_call(
        _resproj_kernel,
        grid=(m // tm,),
        in_specs=[row(kdim), pl.BlockSpec((None, kdim, d), lambda i: (layer, 0, 0)), row(d), mod,
                  pl.BlockSpec((1, d), lambda i: (0, 0)), mod, mod],
        out_specs=[row(d), row(d)],
        out_shape=[jax.ShapeDtypeStruct((m, d), F32), jax.ShapeDtypeStruct((m, d), BF16)],
        compiler_params=_cparams(("arbitrary",)),
        name="residual_proj",
    )(a, w, x2, gate, gain, scale, shift)


def _swiglu_kernel(*refs, emit_next, x_chunks):
    if emit_next:
        h_ref, x_ref, ga_ref, wg_ref, wu_ref, wd_ref, g_ref, sc_ref, sh_ref, o_ref, hn_ref = refs
    else:
        h_ref, x_ref, ga_ref, wg_ref, wu_ref, wd_ref, o_ref = refs
    f = pl.program_id(1)
    last = pl.num_programs(1) - 1
    xw = x_ref.shape[1]

    def gated_partial(rows):
        h = h_ref[rows, :]
        gate = jnp.dot(h, wg_ref[...], preferred_element_type=F32)
        up = jnp.dot(h, wu_ref[...], preferred_element_type=F32)
        act = (_silu(gate) * up).astype(BF16)
        return ga_ref[0] * jnp.dot(act, wd_ref[...], preferred_element_type=F32)

    row_chunks = [slice(r * ROW_CHUNK, (r + 1) * ROW_CHUNK) for r in range(h_ref.shape[0] // ROW_CHUNK)]

    @pl.when(f == 0)
    def _():
        for rows in row_chunks:
            o_ref[rows, :] = gated_partial(rows)

    @pl.when(jnp.logical_and(f > 0, f < last))
    def _():
        for rows in row_chunks:
            o_ref[rows, :] += gated_partial(rows)

    for c in range(x_chunks):
        @pl.when(f == c)
        def _():
            o_ref[:, c * xw:(c + 1) * xw] += x_ref[...]

    @pl.when(f == last)
    def _():
        for rows in row_chunks:
            x_new = o_ref[rows, :] + gated_partial(rows)
            o_ref[rows, :] = x_new
            if emit_next:
                hn_ref[rows, :] = _modulated_norm(x_new, g_ref[...], sc_ref[0], sh_ref[0]).astype(BF16)


def _swiglu(h, x2, gate, w_gate, w_up, w_down, layer, next_norm, *, seq, tm, tf):
    m, d = x2.shape
    dff = w_gate.shape[2]
    tpb = seq // tm
    n_steps = dff // tf
    x_chunks = 8
    assert x_chunks < n_steps and d % (x_chunks * GROUP_DIM) == 0
    emit_next = next_norm is not None
    row = pl.BlockSpec((tm, d), lambda i, f: (i, 0))
    mod = pl.BlockSpec((1, 1, d), lambda i, f: (i // tpb, 0, 0))
    in_specs = [row, pl.BlockSpec((tm, d // x_chunks), lambda i, f: (i, jnp.minimum(f, x_chunks - 1))), mod,
                pl.BlockSpec((None, d, tf), lambda i, f: (layer, 0, f)),
                pl.BlockSpec((None, d, tf), lambda i, f: (layer, 0, f)),
                pl.BlockSpec((None, tf, d), lambda i, f: (layer, f, 0))]
    args = [h, x2, gate, w_gate, w_up, w_down]
    out_specs = [row]
    out_shape = [jax.ShapeDtypeStruct((m, d), F32)]
    if emit_next:
        in_specs += [pl.BlockSpec((1, d), lambda i, f: (0, 0)), mod, mod]
        args += list(next_norm)
        out_specs.append(row)
        out_shape.append(jax.ShapeDtypeStruct((m, d), BF16))
    outs = pl.pallas_call(
        functools.partial(_swiglu_kernel, emit_next=emit_next, x_chunks=x_chunks),
        grid=(m // tm, n_steps),
        in_specs=in_specs,
        out_specs=out_specs,
        out_shape=out_shape,
        compiler_params=_cparams(("arbitrary", "arbitrary")),
        name="swiglu",
    )(*args)
    return (outs[0], outs[1]) if emit_next else (outs[0], None)


class _Tiles(NamedTuple):
    rows: int
    rows_f32: int
    pool_rows: int
    merge_cols: int
    ff_cols: int
    gla_chunks: int
    sb_heads: int


def _tile(n, pref):
    t = min(n, pref)
    assert n % t == 0, (n, t)
    return t


def _tiles(seq):
    return _Tiles(rows=_tile(seq, 1024), rows_f32=_tile(seq, 512), pool_rows=_tile(seq, 2048),
                  merge_cols=512, ff_cols=512,
                  gla_chunks=_tile(seq // GLA_CHUNK, 8), sb_heads=4)


def _mod_vectors(mod_l):
    d = D_MODEL
    return [mod_l[:, None, k * d:(k + 1) * d] for k in range(6)]


class _Weights(NamedTuple):
    f32cols: jax.Array
    sb: jax.Array
    gates: jax.Array
    br_pool: jax.Array
    br_sb: jax.Array
    br_gla: jax.Array
    out: jax.Array
    ff_gate: jax.Array
    ff_up: jax.Array
    ff_down: jax.Array


def _layer(x2, h1, mod_l, next_norm, batch, seq, layer, w, w_pool, pool_scale, sb_q_gain, sb_k_gain,
           gla_w_a2, gla_b_a2, gla_out_gain, g_norm2):
    d = D_MODEL
    t = _tiles(seq)
    _, _, ga1, sh2, sc2, ga2 = _mod_vectors(mod_l)

    proj = _proj(h1, w.f32cols, layer, jnp.ones((1, F32_COLS), F32), tm=t.rows_f32, norm_cols=0, out_dtype=F32)
    q_scale = LOG2E / math.sqrt(GROUP_DIM)
    colgain = jnp.concatenate([jnp.tile(sb_q_gain * q_scale, SB_HEADS), jnp.tile(sb_k_gain, SB_HEADS),
                               jnp.ones((SB_WIDTH,), F32)]).reshape(1, 3 * SB_WIDTH)
    qkv = _proj(h1, w.sb, layer, colgain, tm=t.rows_f32, norm_cols=2 * SB_WIDTH, out_dtype=BF16)

    proj3 = proj.reshape(batch, seq, F32_COLS)
    y_pool = _pool(proj3, w_pool.astype(BF16), pool_scale, ts=t.pool_rows)
    y_sb = _stick_breaking(qkv.reshape(batch, seq, 3 * SB_WIDTH), tq=t.rows, tk=GROUP_DIM,
                           heads_per_step=t.sb_heads)
    w_a2p = jnp.pad(gla_w_a2, ((0, GROUP_DIM - GLA_RANK), (0, 0))).astype(BF16)
    y_gla = _gla(proj3, w_a2p, gla_b_a2.reshape(1, GLA_WIDTH), gla_out_gain.reshape(1, GROUP_DIM),
                 chunks_per_step=t.gla_chunks)

    m = batch * seq
    merged = _merge(h1, y_pool.reshape(m, POOL_WIDTH), y_sb.reshape(m, SB_WIDTH), y_gla.reshape(m, GLA_WIDTH),
                    w.gates, w.br_pool, w.br_sb, w.br_gla, layer, tm=t.rows, tn=t.merge_cols)
    x2, h2 = _resproj(merged, w.out, layer, x2, ga1, g_norm2.reshape(1, d), sc2, sh2, seq=seq, tm=t.rows_f32)
    return _swiglu(h2, x2, ga2, w.ff_gate, w.ff_up, w.ff_down, layer, next_norm, seq=seq, tm=t.rows,
                   tf=t.ff_cols)


def kernel(x, c, w_ada, b_ada, g_norm1, w_in, w_pool, pool_scale, sb_q_gain, sb_k_gain, gla_w_a2, gla_b_a2, gla_out_gain, w_br_pool, w_br_sb, w_br_gla, w_out, g_norm2, w_ff_gate, w_ff_up, w_ff_down):
    batch, seq, d = x.shape
    depth = w_ada.shape[0]
    c_pad = jnp.pad(c, ((0, 8 - batch % 8 if batch % 8 else 0), (0, 0)))
    mod = _ada(c_pad, w_ada, b_ada)[:, :batch]
    x2 = x.reshape(batch * seq, d)
    w = _Weights(*_split_w_in(w_in.astype(BF16)), *[a.astype(BF16) for a in (w_br_pool, w_br_sb, w_br_gla, w_out,
                                                               w_ff_gate, w_ff_up, w_ff_down)])

    def first_norm(l):
        sh1, sc1 = _mod_vectors(mod[l])[:2]
        return g_norm1[l].reshape(1, d), sc1, sh1

    h1 = _norm(x2, *first_norm(0), seq=seq, tm=_tiles(seq).rows_f32)
    for l in range(depth):
        next_norm = first_norm(l + 1) if l + 1 < depth else None
        x2, h1 = _layer(x2, h1, mod[l], next_norm, batch, seq, l, w, w_pool[l], pool_scale[l],
                        sb_q_gain[l], sb_k_gain[l], gla_w_a2[l], gla_b_a2[l], gla_out_gain[l], g_norm2[l])
    return x2.reshape(batch, seq, d)
```

```python
import functools
import math
from typing import NamedTuple

import numpy as np
import jax
import jax.numpy as jnp
from jax import lax
from jax.experimental import pallas as pl
from jax.experimental.pallas import tpu as pltpu

F32 = jnp.float32
BF16 = jnp.bfloat16

D_MODEL = 2048
POOL_WINDOWS = (2, 4, 8, 16)
POOL_GROUPS = 4
GROUP_DIM = 128
POOL_WIDTH = POOL_GROUPS * GROUP_DIM
SB_HEADS = 8
SB_WIDTH = SB_HEADS * GROUP_DIM
GLA_HEADS = 4
GLA_WIDTH = GLA_HEADS * GROUP_DIM
GLA_RANK = 16
GLA_TAU = 16.0
GLA_CHUNK = 128
GLA_LEVELS = GLA_CHUNK.bit_length() - 1
assert GLA_CHUNK == 1 << GLA_LEVELS
N_BRANCH = 3
RMS_EPS = 1e-6
LOG2E = 1.4426950408889634
POOL_HALO = 16
SB_DEAD_BITS = 160.0
SB_MASKED_BITS = 1.0e4
SB_STATIC_LAGS = 3

VMEM_LIMIT = 56 * 1024 * 1024
MXU_WIDTH = 256
ROW_CHUNK = 256


def _cparams(sem):
    return pltpu.CompilerParams(dimension_semantics=sem, vmem_limit_bytes=VMEM_LIMIT)


def _nt_dot(a, b):
    return lax.dot_general(a, b, (((1,), (1,)), ((), ())), preferred_element_type=F32)


def _silu(x):
    return x * jax.nn.sigmoid(x)


def _ada_kernel(c_ref, w_ref, b_ref, o_ref):
    c = c_ref[...]
    a = _silu(c).astype(BF16)
    o_ref[0] = jnp.dot(a, w_ref[0].astype(BF16), preferred_element_type=F32) + b_ref[0]


def _ada(c_pad, w_ada, b_ada, tn=1024):
    depth, d, n = w_ada.shape
    rows = c_pad.shape[0]
    return pl.pallas_call(
        _ada_kernel,
        grid=(depth, n // tn),
        in_specs=[
            pl.BlockSpec((rows, d), lambda l, j: (0, 0)),
            pl.BlockSpec((1, d, tn), lambda l, j: (l, 0, j)),
            pl.BlockSpec((1, 1, tn), lambda l, j: (l, 0, j)),
        ],
        out_specs=pl.BlockSpec((1, rows, tn), lambda l, j: (l, 0, j)),
        out_shape=jax.ShapeDtypeStruct((depth, rows, n), F32),
        compiler_params=_cparams(("arbitrary", "arbitrary")),
        name="ada_modulation",
    )(c_pad, w_ada, b_ada.reshape(depth, 1, n))


W_IN_SB = POOL_WIDTH
W_IN_GLA = W_IN_SB + 3 * SB_WIDTH
W_IN_A = W_IN_GLA + 4 * GLA_WIDTH
W_IN_GATE = W_IN_A + GLA_RANK
W_IN_COLS = W_IN_GATE + N_BRANCH * D_MODEL
A_LOW_COL = POOL_WIDTH + 4 * GLA_WIDTH
F32_COLS = A_LOW_COL + MXU_WIDTH


def _split_w_in_kernel(w_ref, f_ref, sb_ref, g_ref):
    f_ref[0, :, :POOL_WIDTH] = w_ref[0, :, :W_IN_SB]
    f_ref[0, :, POOL_WIDTH:A_LOW_COL] = w_ref[0, :, W_IN_GLA:W_IN_A]
    f_ref[0, :, A_LOW_COL:] = jnp.zeros((f_ref.shape[1], MXU_WIDTH), BF16)
    f_ref[0, :, A_LOW_COL:A_LOW_COL + GLA_RANK] = w_ref[0, :, W_IN_A:W_IN_GATE]
    sb_ref[0] = w_ref[0, :, W_IN_SB:W_IN_GLA]
    g_ref[0] = w_ref[0, :, W_IN_GATE:]


def _split_w_in(w_in, tr=512):
    depth, d, n = w_in.shape
    assert n == W_IN_COLS and w_in.dtype == BF16
    widths = (F32_COLS, 3 * SB_WIDTH, N_BRANCH * D_MODEL)
    return pl.pallas_call(
        _split_w_in_kernel,
        grid=(depth, d // tr),
        in_specs=[pl.BlockSpec((1, tr, n), lambda l, i: (l, i, 0))],
        out_specs=[pl.BlockSpec((1, tr, w), lambda l, i: (l, i, 0)) for w in widths],
        out_shape=[jax.ShapeDtypeStruct((depth, d, w), BF16) for w in widths],
        compiler_params=_cparams(("arbitrary", "arbitrary")),
        name="split_w_in",
    )(w_in)


def _modulated_norm(x, gain, scale, shift):
    ms = jnp.mean(x * x, axis=-1, keepdims=True)
    y = x * lax.rsqrt(ms + RMS_EPS) * gain
    return y * (1.0 + scale) + shift


def _group_rmsnorm(acc, colgain):
    out = []
    for c in range(acc.shape[1] // GROUP_DIM):
        sl = slice(c * GROUP_DIM, (c + 1) * GROUP_DIM)
        blk = acc[:, sl]
        ms = jnp.mean(blk * blk, axis=-1, keepdims=True)
        out.append(blk * lax.rsqrt(ms + RMS_EPS) * colgain[:, sl])
    return jnp.concatenate(out, axis=1)


def _norm_kernel(x_ref, g_ref, sc_ref, sh_ref, h_ref):
    h_ref[...] = _modulated_norm(x_ref[...], g_ref[...], sc_ref[0], sh_ref[0]).astype(BF16)


def _norm(x2, gain, scale, shift, *, seq, tm):
    m, d = x2.shape
    tpb = seq // tm
    mod = pl.BlockSpec((1, 1, d), lambda i: (i // tpb, 0, 0))
    return pl.pallas_call(
        _norm_kernel,
        grid=(m // tm,),
        in_specs=[pl.BlockSpec((tm, d), lambda i: (i, 0)), pl.BlockSpec((1, d), lambda i: (0, 0)), mod, mod],
        out_specs=pl.BlockSpec((tm, d), lambda i: (i, 0)),
        out_shape=jax.ShapeDtypeStruct((m, d), BF16),
        compiler_params=_cparams(("arbitrary",)),
        name="modulated_norm",
    )(x2, gain, scale, shift)


def _proj_kernel(h_ref, w_ref, cg_ref, o_ref, *, norm_cols):
    h = h_ref[...]
    for c in range(o_ref.shape[1] // MXU_WIDTH):
        cols = slice(c * MXU_WIDTH, (c + 1) * MXU_WIDTH)
        acc = jnp.dot(h, w_ref[:, cols], preferred_element_type=F32)
        if c * MXU_WIDTH < norm_cols:
            acc = _group_rmsnorm(acc, cg_ref[:, cols])
        o_ref[:, cols] = acc.astype(o_ref.dtype)


def _proj(h, w, layer, colgain, *, tm, norm_cols, out_dtype):
    m, d = h.shape
    n = w.shape[2]
    assert n % MXU_WIDTH == 0 and norm_cols % MXU_WIDTH == 0
    return pl.pallas_call(
        functools.partial(_proj_kernel, norm_cols=norm_cols),
        grid=(m // tm,),
        in_specs=[
            pl.BlockSpec((tm, d), lambda i: (i, 0)),
            pl.BlockSpec((None, d, n), lambda i: (layer, 0, 0), pipeline_mode=pl.Buffered(1)),
            pl.BlockSpec((1, n), lambda i: (0, 0)),
        ],
        out_specs=pl.BlockSpec((tm, n), lambda i: (i, 0)),
        out_shape=jax.ShapeDtypeStruct((m, n), out_dtype),
        compiler_params=_cparams(("arbitrary",)),
        name="in_proj",
    )(h, w, colgain)


def _pool_kernel(u_ref, w_ref, ps_ref, o_ref, ext_ref, *, ts):
    i = pl.program_id(1)

    @pl.when(i == 0)
    def _():
        ext_ref[0:POOL_HALO, :] = jnp.zeros((POOL_HALO, POOL_WIDTH), F32)

    @pl.when(i > 0)
    def _():
        ext_ref[0:POOL_HALO, :] = ext_ref[ts:ts + POOL_HALO, :]

    ext_ref[POOL_HALO:, :] = u_ref[0]
    pos1 = (i * ts + 1 + lax.broadcasted_iota(jnp.int32, (ts, GROUP_DIM), 0)).astype(F32)
    for g, w in enumerate(POOL_WINDOWS):
        cols = slice(g * GROUP_DIM, (g + 1) * GROUP_DIM)
        u = ext_ref[POOL_HALO:, cols]
        win = u
        for k in range(1, w):
            win = win + ext_ref[POOL_HALO - k:POOL_HALO - k + ts, cols]
        pooled = win / jnp.minimum(pos1, float(w)) - u
        y = jnp.dot(pooled.astype(BF16), w_ref[g], preferred_element_type=F32) * ps_ref[g]
        o_ref[0, :, cols] = y.astype(o_ref.dtype)


def _pool(proj3, w_pool, pool_scale, ts):
    b, s, _ = proj3.shape
    kern = functools.partial(_pool_kernel, ts=ts)
    return pl.pallas_call(
        kern,
        grid=(b, s // ts),
        in_specs=[
            pl.BlockSpec((1, ts, POOL_WIDTH), lambda bi, i: (bi, i, 0)),
            pl.BlockSpec((POOL_GROUPS, GROUP_DIM, GROUP_DIM), lambda bi, i: (0, 0, 0)),
            pl.BlockSpec((POOL_GROUPS, 1, GROUP_DIM), lambda bi, i: (0, 0, 0)),
        ],
        out_specs=pl.BlockSpec((1, ts, POOL_WIDTH), lambda bi, i: (bi, i, 0)),
        out_shape=jax.ShapeDtypeStruct((b, s, POOL_WIDTH), BF16),
        scratch_shapes=[pltpu.VMEM((ts + POOL_HALO, POOL_WIDTH), F32)],
        compiler_params=_cparams(("arbitrary", "arbitrary")),
        name="pool_mixer",
    )(proj3, w_pool, pool_scale.reshape(POOL_GROUPS, 1, GROUP_DIM))


def _sb_cumsum_matrix(tk):
    j = np.arange(tk)[:, None]
    s = np.arange(tk)[None, :]
    one = np.concatenate([(j >= s).astype(np.float32), np.ones((tk, tk), np.float32)], axis=1)
    return jnp.asarray(np.concatenate([one, one], axis=0), dtype=BF16)


def _sb_kernel(q_ref, k_ref, v_ref, w2_ref, o_ref, acc_ref, run_ref, *, tq, tk):
    i = pl.program_id(2)
    w2 = w2_ref[...]
    n_sub = tq // tk
    heads = [slice(hh * GROUP_DIM, (hh + 1) * GROUP_DIM) for hh in range(q_ref.shape[2] // GROUP_DIM)]
    acc_ref[...] = jnp.zeros(acc_ref.shape, F32)
    run_ref[...] = jnp.zeros(run_ref.shape, F32)
    row = lax.broadcasted_iota(jnp.int32, (tq, tk), 0)
    col = lax.broadcasted_iota(jnp.int32, (tq, tk), 1)
    diag_mask = col < (row & (tk - 1))
    sub = lambda r: slice(r * tk, (r + 1) * tk)

    def key_blocks(lag):
        blocks = []
        for r in range(n_sub):
            jb = i * n_sub + r - lag
            start = pl.multiple_of(jnp.maximum(jb, 0) * tk, tk)
            blocks.append((start, jnp.where(jb >= 0, 0.0, SB_MASKED_BITS)))
        return blocks

    def scores(cols, blocks, diagonal):
        z = jnp.concatenate([_nt_dot(q_ref[0, sub(r), cols], k_ref[0, pl.ds(start, tk), cols])
                             for r, (start, _) in enumerate(blocks)], axis=0)
        neg_abs = lax.bitcast_convert_type(
            lax.bitcast_convert_type(z, jnp.uint32) | jnp.uint32(0x80000000), F32)
        neg_log_keep = jnp.maximum(z, 0.0) + jnp.log(1.0 + jnp.exp2(neg_abs)) * LOG2E
        if diagonal:
            neg_log_keep = jnp.where(diag_mask, neg_log_keep, 0.0)
            z = jnp.where(diag_mask, z, -SB_MASKED_BITS)
        hi = lax.bitcast_convert_type(
            lax.bitcast_convert_type(neg_log_keep, jnp.uint32) & jnp.uint32(0xFFFF0000), F32)
        lo = neg_log_keep - hi
        c = jnp.dot(jnp.concatenate([hi.astype(BF16), lo.astype(BF16)], axis=1), w2, preferred_element_type=F32)
        return z, c

    def accumulate(hh, cols, blocks, z, c, diagonal):
        run = run_ref[hh]
        if not diagonal:
            run = run + jnp.concatenate([jnp.full((tk, tk), pen, F32) for _, pen in blocks], axis=0)
        a = jnp.exp2(z - c[:, :tk] - run).astype(BF16)
        for r, (start, _) in enumerate(blocks):
            acc_ref[hh, sub(r), :] += jnp.dot(a[sub(r), :], v_ref[0, pl.ds(start, tk), cols],
                                              preferred_element_type=F32)
        run_ref[hh] = run + c[:, tk:]

    lags = [key_blocks(lag) for lag in range(SB_STATIC_LAGS)]
    staged = [[scores(cols, blocks, lag == 0) for lag, blocks in enumerate(lags)] for cols in heads]
    for lag, blocks in enumerate(lags):
        for hh, cols in enumerate(heads):
            accumulate(hh, cols, blocks, *staged[hh][lag], lag == 0)

    for hh, cols in enumerate(heads):
        def cond(carry):
            lag, min_run = carry
            return jnp.logical_and(lag <= i * n_sub + n_sub - 1, min_run < SB_DEAD_BITS)

        def body(carry, hh=hh, cols=cols):
            lag, _ = carry
            blocks = key_blocks(lag)
            z, c = scores(cols, blocks, False)
            accumulate(hh, cols, blocks, z, c, False)
            return lag + 1, jnp.min(run_ref[hh])

        lax.while_loop(cond, body, (jnp.int32(SB_STATIC_LAGS), jnp.min(run_ref[hh])))
        o_ref[0, :, cols] = acc_ref[hh].astype(o_ref.dtype)


def _stick_breaking(qkv, tq, tk, heads_per_step):
    b, s, _ = qkv.shape
    hw = heads_per_step * GROUP_DIM
    groups = SB_HEADS // heads_per_step
    kern = functools.partial(_sb_kernel, tq=tq, tk=tk)
    return pl.pallas_call(
        kern,
        grid=(b, groups, s // tq),
        in_specs=[
            pl.BlockSpec((1, tq, hw), lambda bi, h, i: (bi, i, h)),
            pl.BlockSpec((1, s, hw), lambda bi, h, i: (bi, 0, groups + h)),
            pl.BlockSpec((1, s, hw), lambda bi, h, i: (bi, 0, 2 * groups + h)),
            pl.BlockSpec((2 * tk, 2 * tk), lambda bi, h, i: (0, 0)),
        ],
        out_specs=pl.BlockSpec((1, tq, hw), lambda bi, h, i: (bi, i, h)),
        out_shape=jax.ShapeDtypeStruct((b, s, SB_WIDTH), BF16),
        scratch_shapes=[pltpu.VMEM((heads_per_step, tq, GROUP_DIM), F32),
                        pltpu.VMEM((heads_per_step, tq, tk), F32)],
        compiler_params=_cparams(("arbitrary", "arbitrary", "arbitrary")),
        name="stick_breaking",
    )(qkv, qkv, qkv, _sb_cumsum_matrix(tk))


def _gla_constants():
    c = GLA_CHUNK
    t = np.arange(c)[:, None]
    j = np.arange(c)[None, :]
    dst = np.concatenate([j <= t, np.ones((8, c), bool)], axis=0).astype(np.float32)
    s = j
    masks = []
    for l in range(GLA_LEVELS):
        masks.append(((t ^ s) >> l == 1) & (t > s))
    masks.append(t == s)
    return jnp.asarray(dst, dtype=BF16), jnp.asarray(np.stack(masks).astype(np.float32))


def _split3(x):
    x1 = x.astype(BF16)
    r = x - x1.astype(F32)
    x2 = r.astype(BF16)
    x3 = (r - x2.astype(F32)).astype(BF16)
    return x1, x2, x3


def _gla_kernel(q_ref, k_ref, v_ref, r_ref, a_ref, wa_ref, ba_ref, og_ref, dst_ref, msk_ref,
                o_ref, st_ref):
    @pl.when(pl.program_id(1) == 0)
    def _():
        st_ref[...] = jnp.zeros(st_ref.shape, F32)

    for cc in range(q_ref.shape[1] // GLA_CHUNK):
        _gla_chunk(slice(cc * GLA_CHUNK, (cc + 1) * GLA_CHUNK), q_ref, k_ref, v_ref, r_ref, a_ref,
                   wa_ref, ba_ref, og_ref, dst_ref, msk_ref, o_ref, st_ref)


def _gla_chunk(rows, q_ref, k_ref, v_ref, r_ref, a_ref, wa_ref, ba_ref, og_ref, dst_ref, msk_ref,
               o_ref, st_ref):
    c = GLA_CHUNK
    x = jnp.dot(a_ref[0, rows, :].astype(BF16), wa_ref[...], preferred_element_type=F32) + ba_ref[...]
    log_sig = jnp.minimum(x, 0.0) - jnp.log(1.0 + jnp.exp(-jnp.abs(x)))
    g = log_sig * (LOG2E / GLA_TAU)
    dst = dst_ref[...]
    g1, g2, g3 = _split3(g)
    sums = (jnp.dot(dst, g1, preferred_element_type=F32)
            + jnp.dot(dst, g2, preferred_element_type=F32)
            + jnp.dot(dst, g3, preferred_element_type=F32))
    prefix = sums[:c]
    total = sums[c:c + 1]
    e_in = prefix
    e_out = total - prefix

    row = lax.broadcasted_iota(jnp.int32, g.shape, 0)
    pos = row & 3
    g_prev = pltpu.roll(g, 1, 0)
    g_next = pltpu.roll(g, c - 1, 0)
    e_lvl = [jnp.where((row & 1) == 1, g, 0.0),
             jnp.where(pos == 0, g_next, jnp.where(pos == 1, 0.0, jnp.where(pos == 2, g, g_prev + g)))]
    for l in range(2, GLA_LEVELS):
        m = 1 << l
        blocks = prefix.reshape(c // (2 * m), 2 * m, GLA_WIDTH)
        e_lvl.append((-jnp.abs(blocks - blocks[:, m - 1:m, :])).reshape(c, GLA_WIDTH))

    for h in range(GLA_HEADS):
        cols = slice(h * GROUP_DIM, (h + 1) * GROUP_DIM)
        q = q_ref[0, rows, cols] * (GROUP_DIM ** -0.5)
        k = k_ref[0, rows, cols]
        v = v_ref[0, rows, cols].astype(BF16)
        scores = msk_ref[GLA_LEVELS] * _nt_dot(q.astype(BF16), k.astype(BF16))
        for l in range(GLA_LEVELS):
            xl = jnp.exp2(e_lvl[l][:, cols])
            scores = scores + msk_ref[l] * _nt_dot((q * xl).astype(BF16), (k * xl).astype(BF16))
        o = jnp.dot(scores.astype(BF16), v, preferred_element_type=F32)
        x_in = jnp.exp2(e_in[:, cols])
        st = st_ref[h]
        o = o + _nt_dot((q * x_in).astype(BF16), st.astype(BF16))
        x_out = jnp.exp2(e_out[:, cols])
        x_all = jnp.exp2(total[:, cols])
        vt = v_ref[0, rows, cols].T.astype(BF16)
        st_ref[h] = st * x_all + jnp.dot(vt, (k * x_out).astype(BF16), preferred_element_type=F32)
        ms = jnp.mean(o * o, axis=-1, keepdims=True)
        o = o * lax.rsqrt(ms + RMS_EPS) * og_ref[...]
        o_ref[0, rows, cols] = (o * _silu(r_ref[0, rows, cols])).astype(o_ref.dtype)


def _gla(proj3, w_a2p, b_a2, out_gain, chunks_per_step):
    b, s, _ = proj3.shape
    c = GLA_CHUNK * chunks_per_step
    dst, masks = _gla_constants()
    wide = lambda blk: pl.BlockSpec((1, c, GLA_WIDTH), lambda bi, ci: (bi, ci, blk))
    const2 = lambda shape: pl.BlockSpec(shape, lambda bi, ci: (0, 0))
    a_blk = (POOL_WIDTH + 4 * GLA_WIDTH) // GROUP_DIM
    return pl.pallas_call(
        _gla_kernel,
        grid=(b, s // c),
        in_specs=[
            wide(1), wide(2), wide(3), wide(4),
            pl.BlockSpec((1, c, GROUP_DIM), lambda bi, ci: (bi, ci, a_blk)),
            const2((GROUP_DIM, GLA_WIDTH)),
            const2((1, GLA_WIDTH)),
            const2((1, GROUP_DIM)),
            const2(dst.shape),
            pl.BlockSpec(masks.shape, lambda bi, ci: (0, 0, 0)),
        ],
        out_specs=pl.BlockSpec((1, c, GLA_WIDTH), lambda bi, ci: (bi, ci, 0)),
        out_shape=jax.ShapeDtypeStruct((b, s, GLA_WIDTH), BF16),
        scratch_shapes=[pltpu.VMEM((GLA_HEADS, GROUP_DIM, GROUP_DIM), F32)],
        compiler_params=_cparams(("arbitrary", "arbitrary")),
        name="gla",
    )(proj3, proj3, proj3, proj3, proj3, w_a2p, b_a2, out_gain, dst, masks)


def _merge_kernel(h_ref, yp_ref, ys_ref, yg_ref, wgate_ref, wp_ref, ws_ref, wg_ref, o_ref):
    h = h_ref[...]
    d = o_ref.shape[1]
    branches = ((yp_ref[...], wp_ref), (ys_ref[...], ws_ref), (yg_ref[...], wg_ref))
    for c in range(d // MXU_WIDTH):
        cols = slice(c * MXU_WIDTH, (c + 1) * MXU_WIDTH)
        merged = None
        for br, (y, w_ref) in enumerate(branches):
            gate_cols = slice(br * d + c * MXU_WIDTH, br * d + (c + 1) * MXU_WIDTH)
            gate = jax.nn.sigmoid(jnp.dot(h, wgate_ref[:, gate_cols], preferred_element_type=F32))
            term = gate * jnp.dot(y, w_ref[:, cols], preferred_element_type=F32)
            merged = term if merged is None else merged + term
        o_ref[:, cols] = merged.astype(o_ref.dtype)


def _merge(h, y_pool, y_sb, y_gla, w_gates, w_br_pool, w_br_sb, w_br_gla, layer, *, tm):
    m, d = h.shape
    row = lambda width: pl.BlockSpec((tm, width), lambda i: (i, 0))
    resident = lambda w: pl.BlockSpec((None,) + w.shape[1:], lambda i: (layer, 0, 0),
                                      pipeline_mode=pl.Buffered(1))
    return pl.pallas_call(
        _merge_kernel,
        grid=(m // tm,),
        in_specs=[
            row(d), row(POOL_WIDTH), row(SB_WIDTH), row(GLA_WIDTH),
            resident(w_gates), resident(w_br_pool), resident(w_br_sb), resident(w_br_gla),
        ],
        out_specs=row(d),
        out_shape=jax.ShapeDtypeStruct((m, d), BF16),
        compiler_params=_cparams(("arbitrary",)),
        name="gated_merge",
    )(h, y_pool, y_sb, y_gla, w_gates, w_br_pool, w_br_sb, w_br_gla)


def _resproj_kernel(a_ref, w_ref, x_ref, ga_ref, g_ref, sc_ref, sh_ref, o_ref, h_ref):
    for r in range(a_ref.shape[0] // ROW_CHUNK):
        rows = slice(r * ROW_CHUNK, (r + 1) * ROW_CHUNK)
        y = jnp.dot(a_ref[rows, :], w_ref[...], preferred_element_type=F32)
        x_new = x_ref[rows, :] + ga_ref[0] * y
        o_ref[rows, :] = x_new
        h_ref[rows, :] = _modulated_norm(x_new, g_ref[...], sc_ref[0], sh_ref[0]).astype(BF16)


def _resproj(a, w, layer, x2, gate, gain, scale, shift, *, seq, tm):
    m, kdim = a.shape
    d = w.shape[2]
    tpb = seq // tm
    row = lambda width: pl.BlockSpec((tm, width), lambda i: (i, 0))
    mod = pl.BlockSpec((1, 1, d), lambda i: (i // tpb, 0, 0))
    return pl.pallas_call(
        _resproj_kernel,
        grid=(m // tm,),
        in_specs=[row(kdim), pl.BlockSpec((None, kdim, d), lambda i: (layer, 0, 0)), row(d), mod,
                  pl.BlockSpec((1, d), lambda i: (0, 0)), mod, mod],
        out_specs=[row(d), row(d)],
        out_shape=[jax.ShapeDtypeStruct((m, d), F32), jax.ShapeDtypeStruct((m, d), BF16)],
        compiler_params=_cparams(("arbitrary",)),
        name="residual_proj",
    )(a, w, x2, gate, gain, scale, shift)


def _swiglu_kernel(*refs, emit_next, x_chunks):
    if emit_next:
        h_ref, x_ref, ga_ref, wg_ref, wu_ref, wd_ref, g_ref, sc_ref, sh_ref, o_ref, hn_ref = refs
    else:
        h_ref, x_ref, ga_ref, wg_ref, wu_ref, wd_ref, o_ref = refs
    f = pl.program_id(1)
    last = pl.num_programs(1) - 1
    xw = x_ref.shape[1]

    def gated_partial(rows):
        h = h_ref[rows, :]
        gate = jnp.dot(h, wg_ref[...], preferred_element_type=F32)
        up = jnp.dot(h, wu_ref[...], preferred_element_type=F32)
        act = (_silu(gate) * up).astype(BF16)
        return ga_ref[0] * jnp.dot(act, wd_ref[...], preferred_element_type=F32)

    row_chunks = [slice(r * ROW_CHUNK, (r + 1) * ROW_CHUNK) for r in range(h_ref.shape[0] // ROW_CHUNK)]

    @pl.when(f == 0)
    def _():
        for rows in row_chunks:
            o_ref[rows, :] = gated_partial(rows)

    @pl.when(jnp.logical_and(f > 0, f < last))
    def _():
        for rows in row_chunks:
            o_ref[rows, :] += gated_partial(rows)

    for c in range(x_chunks):
        @pl.when(f == c)
        def _():
            o_ref[:, c * xw:(c + 1) * xw] += x_ref[...]

    @pl.when(f == last)
    def _():
        for rows in row_chunks:
            x_new = o_ref[rows, :] + gated_partial(rows)
            o_ref[rows, :] = x_new
            if emit_next:
                hn_ref[rows, :] = _modulated_norm(x_new, g_ref[...], sc_ref[0], sh_ref[0]).astype(BF16)


def _swiglu(h, x2, gate, w_gate, w_up, w_down, layer, next_norm, *, seq, tm, tf):
    m, d = x2.shape
    dff = w_gate.shape[2]
    tpb = seq // tm
    n_steps = dff // tf
    x_chunks = 8
    assert x_chunks < n_steps and d % (x_chunks * GROUP_DIM) == 0
    emit_next = next_norm is not None
    row = pl.BlockSpec((tm, d), lambda i, f: (i, 0))
    mod = pl.BlockSpec((1, 1, d), lambda i, f: (i // tpb, 0, 0))
    in_specs = [row, pl.BlockSpec((tm, d // x_chunks), lambda i, f: (i, jnp.minimum(f, x_chunks - 1))), mod,
                pl.BlockSpec((None, d, tf), lambda i, f: (layer, 0, f)),
                pl.BlockSpec((None, d, tf), lambda i, f: (layer, 0, f)),
                pl.BlockSpec((None, tf, d), lambda i, f: (layer, f, 0))]
    args = [h, x2, gate, w_gate, w_up, w_down]
    out_specs = [row]
    out_shape = [jax.ShapeDtypeStruct((m, d), F32)]
    if emit_next:
        in_specs += [pl.BlockSpec((1, d), lambda i, f: (0, 0)), mod, mod]
        args += list(next_norm)
        out_specs.append(row)
        out_shape.append(jax.ShapeDtypeStruct((m, d), BF16))
    outs = pl.pallas_call(
        functools.partial(_swiglu_kernel, emit_next=emit_next, x_chunks=x_chunks),
        grid=(m // tm, n_steps),
        in_specs=in_specs,
        out_specs=out_specs,
        out_shape=out_shape,
        compiler_params=_cparams(("arbitrary", "arbitrary")),
        name="swiglu",
    )(*args)
    return (outs[0], outs[1]) if emit_next else (outs[0], None)


class _Tiles(NamedTuple):
    rows: int
    rows_f32: int
    pool_rows: int
    ff_cols: int
    gla_chunks: int
    sb_heads: int


def _tile(n, pref):
    t = min(n, pref)
    assert n % t == 0, (n, t)
    return t


def _tiles(seq):
    return _Tiles(rows=_tile(seq, 1024), rows_f32=_tile(seq, 512), pool_rows=_tile(seq, 2048),
                  ff_cols=512,
                  gla_chunks=_tile(seq // GLA_CHUNK, 8), sb_heads=4)


def _mod_vectors(mod_l):
    d = D_MODEL
    return [mod_l[:, None, k * d:(k + 1) * d] for k in range(6)]


class _Weights(NamedTuple):
    f32cols: jax.Array
    sb: jax.Array
    gates: jax.Array
    br_pool: jax.Array
    br_sb: jax.Array
    br_gla: jax.Array
    out: jax.Array
    ff_gate: jax.Array
    ff_up: jax.Array
    ff_down: jax.Array


def _layer(x2, h1, mod_l, next_norm, batch, seq, layer, w, w_pool, pool_scale, sb_q_gain, sb_k_gain,
           gla_w_a2, gla_b_a2, gla_out_gain, g_norm2):
    d = D_MODEL
    t = _tiles(seq)
    _, _, ga1, sh2, sc2, ga2 = _mod_vectors(mod_l)

    proj = _proj(h1, w.f32cols, layer, jnp.ones((1, F32_COLS), F32), tm=t.rows, norm_cols=0, out_dtype=F32)
    q_scale = LOG2E / math.sqrt(GROUP_DIM)
    colgain = jnp.concatenate([jnp.tile(sb_q_gain * q_scale, SB_HEADS), jnp.tile(sb_k_gain, SB_HEADS),
                               jnp.ones((SB_WIDTH,), F32)]).reshape(1, 3 * SB_WIDTH)
    qkv = _proj(h1, w.sb, layer, colgain, tm=t.rows, norm_cols=2 * SB_WIDTH, out_dtype=BF16)

    proj3 = proj.reshape(batch, seq, F32_COLS)
    y_pool = _pool(proj3, w_pool.astype(BF16), pool_scale, ts=t.pool_rows)
    y_sb = _stick_breaking(qkv.reshape(batch, seq, 3 * SB_WIDTH), tq=t.rows, tk=GROUP_DIM,
                           heads_per_step=t.sb_heads)
    w_a2p = jnp.pad(gla_w_a2, ((0, GROUP_DIM - GLA_RANK), (0, 0))).astype(BF16)
    y_gla = _gla(proj3, w_a2p, gla_b_a2.reshape(1, GLA_WIDTH), gla_out_gain.reshape(1, GROUP_DIM),
                 chunks_per_step=t.gla_chunks)

    m = batch * seq
    merged = _merge(h1, y_pool.reshape(m, POOL_WIDTH), y_sb.reshape(m, SB_WIDTH), y_gla.reshape(m, GLA_WIDTH),
                    w.gates, w.br_pool, w.br_sb, w.br_gla, layer, tm=t.rows_f32)
    x2, h2 = _resproj(merged, w.out, layer, x2, ga1, g_norm2.reshape(1, d), sc2, sh2, seq=seq, tm=t.rows_f32)
    return _swiglu(h2, x2, ga2, w.ff_gate, w.ff_up, w.ff_down, layer, next_norm, seq=seq, tm=t.rows,
                   tf=t.ff_cols)


def kernel(x, c, w_ada, b_ada, g_norm1, w_in, w_pool, pool_scale, sb_q_gain, sb_k_gain, gla_w_a2, gla_b_a2, gla_out_gain, w_br_pool, w_br_sb, w_br_gla, w_out, g_norm2, w_ff_gate, w_ff_up, w_ff_down):
    batch, seq, d = x.shape
    depth = w_ada.shape[0]
    c_pad = jnp.pad(c, ((0, 8 - batch % 8 if batch % 8 else 0), (0, 0)))
    mod = _ada(c_pad, w_ada, b_ada)[:, :batch]
    x2 = x.reshape(batch * seq, d)
    w = _Weights(*_split_w_in(w_in.astype(BF16)), *[a.astype(BF16) for a in (w_br_pool, w_br_sb, w_br_gla, w_out,
                                                               w_ff_gate, w_ff_up, w_ff_down)])

    def first_norm(l):
        sh1, sc1 = _mod_vectors(mod[l])[:2]
        return g_norm1[l].reshape(1, d), sc1, sh1

    h1 = _norm(x2, *first_norm(0), seq=seq, tm=_tiles(seq).rows_f32)
    for l in range(depth):
        next_norm = first_norm(l + 1) if l + 1 < depth else None
        x2, h1 = _layer(x2, h1, mod[l], next_norm, batch, seq, l, w, w_pool[l], pool_scale[l],
                        sb_q_gain[l], sb_k_gain[l], gla_w_a2[l], gla_b_a2[l], gla_out_gain[l], g_norm2[l])
    return x2.reshape(batch, seq, d)
```

```python
import functools
import math
from typing import NamedTuple

import numpy as np
import jax
import jax.numpy as jnp
from jax import lax
from jax.experimental import pallas as pl
from jax.experimental.pallas import tpu as pltpu

F32 = jnp.float32
BF16 = jnp.bfloat16

D_MODEL = 2048
POOL_WINDOWS = (2, 4, 8, 16)
POOL_GROUPS = 4
GROUP_DIM = 128
POOL_WIDTH = POOL_GROUPS * GROUP_DIM
SB_HEADS = 8
SB_WIDTH = SB_HEADS * GROUP_DIM
GLA_HEADS = 4
GLA_WIDTH = GLA_HEADS * GROUP_DIM
GLA_RANK = 16
GLA_TAU = 16.0
GLA_CHUNK = 128
GLA_LEVELS = GLA_CHUNK.bit_length() - 1
assert GLA_CHUNK == 1 << GLA_LEVELS
N_BRANCH = 3
RMS_EPS = 1e-6
LOG2E = 1.4426950408889634
POOL_HALO = 16
SB_DEAD_BITS = 160.0
SB_MASKED_BITS = 1.0e4
SB_STATIC_LAGS = 3

VMEM_LIMIT = 56 * 1024 * 1024
MXU_WIDTH = 256
ROW_CHUNK = 256


def _cparams(sem):
    return pltpu.CompilerParams(dimension_semantics=sem, vmem_limit_bytes=VMEM_LIMIT)


def _nt_dot(a, b):
    return lax.dot_general(a, b, (((1,), (1,)), ((), ())), preferred_element_type=F32)


def _silu(x):
    return x * jax.nn.sigmoid(x)


def _ada_kernel(c_ref, w_ref, b_ref, o_ref):
    c = c_ref[...]
    a = _silu(c).astype(BF16)
    o_ref[0] = jnp.dot(a, w_ref[0].astype(BF16), preferred_element_type=F32) + b_ref[0]


def _ada(c_pad, w_ada, b_ada, tn=1024):
    depth, d, n = w_ada.shape
    rows = c_pad.shape[0]
    return pl.pallas_call(
        _ada_kernel,
        grid=(depth, n // tn),
        in_specs=[
            pl.BlockSpec((rows, d), lambda l, j: (0, 0)),
            pl.BlockSpec((1, d, tn), lambda l, j: (l, 0, j)),
            pl.BlockSpec((1, 1, tn), lambda l, j: (l, 0, j)),
        ],
        out_specs=pl.BlockSpec((1, rows, tn), lambda l, j: (l, 0, j)),
        out_shape=jax.ShapeDtypeStruct((depth, rows, n), F32),
        compiler_params=_cparams(("arbitrary", "arbitrary")),
        name="ada_modulation",
    )(c_pad, w_ada, b_ada.reshape(depth, 1, n))


W_IN_SB = POOL_WIDTH
W_IN_GLA = W_IN_SB + 3 * SB_WIDTH
W_IN_A = W_IN_GLA + 4 * GLA_WIDTH
W_IN_GATE = W_IN_A + GLA_RANK
W_IN_COLS = W_IN_GATE + N_BRANCH * D_MODEL
A_LOW_COL = POOL_WIDTH + 4 * GLA_WIDTH
F32_COLS = A_LOW_COL + MXU_WIDTH


def _split_w_in_kernel(w_ref, f_ref, sb_ref, g_ref):
    f_ref[0, :, :POOL_WIDTH] = w_ref[0, :, :W_IN_SB]
    f_ref[0, :, POOL_WIDTH:A_LOW_COL] = w_ref[0, :, W_IN_GLA:W_IN_A]
    f_ref[0, :, A_LOW_COL:] = jnp.zeros((f_ref.shape[1], MXU_WIDTH), BF16)
    f_ref[0, :, A_LOW_COL:A_LOW_COL + GLA_RANK] = w_ref[0, :, W_IN_A:W_IN_GATE]
    sb_ref[0] = w_ref[0, :, W_IN_SB:W_IN_GLA]
    g_ref[0] = w_ref[0, :, W_IN_GATE:]


def _split_w_in(w_in, tr=512):
    depth, d, n = w_in.shape
    assert n == W_IN_COLS and w_in.dtype == BF16
    widths = (F32_COLS, 3 * SB_WIDTH, N_BRANCH * D_MODEL)
    return pl.pallas_call(
        _split_w_in_kernel,
        grid=(depth, d // tr),
        in_specs=[pl.BlockSpec((1, tr, n), lambda l, i: (l, i, 0))],
        out_specs=[pl.BlockSpec((1, tr, w), lambda l, i: (l, i, 0)) for w in widths],
        out_shape=[jax.ShapeDtypeStruct((depth, d, w), BF16) for w in widths],
        compiler_params=_cparams(("arbitrary", "arbitrary")),
        name="split_w_in",
    )(w_in)


def _modulated_norm(x, gain, scale, shift):
    ms = jnp.mean(x * x, axis=-1, keepdims=True)
    y = x * lax.rsqrt(ms + RMS_EPS) * gain
    return y * (1.0 + scale) + shift


def _group_rmsnorm(acc, colgain):
    out = []
    for c in range(acc.shape[1] // GROUP_DIM):
        sl = slice(c * GROUP_DIM, (c + 1) * GROUP_DIM)
        blk = acc[:, sl]
        ms = jnp.mean(blk * blk, axis=-1, keepdims=True)
        out.append(blk * lax.rsqrt(ms + RMS_EPS) * colgain[:, sl])
    return jnp.concatenate(out, axis=1)


def _norm_kernel(x_ref, g_ref, sc_ref, sh_ref, h_ref):
    h_ref[...] = _modulated_norm(x_ref[...], g_ref[...], sc_ref[0], sh_ref[0]).astype(BF16)


def _norm(x2, gain, scale, shift, *, seq, tm):
    m, d = x2.shape
    tpb = seq // tm
    mod = pl.BlockSpec((1, 1, d), lambda i: (i // tpb, 0, 0))
    return pl.pallas_call(
        _norm_kernel,
        grid=(m // tm,),
        in_specs=[pl.BlockSpec((tm, d), lambda i: (i, 0)), pl.BlockSpec((1, d), lambda i: (0, 0)), mod, mod],
        out_specs=pl.BlockSpec((tm, d), lambda i: (i, 0)),
        out_shape=jax.ShapeDtypeStruct((m, d), BF16),
        compiler_params=_cparams(("arbitrary",)),
        name="modulated_norm",
    )(x2, gain, scale, shift)


def _proj_kernel(h_ref, w_ref, cg_ref, o_ref, *, norm_cols):
    h = h_ref[...]
    for c in range(o_ref.shape[1] // MXU_WIDTH):
        cols = slice(c * MXU_WIDTH, (c + 1) * MXU_WIDTH)
        acc = jnp.dot(h, w_ref[:, cols], preferred_element_type=F32)
        if c * MXU_WIDTH < norm_cols:
            acc = _group_rmsnorm(acc, cg_ref[:, cols])
        o_ref[:, cols] = acc.astype(o_ref.dtype)


def _proj(h, w, layer, colgain, *, tm, norm_cols, out_dtype):
    m, d = h.shape
    n = w.shape[2]
    assert n % MXU_WIDTH == 0 and norm_cols % MXU_WIDTH == 0
    return pl.pallas_call(
        functools.partial(_proj_kernel, norm_cols=norm_cols),
        grid=(m // tm,),
        in_specs=[
            pl.BlockSpec((tm, d), lambda i: (i, 0)),
            pl.BlockSpec((None, d, n), lambda i: (layer, 0, 0), pipeline_mode=pl.Buffered(1)),
            pl.BlockSpec((1, n), lambda i: (0, 0)),
        ],
        out_specs=pl.BlockSpec((tm, n), lambda i: (i, 0)),
        out_shape=jax.ShapeDtypeStruct((m, n), out_dtype),
        compiler_params=_cparams(("arbitrary",)),
        name="in_proj",
    )(h, w, colgain)


def _pool_kernel(u_ref, w_ref, ps_ref, o_ref, ext_ref, *, ts):
    i = pl.program_id(1)

    @pl.when(i == 0)
    def _():
        ext_ref[0:POOL_HALO, :] = jnp.zeros((POOL_HALO, POOL_WIDTH), F32)

    @pl.when(i > 0)
    def _():
        ext_ref[0:POOL_HALO, :] = ext_ref[ts:ts + POOL_HALO, :]

    ext_ref[POOL_HALO:, :] = u_ref[0]
    pos1 = (i * ts + 1 + lax.broadcasted_iota(jnp.int32, (ts, GROUP_DIM), 0)).astype(F32)
    for g, w in enumerate(POOL_WINDOWS):
        cols = slice(g * GROUP_DIM, (g + 1) * GROUP_DIM)
        u = ext_ref[POOL_HALO:, cols]
        win = u
        for k in range(1, w):
            win = win + ext_ref[POOL_HALO - k:POOL_HALO - k + ts, cols]
        pooled = win / jnp.minimum(pos1, float(w)) - u
        y = jnp.dot(pooled.astype(BF16), w_ref[g], preferred_element_type=F32) * ps_ref[g]
        o_ref[0, :, cols] = y.astype(o_ref.dtype)


def _pool(proj3, w_pool, pool_scale, ts):
    b, s, _ = proj3.shape
    kern = functools.partial(_pool_kernel, ts=ts)
    return pl.pallas_call(
        kern,
        grid=(b, s // ts),
        in_specs=[
            pl.BlockSpec((1, ts, POOL_WIDTH), lambda bi, i: (bi, i, 0)),
            pl.BlockSpec((POOL_GROUPS, GROUP_DIM, GROUP_DIM), lambda bi, i: (0, 0, 0)),
            pl.BlockSpec((POOL_GROUPS, 1, GROUP_DIM), lambda bi, i: (0, 0, 0)),
        ],
        out_specs=pl.BlockSpec((1, ts, POOL_WIDTH), lambda bi, i: (bi, i, 0)),
        out_shape=jax.ShapeDtypeStruct((b, s, POOL_WIDTH), BF16),
        scratch_shapes=[pltpu.VMEM((ts + POOL_HALO, POOL_WIDTH), F32)],
        compiler_params=_cparams(("arbitrary", "arbitrary")),
        name="pool_mixer",
    )(proj3, w_pool, pool_scale.reshape(POOL_GROUPS, 1, GROUP_DIM))


def _sb_cumsum_matrix(tk):
    j = np.arange(tk)[:, None]
    s = np.arange(tk)[None, :]
    one = np.concatenate([(j >= s).astype(np.float32), np.ones((tk, tk), np.float32)], axis=1)
    return jnp.asarray(np.concatenate([one, one], axis=0), dtype=BF16)


def _sb_kernel(q_ref, k_ref, v_ref, w2_ref, o_ref, acc_ref, run_ref, *, tq, tk):
    i = pl.program_id(2)
    w2 = w2_ref[...]
    n_sub = tq // tk
    heads = [slice(hh * GROUP_DIM, (hh + 1) * GROUP_DIM) for hh in range(q_ref.shape[2] // GROUP_DIM)]
    acc_ref[...] = jnp.zeros(acc_ref.shape, F32)
    run_ref[...] = jnp.zeros(run_ref.shape, F32)
    row = lax.broadcasted_iota(jnp.int32, (tq, tk), 0)
    col = lax.broadcasted_iota(jnp.int32, (tq, tk), 1)
    diag_mask = col < (row & (tk - 1))
    sub = lambda r: slice(r * tk, (r + 1) * tk)

    def key_blocks(lag):
        blocks = []
        for r in range(n_sub):
            jb = i * n_sub + r - lag
            start = pl.multiple_of(jnp.maximum(jb, 0) * tk, tk)
            blocks.append((start, jnp.where(jb >= 0, 0.0, SB_MASKED_BITS)))
        return blocks

    def scores(cols, blocks, diagonal):
        z = jnp.concatenate([_nt_dot(q_ref[0, sub(r), cols], k_ref[0, pl.ds(start, tk), cols])
                             for r, (start, _) in enumerate(blocks)], axis=0)
        neg_abs = lax.bitcast_convert_type(
            lax.bitcast_convert_type(z, jnp.uint32) | jnp.uint32(0x80000000), F32)
        neg_log_keep = jnp.maximum(z, 0.0) + jnp.log(1.0 + jnp.exp2(neg_abs)) * LOG2E
        if diagonal:
            neg_log_keep = jnp.where(diag_mask, neg_log_keep, 0.0)
            z = jnp.where(diag_mask, z, -SB_MASKED_BITS)
        hi = lax.bitcast_convert_type(
            lax.bitcast_convert_type(neg_log_keep, jnp.uint32) & jnp.uint32(0xFFFF0000), F32)
        lo = neg_log_keep - hi
        c = jnp.dot(jnp.concatenate([hi.astype(BF16), lo.astype(BF16)], axis=1), w2, preferred_element_type=F32)
        return z, c

    def accumulate(hh, cols, blocks, z, c, diagonal):
        run = run_ref[hh]
        if not diagonal:
            run = run + jnp.concatenate([jnp.full((tk, tk), pen, F32) for _, pen in blocks], axis=0)
        a = jnp.exp2(z - c[:, :tk] - run).astype(BF16)
        for r, (start, _) in enumerate(blocks):
            acc_ref[hh, sub(r), :] += jnp.dot(a[sub(r), :], v_ref[0, pl.ds(start, tk), cols],
                                              preferred_element_type=F32)
        run_ref[hh] = run + c[:, tk:]

    lags = [key_blocks(lag) for lag in range(SB_STATIC_LAGS)]
    staged = [[scores(cols, blocks, lag == 0) for lag, blocks in enumerate(lags)] for cols in heads]
    for lag, blocks in enumerate(lags):
        for hh, cols in enumerate(heads):
            accumulate(hh, cols, blocks, *staged[hh][lag], lag == 0)

    for hh, cols in enumerate(heads):
        def cond(carry):
            lag, min_run = carry
            return jnp.logical_and(lag <= i * n_sub + n_sub - 1, min_run < SB_DEAD_BITS)

        def body(carry, hh=hh, cols=cols):
            lag, _ = carry
            blocks = key_blocks(lag)
            z, c = scores(cols, blocks, False)
            accumulate(hh, cols, blocks, z, c, False)
            return lag + 1, jnp.min(run_ref[hh])

        lax.while_loop(cond, body, (jnp.int32(SB_STATIC_LAGS), jnp.min(run_ref[hh])))
        o_ref[0, :, cols] = acc_ref[hh].astype(o_ref.dtype)


def _stick_breaking(qkv, tq, tk, heads_per_step):
    b, s, _ = qkv.shape
    hw = heads_per_step * GROUP_DIM
    groups = SB_HEADS // heads_per_step
    kern = functools.partial(_sb_kernel, tq=tq, tk=tk)
    return pl.pallas_call(
        kern,
        grid=(b, groups, s // tq),
        in_specs=[
            pl.BlockSpec((1, tq, hw), lambda bi, h, i: (bi, i, h)),
            pl.BlockSpec((1, s, hw), lambda bi, h, i: (bi, 0, groups + h)),
            pl.BlockSpec((1, s, hw), lambda bi, h, i: (bi, 0, 2 * groups + h)),
            pl.BlockSpec((2 * tk, 2 * tk), lambda bi, h, i: (0, 0)),
        ],
        out_specs=pl.BlockSpec((1, tq, hw), lambda bi, h, i: (bi, i, h)),
        out_shape=jax.ShapeDtypeStruct((b, s, SB_WIDTH), BF16),
        scratch_shapes=[pltpu.VMEM((heads_per_step, tq, GROUP_DIM), F32),
                        pltpu.VMEM((heads_per_step, tq, tk), F32)],
        compiler_params=_cparams(("arbitrary", "arbitrary", "arbitrary")),
        name="stick_breaking",
    )(qkv, qkv, qkv, _sb_cumsum_matrix(tk))


def _gla_constants():
    c = GLA_CHUNK
    t = np.arange(c)[:, None]
    j = np.arange(c)[None, :]
    dst = np.concatenate([j <= t, np.ones((8, c), bool)], axis=0).astype(np.float32)
    s = j
    masks = []
    for l in range(GLA_LEVELS):
        masks.append(((t ^ s) >> l == 1) & (t > s))
    masks.append(t == s)
    return jnp.asarray(dst, dtype=BF16), jnp.asarray(np.stack(masks).astype(np.float32))


def _split3(x):
    x1 = x.astype(BF16)
    r = x - x1.astype(F32)
    x2 = r.astype(BF16)
    x3 = (r - x2.astype(F32)).astype(BF16)
    return x1, x2, x3


def _gla_kernel(q_ref, k_ref, v_ref, r_ref, a_ref, wa_ref, ba_ref, og_ref, dst_ref, msk_ref,
                o_ref, st_ref):
    @pl.when(pl.program_id(1) == 0)
    def _():
        st_ref[...] = jnp.zeros(st_ref.shape, F32)

    for cc in range(q_ref.shape[1] // GLA_CHUNK):
        _gla_chunk(slice(cc * GLA_CHUNK, (cc + 1) * GLA_CHUNK), q_ref, k_ref, v_ref, r_ref, a_ref,
                   wa_ref, ba_ref, og_ref, dst_ref, msk_ref, o_ref, st_ref)


def _gla_chunk(rows, q_ref, k_ref, v_ref, r_ref, a_ref, wa_ref, ba_ref, og_ref, dst_ref, msk_ref,
               o_ref, st_ref):
    c = GLA_CHUNK
    x = jnp.dot(a_ref[0, rows, :].astype(BF16), wa_ref[...], preferred_element_type=F32) + ba_ref[...]
    log_sig = jnp.minimum(x, 0.0) - jnp.log(1.0 + jnp.exp(-jnp.abs(x)))
    g = log_sig * (LOG2E / GLA_TAU)
    dst = dst_ref[...]
    g1, g2, g3 = _split3(g)
    sums = (jnp.dot(dst, g1, preferred_element_type=F32)
            + jnp.dot(dst, g2, preferred_element_type=F32)
            + jnp.dot(dst, g3, preferred_element_type=F32))
    prefix = sums[:c]
    total = sums[c:c + 1]
    e_in = prefix
    e_out = total - prefix

    row = lax.broadcasted_iota(jnp.int32, g.shape, 0)
    pos = row & 3
    g_prev = pltpu.roll(g, 1, 0)
    g_next = pltpu.roll(g, c - 1, 0)
    e_lvl = [jnp.where((row & 1) == 1, g, 0.0),
             jnp.where(pos == 0, g_next, jnp.where(pos == 1, 0.0, jnp.where(pos == 2, g, g_prev + g)))]
    for l in range(2, GLA_LEVELS):
        m = 1 << l
        blocks = prefix.reshape(c // (2 * m), 2 * m, GLA_WIDTH)
        e_lvl.append((-jnp.abs(blocks - blocks[:, m - 1:m, :])).reshape(c, GLA_WIDTH))

    for h in range(GLA_HEADS):
        cols = slice(h * GROUP_DIM, (h + 1) * GROUP_DIM)
        q = q_ref[0, rows, cols] * (GROUP_DIM ** -0.5)
        k = k_ref[0, rows, cols]
        v = v_ref[0, rows, cols].astype(BF16)
        scores = msk_ref[GLA_LEVELS] * _nt_dot(q.astype(BF16), k.astype(BF16))
        for l in range(GLA_LEVELS):
            xl = jnp.exp2(e_lvl[l][:, cols])
            scores = scores + msk_ref[l] * _nt_dot((q * xl).astype(BF16), (k * xl).astype(BF16))
        o = jnp.dot(scores.astype(BF16), v, preferred_element_type=F32)
        x_in = jnp.exp2(e_in[:, cols])
        st = st_ref[h]
        o = o + _nt_dot((q * x_in).astype(BF16), st.astype(BF16))
        x_out = jnp.exp2(e_out[:, cols])
        x_all = jnp.exp2(total[:, cols])
        vt = v_ref[0, rows, cols].T.astype(BF16)
        st_ref[h] = st * x_all + jnp.dot(vt, (k * x_out).astype(BF16), preferred_element_type=F32)
        ms = jnp.mean(o * o, axis=-1, keepdims=True)
        o = o * lax.rsqrt(ms + RMS_EPS) * og_ref[...]
        o_ref[0, rows, cols] = (o * _silu(r_ref[0, rows, cols])).astype(o_ref.dtype)


def _gla(proj3, w_a2p, b_a2, out_gain, chunks_per_step):
    b, s, _ = proj3.shape
    c = GLA_CHUNK * chunks_per_step
    dst, masks = _gla_constants()
    wide = lambda blk: pl.BlockSpec((1, c, GLA_WIDTH), lambda bi, ci: (bi, ci, blk))
    const2 = lambda shape: pl.BlockSpec(shape, lambda bi, ci: (0, 0))
    a_blk = (POOL_WIDTH + 4 * GLA_WIDTH) // GROUP_DIM
    return pl.pallas_call(
        _gla_kernel,
        grid=(b, s // c),
        in_specs=[
            wide(1), wide(2), wide(3), wide(4),
            pl.BlockSpec((1, c, GROUP_DIM), lambda bi, ci: (bi, ci, a_blk)),
            const2((GROUP_DIM, GLA_WIDTH)),
            const2((1, GLA_WIDTH)),
            const2((1, GROUP_DIM)),
            const2(dst.shape),
            pl.BlockSpec(masks.shape, lambda bi, ci: (0, 0, 0)),
        ],
        out_specs=pl.BlockSpec((1, c, GLA_WIDTH), lambda bi, ci: (bi, ci, 0)),
        out_shape=jax.ShapeDtypeStruct((b, s, GLA_WIDTH), BF16),
        scratch_shapes=[pltpu.VMEM((GLA_HEADS, GROUP_DIM, GROUP_DIM), F32)],
        compiler_params=_cparams(("arbitrary", "arbitrary")),
        name="gla",
    )(proj3, proj3, proj3, proj3, proj3, w_a2p, b_a2, out_gain, dst, masks)


def _merge_kernel(h_ref, yp_ref, ys_ref, yg_ref, wgate_ref, wp_ref, ws_ref, wg_ref, o_ref):
    h = h_ref[...]
    d = o_ref.shape[1]
    branches = ((yp_ref[...], wp_ref), (ys_ref[...], ws_ref), (yg_ref[...], wg_ref))
    for c in range(d // MXU_WIDTH):
        cols = slice(c * MXU_WIDTH, (c + 1) * MXU_WIDTH)
        merged = None
        for br, (y, w_ref) in enumerate(branches):
            gate_cols = slice(br * d + c * MXU_WIDTH, br * d + (c + 1) * MXU_WIDTH)
            gate = jax.nn.sigmoid(jnp.dot(h, wgate_ref[:, gate_cols], preferred_element_type=F32))
            term = gate * jnp.dot(y, w_ref[:, cols], preferred_element_type=F32)
            merged = term if merged is None else merged + term
        o_ref[:, cols] = merged.astype(o_ref.dtype)


def _merge(h, y_pool, y_sb, y_gla, w_gates, w_br_pool, w_br_sb, w_br_gla, layer, *, tm):
    m, d = h.shape
    row = lambda width: pl.BlockSpec((tm, width), lambda i: (i, 0))
    resident = lambda w: pl.BlockSpec((None,) + w.shape[1:], lambda i: (layer, 0, 0),
                                      pipeline_mode=pl.Buffered(1))
    return pl.pallas_call(
        _merge_kernel,
        grid=(m // tm,),
        in_specs=[
            row(d), row(POOL_WIDTH), row(SB_WIDTH), row(GLA_WIDTH),
            resident(w_gates), resident(w_br_pool), resident(w_br_sb), resident(w_br_gla),
        ],
        out_specs=row(d),
        out_shape=jax.ShapeDtypeStruct((m, d), BF16),
        compiler_params=_cparams(("arbitrary",)),
        name="gated_merge",
    )(h, y_pool, y_sb, y_gla, w_gates, w_br_pool, w_br_sb, w_br_gla)


def _resproj_kernel(a_ref, w_ref, x_ref, ga_ref, g_ref, sc_ref, sh_ref, o_ref, h_ref):
    for r in range(a_ref.shape[0] // ROW_CHUNK):
        rows = slice(r * ROW_CHUNK, (r + 1) * ROW_CHUNK)
        y = jnp.dot(a_ref[rows, :], w_ref[...], preferred_element_type=F32)
        x_new = x_ref[rows, :] + ga_ref[0] * y
        o_ref[rows, :] = x_new
        h_ref[rows, :] = _modulated_norm(x_new, g_ref[...], sc_ref[0], sh_ref[0]).astype(BF16)


def _resproj(a, w, layer, x2, gate, gain, scale, shift, *, seq, tm):
    m, kdim = a.shape
    d = w.shape[2]
    tpb = seq // tm
    row = lambda width: pl.BlockSpec((tm, width), lambda i: (i, 0))
    mod = pl.BlockSpec((1, 1, d), lambda i: (i // tpb, 0, 0))
    return pl.pallas_call(
        _resproj_kernel,
        grid=(m // tm,),
        in_specs=[row(kdim), pl.BlockSpec((None, kdim, d), lambda i: (layer, 0, 0)), row(d), mod,
                  pl.BlockSpec((1, d), lambda i: (0, 0)), mod, mod],
        out_specs=[row(d), row(d)],
        out_shape=[jax.ShapeDtypeStruct((m, d), F32), jax.ShapeDtypeStruct((m, d), BF16)],
        compiler_params=_cparams(("arbitrary",)),
        name="residual_proj",
    )(a, w, x2, gate, gain, scale, shift)


def _swiglu_kernel(*refs, emit_next):
    if emit_next:
        h_ref, x_hbm, ga_ref, wg_ref, wu_ref, wd_ref, g_ref, sc_ref, sh_ref, o_ref, hn_ref, x_buf, x_sem = refs
    else:
        h_ref, x_hbm, ga_ref, wg_ref, wu_ref, wd_ref, o_ref, x_buf, x_sem = refs
    f = pl.program_id(1)
    last = pl.num_programs(1) - 1
    tm = h_ref.shape[0]

    def x_copy():
        row0 = pl.multiple_of(pl.program_id(0) * tm, tm)
        return pltpu.make_async_copy(x_hbm.at[pl.ds(row0, tm), :], x_buf, x_sem)

    def gated_partial(rows):
        h = h_ref[rows, :]
        gate = jnp.dot(h, wg_ref[...], preferred_element_type=F32)
        up = jnp.dot(h, wu_ref[...], preferred_element_type=F32)
        act = (_silu(gate) * up).astype(BF16)
        return ga_ref[0] * jnp.dot(act, wd_ref[...], preferred_element_type=F32)

    row_chunks = [slice(r * ROW_CHUNK, (r + 1) * ROW_CHUNK) for r in range(h_ref.shape[0] // ROW_CHUNK)]

    @pl.when(f == 0)
    def _():
        x_copy().start()
        for rows in row_chunks:
            o_ref[rows, :] = gated_partial(rows)

    @pl.when(jnp.logical_and(f > 0, f < last))
    def _():
        for rows in row_chunks:
            o_ref[rows, :] += gated_partial(rows)

    @pl.when(f == last)
    def _():
        x_copy().wait()
        for rows in row_chunks:
            x_new = x_buf[rows, :] + (o_ref[rows, :] + gated_partial(rows))
            o_ref[rows, :] = x_new
            if emit_next:
                hn_ref[rows, :] = _modulated_norm(x_new, g_ref[...], sc_ref[0], sh_ref[0]).astype(BF16)


def _swiglu(h, x2, gate, w_gate, w_up, w_down, layer, next_norm, *, seq, tm, tf):
    m, d = x2.shape
    dff = w_gate.shape[2]
    tpb = seq // tm
    n_steps = dff // tf
    assert n_steps >= 2
    emit_next = next_norm is not None
    row = pl.BlockSpec((tm, d), lambda i, f: (i, 0))
    mod = pl.BlockSpec((1, 1, d), lambda i, f: (i // tpb, 0, 0))
    in_specs = [row, pl.BlockSpec(memory_space=pl.ANY), mod,
                pl.BlockSpec((None, d, tf), lambda i, f: (layer, 0, f)),
                pl.BlockSpec((None, d, tf), lambda i, f: (layer, 0, f)),
                pl.BlockSpec((None, tf, d), lambda i, f: (layer, f, 0))]
    args = [h, x2, gate, w_gate, w_up, w_down]
    out_specs = [row]
    out_shape = [jax.ShapeDtypeStruct((m, d), F32)]
    if emit_next:
        in_specs += [pl.BlockSpec((1, d), lambda i, f: (0, 0)), mod, mod]
        args += list(next_norm)
        out_specs.append(row)
        out_shape.append(jax.ShapeDtypeStruct((m, d), BF16))
    outs = pl.pallas_call(
        functools.partial(_swiglu_kernel, emit_next=emit_next),
        grid=(m // tm, n_steps),
        in_specs=in_specs,
        out_specs=out_specs,
        out_shape=out_shape,
        scratch_shapes=[pltpu.VMEM((tm, d), F32), pltpu.SemaphoreType.DMA(())],
        compiler_params=_cparams(("arbitrary", "arbitrary")),
        name="swiglu",
    )(*args)
    return (outs[0], outs[1]) if emit_next else (outs[0], None)


class _Tiles(NamedTuple):
    rows: int
    rows_f32: int
    pool_rows: int
    ff_cols: int
    gla_chunks: int
    sb_heads: int


def _tile(n, pref):
    t = min(n, pref)
    assert n % t == 0, (n, t)
    return t


def _tiles(seq):
    return _Tiles(rows=_tile(seq, 1024), rows_f32=_tile(seq, 512), pool_rows=_tile(seq, 2048),
                  ff_cols=512,
                  gla_chunks=_tile(seq // GLA_CHUNK, 8), sb_heads=4)


def _mod_vectors(mod_l):
    d = D_MODEL
    return [mod_l[:, None, k * d:(k + 1) * d] for k in range(6)]


class _Weights(NamedTuple):
    f32cols: jax.Array
    sb: jax.Array
    gates: jax.Array
    br_pool: jax.Array
    br_sb: jax.Array
    br_gla: jax.Array
    out: jax.Array
    ff_gate: jax.Array
    ff_up: jax.Array
    ff_down: jax.Array


def _layer(x2, h1, mod_l, next_norm, batch, seq, layer, w, w_pool, pool_scale, sb_q_gain, sb_k_gain,
           gla_w_a2, gla_b_a2, gla_out_gain, g_norm2):
    d = D_MODEL
    t = _tiles(seq)
    _, _, ga1, sh2, sc2, ga2 = _mod_vectors(mod_l)

    proj = _proj(h1, w.f32cols, layer, jnp.ones((1, F32_COLS), F32), tm=t.rows, norm_cols=0, out_dtype=F32)
    q_scale = LOG2E / math.sqrt(GROUP_DIM)
    colgain = jnp.concatenate([jnp.tile(sb_q_gain * q_scale, SB_HEADS), jnp.tile(sb_k_gain, SB_HEADS),
                               jnp.ones((SB_WIDTH,), F32)]).reshape(1, 3 * SB_WIDTH)
    qkv = _proj(h1, w.sb, layer, colgain, tm=t.rows, norm_cols=2 * SB_WIDTH, out_dtype=BF16)

    proj3 = proj.reshape(batch, seq, F32_COLS)
    y_pool = _pool(proj3, w_pool.astype(BF16), pool_scale, ts=t.pool_rows)
    y_sb = _stick_breaking(qkv.reshape(batch, seq, 3 * SB_WIDTH), tq=t.rows, tk=GROUP_DIM,
                           heads_per_step=t.sb_heads)
    w_a2p = jnp.pad(gla_w_a2, ((0, GROUP_DIM - GLA_RANK), (0, 0))).astype(BF16)
    y_gla = _gla(proj3, w_a2p, gla_b_a2.reshape(1, GLA_WIDTH), gla_out_gain.reshape(1, GROUP_DIM),
                 chunks_per_step=t.gla_chunks)

    m = batch * seq
    merged = _merge(h1, y_pool.reshape(m, POOL_WIDTH), y_sb.reshape(m, SB_WIDTH), y_gla.reshape(m, GLA_WIDTH),
                    w.gates, w.br_pool, w.br_sb, w.br_gla, layer, tm=t.rows_f32)
    x2, h2 = _resproj(merged, w.out, layer, x2, ga1, g_norm2.reshape(1, d), sc2, sh2, seq=seq, tm=t.rows_f32)
    return _swiglu(h2, x2, ga2, w.ff_gate, w.ff_up, w.ff_down, layer, next_norm, seq=seq, tm=t.rows,
                   tf=t.ff_cols)


def kernel(x, c, w_ada, b_ada, g_norm1, w_in, w_pool, pool_scale, sb_q_gain, sb_k_gain, gla_w_a2, gla_b_a2, gla_out_gain, w_br_pool, w_br_sb, w_br_gla, w_out, g_norm2, w_ff_gate, w_ff_up, w_ff_down):
    batch, seq, d = x.shape
    depth = w_ada.shape[0]
    c_pad = jnp.pad(c, ((0, 8 - batch % 8 if batch % 8 else 0), (0, 0)))
    mod = _ada(c_pad, w_ada, b_ada)[:, :batch]
    x2 = x.reshape(batch * seq, d)
    w = _Weights(*_split_w_in(w_in.astype(BF16)), *[a.astype(BF16) for a in (w_br_pool, w_br_sb, w_br_gla, w_out,
                                                               w_ff_gate, w_ff_up, w_ff_down)])

    def first_norm(l):
        sh1, sc1 = _mod_vectors(mod[l])[:2]
        return g_norm1[l].reshape(1, d), sc1, sh1

    h1 = _norm(x2, *first_norm(0), seq=seq, tm=_tiles(seq).rows_f32)
    for l in range(depth):
        next_norm = first_norm(l + 1) if l + 1 < depth else None
        x2, h1 = _layer(x2, h1, mod[l], next_norm, batch, seq, l, w, w_pool[l], pool_scale[l],
                        sb_q_gain[l], sb_k_gain[l], gla_w_a2[l], gla_b_a2[l], gla_out_gain[l], g_norm2[l])
    return x2.reshape(batch, seq, d)
```

```python
import functools
import math
from typing import NamedTuple

import numpy as np
import jax
import jax.numpy as jnp
from jax import lax
from jax.experimental import pallas as pl
from jax.experimental.pallas import tpu as pltpu

F32 = jnp.float32
BF16 = jnp.bfloat16

D_MODEL = 2048
POOL_WINDOWS = (2, 4, 8, 16)
POOL_GROUPS = 4
GROUP_DIM = 128
POOL_WIDTH = POOL_GROUPS * GROUP_DIM
SB_HEADS = 8
SB_WIDTH = SB_HEADS * GROUP_DIM
GLA_HEADS = 4
GLA_WIDTH = GLA_HEADS * GROUP_DIM
GLA_RANK = 16
GLA_TAU = 16.0
GLA_CHUNK = 128
GLA_LEVELS = GLA_CHUNK.bit_length() - 1
assert GLA_CHUNK == 1 << GLA_LEVELS
N_BRANCH = 3
RMS_EPS = 1e-6
LOG2E = 1.4426950408889634
POOL_HALO = 16
SB_DEAD_BITS = 160.0
SB_MASKED_BITS = 1.0e4
SB_STATIC_LAGS = 3

VMEM_LIMIT = 56 * 1024 * 1024
MXU_WIDTH = 256
ROW_CHUNK = 256


def _cparams(sem):
    return pltpu.CompilerParams(dimension_semantics=sem, vmem_limit_bytes=VMEM_LIMIT)


def _nt_dot(a, b):
    return lax.dot_general(a, b, (((1,), (1,)), ((), ())), preferred_element_type=F32)


def _silu(x):
    return x * jax.nn.sigmoid(x)


def _ada_kernel(c_ref, w_ref, b_ref, o_ref):
    c = c_ref[...]
    a = _silu(c).astype(BF16)
    o_ref[0] = jnp.dot(a, w_ref[0].astype(BF16), preferred_element_type=F32) + b_ref[0]


def _ada(c_pad, w_ada, b_ada, tn=2048):
    depth, d, n = w_ada.shape
    rows = c_pad.shape[0]
    return pl.pallas_call(
        _ada_kernel,
        grid=(depth, n // tn),
        in_specs=[
            pl.BlockSpec((rows, d), lambda l, j: (0, 0)),
            pl.BlockSpec((1, d, tn), lambda l, j: (l, 0, j)),
            pl.BlockSpec((1, 1, tn), lambda l, j: (l, 0, j)),
        ],
        out_specs=pl.BlockSpec((1, rows, tn), lambda l, j: (l, 0, j)),
        out_shape=jax.ShapeDtypeStruct((depth, rows, n), F32),
        compiler_params=_cparams(("arbitrary", "arbitrary")),
        name="ada_modulation",
    )(c_pad, w_ada, b_ada.reshape(depth, 1, n))


W_IN_SB = POOL_WIDTH
W_IN_GLA = W_IN_SB + 3 * SB_WIDTH
W_IN_A = W_IN_GLA + 4 * GLA_WIDTH
W_IN_GATE = W_IN_A + GLA_RANK
W_IN_COLS = W_IN_GATE + N_BRANCH * D_MODEL
A_LOW_COL = POOL_WIDTH + 4 * GLA_WIDTH
F32_COLS = A_LOW_COL + MXU_WIDTH


def _split_w_in_kernel(w_ref, f_ref, sb_ref, g_ref):
    f_ref[0, :, :POOL_WIDTH] = w_ref[0, :, :W_IN_SB]
    f_ref[0, :, POOL_WIDTH:A_LOW_COL] = w_ref[0, :, W_IN_GLA:W_IN_A]
    f_ref[0, :, A_LOW_COL:] = jnp.zeros((f_ref.shape[1], MXU_WIDTH), BF16)
    f_ref[0, :, A_LOW_COL:A_LOW_COL + GLA_RANK] = w_ref[0, :, W_IN_A:W_IN_GATE]
    sb_ref[0] = w_ref[0, :, W_IN_SB:W_IN_GLA]
    g_ref[0] = w_ref[0, :, W_IN_GATE:]


def _split_w_in(w_in, tr=512):
    depth, d, n = w_in.shape
    assert n == W_IN_COLS and w_in.dtype == BF16
    widths = (F32_COLS, 3 * SB_WIDTH, N_BRANCH * D_MODEL)
    return pl.pallas_call(
        _split_w_in_kernel,
        grid=(depth, d // tr),
        in_specs=[pl.BlockSpec((1, tr, n), lambda l, i: (l, i, 0))],
        out_specs=[pl.BlockSpec((1, tr, w), lambda l, i: (l, i, 0)) for w in widths],
        out_shape=[jax.ShapeDtypeStruct((depth, d, w), BF16) for w in widths],
        compiler_params=_cparams(("arbitrary", "arbitrary")),
        name="split_w_in",
    )(w_in)


def _modulated_norm(x, gain, scale, shift):
    ms = jnp.mean(x * x, axis=-1, keepdims=True)
    y = x * lax.rsqrt(ms + RMS_EPS) * gain
    return y * (1.0 + scale) + shift


def _group_rmsnorm(acc, colgain):
    out = []
    for c in range(acc.shape[1] // GROUP_DIM):
        sl = slice(c * GROUP_DIM, (c + 1) * GROUP_DIM)
        blk = acc[:, sl]
        ms = jnp.mean(blk * blk, axis=-1, keepdims=True)
        out.append(blk * lax.rsqrt(ms + RMS_EPS) * colgain[:, sl])
    return jnp.concatenate(out, axis=1)


def _norm_kernel(x_ref, g_ref, sc_ref, sh_ref, h_ref):
    h_ref[...] = _modulated_norm(x_ref[...], g_ref[...], sc_ref[0], sh_ref[0]).astype(BF16)


def _norm(x2, gain, scale, shift, *, seq, tm):
    m, d = x2.shape
    tpb = seq // tm
    mod = pl.BlockSpec((1, 1, d), lambda i: (i // tpb, 0, 0))
    return pl.pallas_call(
        _norm_kernel,
        grid=(m // tm,),
        in_specs=[pl.BlockSpec((tm, d), lambda i: (i, 0)), pl.BlockSpec((1, d), lambda i: (0, 0)), mod, mod],
        out_specs=pl.BlockSpec((tm, d), lambda i: (i, 0)),
        out_shape=jax.ShapeDtypeStruct((m, d), BF16),
        compiler_params=_cparams(("arbitrary",)),
        name="modulated_norm",
    )(x2, gain, scale, shift)


def _proj_kernel(h_ref, w_ref, cg_ref, o_ref, *, norm_cols):
    h = h_ref[...]
    for c in range(o_ref.shape[1] // MXU_WIDTH):
        cols = slice(c * MXU_WIDTH, (c + 1) * MXU_WIDTH)
        acc = jnp.dot(h, w_ref[:, cols], preferred_element_type=F32)
        if c * MXU_WIDTH < norm_cols:
            acc = _group_rmsnorm(acc, cg_ref[:, cols])
        o_ref[:, cols] = acc.astype(o_ref.dtype)


def _proj(h, w, layer, colgain, *, tm, norm_cols, out_dtype):
    m, d = h.shape
    n = w.shape[2]
    assert n % MXU_WIDTH == 0 and norm_cols % MXU_WIDTH == 0
    return pl.pallas_call(
        functools.partial(_proj_kernel, norm_cols=norm_cols),
        grid=(m // tm,),
        in_specs=[
            pl.BlockSpec((tm, d), lambda i: (i, 0)),
            pl.BlockSpec((None, d, n), lambda i: (layer, 0, 0), pipeline_mode=pl.Buffered(1)),
            pl.BlockSpec((1, n), lambda i: (0, 0)),
        ],
        out_specs=pl.BlockSpec((tm, n), lambda i: (i, 0)),
        out_shape=jax.ShapeDtypeStruct((m, n), out_dtype),
        compiler_params=_cparams(("arbitrary",)),
        name="in_proj",
    )(h, w, colgain)


def _pool_kernel(u_ref, w_ref, ps_ref, o_ref, ext_ref, *, ts):
    i = pl.program_id(1)

    @pl.when(i == 0)
    def _():
        ext_ref[0:POOL_HALO, :] = jnp.zeros((POOL_HALO, POOL_WIDTH), F32)

    @pl.when(i > 0)
    def _():
        ext_ref[0:POOL_HALO, :] = ext_ref[ts:ts + POOL_HALO, :]

    ext_ref[POOL_HALO:, :] = u_ref[0]
    pos1 = (i * ts + 1 + lax.broadcasted_iota(jnp.int32, (ts, GROUP_DIM), 0)).astype(F32)
    for g, w in enumerate(POOL_WINDOWS):
        cols = slice(g * GROUP_DIM, (g + 1) * GROUP_DIM)
        u = ext_ref[POOL_HALO:, cols]
        win = u
        for k in range(1, w):
            win = win + ext_ref[POOL_HALO - k:POOL_HALO - k + ts, cols]
        pooled = win / jnp.minimum(pos1, float(w)) - u
        y = jnp.dot(pooled.astype(BF16), w_ref[g], preferred_element_type=F32) * ps_ref[g]
        o_ref[0, :, cols] = y.astype(o_ref.dtype)


def _pool(proj3, w_pool, pool_scale, ts):
    b, s, _ = proj3.shape
    kern = functools.partial(_pool_kernel, ts=ts)
    return pl.pallas_call(
        kern,
        grid=(b, s // ts),
        in_specs=[
            pl.BlockSpec((1, ts, POOL_WIDTH), lambda bi, i: (bi, i, 0)),
            pl.BlockSpec((POOL_GROUPS, GROUP_DIM, GROUP_DIM), lambda bi, i: (0, 0, 0)),
            pl.BlockSpec((POOL_GROUPS, 1, GROUP_DIM), lambda bi, i: (0, 0, 0)),
        ],
        out_specs=pl.BlockSpec((1, ts, POOL_WIDTH), lambda bi, i: (bi, i, 0)),
        out_shape=jax.ShapeDtypeStruct((b, s, POOL_WIDTH), BF16),
        scratch_shapes=[pltpu.VMEM((ts + POOL_HALO, POOL_WIDTH), F32)],
        compiler_params=_cparams(("arbitrary", "arbitrary")),
        name="pool_mixer",
    )(proj3, w_pool, pool_scale.reshape(POOL_GROUPS, 1, GROUP_DIM))


def _sb_cumsum_matrix(tk):
    j = np.arange(tk)[:, None]
    s = np.arange(tk)[None, :]
    one = np.concatenate([(j >= s).astype(np.float32), np.ones((tk, tk), np.float32)], axis=1)
    return jnp.asarray(np.concatenate([one, one], axis=0), dtype=BF16)


def _sb_kernel(q_ref, k_ref, v_ref, w2_ref, o_ref, acc_ref, run_ref, *, tq, tk):
    i = pl.program_id(2)
    w2 = w2_ref[...]
    n_sub = tq // tk
    heads = [slice(hh * GROUP_DIM, (hh + 1) * GROUP_DIM) for hh in range(q_ref.shape[2] // GROUP_DIM)]
    acc_ref[...] = jnp.zeros(acc_ref.shape, F32)
    run_ref[...] = jnp.zeros(run_ref.shape, F32)
    row = lax.broadcasted_iota(jnp.int32, (tq, tk), 0)
    col = lax.broadcasted_iota(jnp.int32, (tq, tk), 1)
    diag_mask = col < (row & (tk - 1))
    sub = lambda r: slice(r * tk, (r + 1) * tk)

    def key_blocks(lag):
        blocks = []
        for r in range(n_sub):
            jb = i * n_sub + r - lag
            start = pl.multiple_of(jnp.maximum(jb, 0) * tk, tk)
            blocks.append((start, jnp.where(jb >= 0, 0.0, SB_MASKED_BITS)))
        return blocks

    def scores(cols, blocks, diagonal):
        z = jnp.concatenate([_nt_dot(q_ref[0, sub(r), cols], k_ref[0, pl.ds(start, tk), cols])
                             for r, (start, _) in enumerate(blocks)], axis=0)
        neg_abs = lax.bitcast_convert_type(
            lax.bitcast_convert_type(z, jnp.uint32) | jnp.uint32(0x80000000), F32)
        neg_log_keep = jnp.maximum(z, 0.0) + jnp.log(1.0 + jnp.exp2(neg_abs)) * LOG2E
        if diagonal:
            neg_log_keep = jnp.where(diag_mask, neg_log_keep, 0.0)
            z = jnp.where(diag_mask, z, -SB_MASKED_BITS)
        hi = lax.bitcast_convert_type(
            lax.bitcast_convert_type(neg_log_keep, jnp.uint32) & jnp.uint32(0xFFFF0000), F32)
        lo = neg_log_keep - hi
        c = jnp.dot(jnp.concatenate([hi.astype(BF16), lo.astype(BF16)], axis=1), w2, preferred_element_type=F32)
        return z, c

    def accumulate(hh, cols, blocks, z, c, diagonal):
        run = run_ref[hh]
        if not diagonal:
            run = run + jnp.concatenate([jnp.full((tk, tk), pen, F32) for _, pen in blocks], axis=0)
        a = jnp.exp2(z - c[:, :tk] - run).astype(BF16)
        for r, (start, _) in enumerate(blocks):
            acc_ref[hh, sub(r), :] += jnp.dot(a[sub(r), :], v_ref[0, pl.ds(start, tk), cols],
                                              preferred_element_type=F32)
        run_ref[hh] = run + c[:, tk:]

    lags = [key_blocks(lag) for lag in range(SB_STATIC_LAGS)]
    staged = [[scores(cols, blocks, lag == 0) for lag, blocks in enumerate(lags)] for cols in heads]
    for lag, blocks in enumerate(lags):
        for hh, cols in enumerate(heads):
            accumulate(hh, cols, blocks, *staged[hh][lag], lag == 0)

    for hh, cols in enumerate(heads):
        def cond(carry):
            lag, min_run = carry
            return jnp.logical_and(lag <= i * n_sub + n_sub - 1, min_run < SB_DEAD_BITS)

        def body(carry, hh=hh, cols=cols):
            lag, _ = carry
            blocks = key_blocks(lag)
            z, c = scores(cols, blocks, False)
            accumulate(hh, cols, blocks, z, c, False)
            return lag + 1, jnp.min(run_ref[hh])

        lax.while_loop(cond, body, (jnp.int32(SB_STATIC_LAGS), jnp.min(run_ref[hh])))
        o_ref[0, :, cols] = acc_ref[hh].astype(o_ref.dtype)


def _stick_breaking(qkv, tq, tk, heads_per_step):
    b, s, _ = qkv.shape
    hw = heads_per_step * GROUP_DIM
    groups = SB_HEADS // heads_per_step
    kern = functools.partial(_sb_kernel, tq=tq, tk=tk)
    return pl.pallas_call(
        kern,
        grid=(b, groups, s // tq),
        in_specs=[
            pl.BlockSpec((1, tq, hw), lambda bi, h, i: (bi, i, h)),
            pl.BlockSpec((1, s, hw), lambda bi, h, i: (bi, 0, groups + h)),
            pl.BlockSpec((1, s, hw), lambda bi, h, i: (bi, 0, 2 * groups + h)),
            pl.BlockSpec((2 * tk, 2 * tk), lambda bi, h, i: (0, 0)),
        ],
        out_specs=pl.BlockSpec((1, tq, hw), lambda bi, h, i: (bi, i, h)),
        out_shape=jax.ShapeDtypeStruct((b, s, SB_WIDTH), BF16),
        scratch_shapes=[pltpu.VMEM((heads_per_step, tq, GROUP_DIM), F32),
                        pltpu.VMEM((heads_per_step, tq, tk), F32)],
        compiler_params=_cparams(("arbitrary", "arbitrary", "arbitrary")),
        name="stick_breaking",
    )(qkv, qkv, qkv, _sb_cumsum_matrix(tk))


def _gla_constants():
    c = GLA_CHUNK
    t = np.arange(c)[:, None]
    j = np.arange(c)[None, :]
    dst = np.concatenate([j <= t, np.ones((8, c), bool)], axis=0).astype(np.float32)
    s = j
    masks = []
    for l in range(GLA_LEVELS):
        masks.append(((t ^ s) >> l == 1) & (t > s))
    masks.append(t == s)
    return jnp.asarray(dst, dtype=BF16), jnp.asarray(np.stack(masks).astype(np.float32))


def _split3(x):
    x1 = x.astype(BF16)
    r = x - x1.astype(F32)
    x2 = r.astype(BF16)
    x3 = (r - x2.astype(F32)).astype(BF16)
    return x1, x2, x3


def _gla_kernel(q_ref, k_ref, v_ref, r_ref, a_ref, wa_ref, ba_ref, og_ref, dst_ref, msk_ref,
                o_ref, st_ref):
    @pl.when(pl.program_id(1) == 0)
    def _():
        st_ref[...] = jnp.zeros(st_ref.shape, F32)

    for cc in range(q_ref.shape[1] // GLA_CHUNK):
        _gla_chunk(slice(cc * GLA_CHUNK, (cc + 1) * GLA_CHUNK), q_ref, k_ref, v_ref, r_ref, a_ref,
                   wa_ref, ba_ref, og_ref, dst_ref, msk_ref, o_ref, st_ref)


def _gla_chunk(rows, q_ref, k_ref, v_ref, r_ref, a_ref, wa_ref, ba_ref, og_ref, dst_ref, msk_ref,
               o_ref, st_ref):
    c = GLA_CHUNK
    x = jnp.dot(a_ref[0, rows, :].astype(BF16), wa_ref[...], preferred_element_type=F32) + ba_ref[...]
    log_sig = jnp.minimum(x, 0.0) - jnp.log(1.0 + jnp.exp(-jnp.abs(x)))
    g = log_sig * (LOG2E / GLA_TAU)
    dst = dst_ref[...]
    g1, g2, g3 = _split3(g)
    sums = (jnp.dot(dst, g1, preferred_element_type=F32)
            + jnp.dot(dst, g2, preferred_element_type=F32)
            + jnp.dot(dst, g3, preferred_element_type=F32))
    prefix = sums[:c]
    total = sums[c:c + 1]
    e_in = prefix
    e_out = total - prefix

    row = lax.broadcasted_iota(jnp.int32, g.shape, 0)
    pos = row & 3
    g_prev = pltpu.roll(g, 1, 0)
    g_next = pltpu.roll(g, c - 1, 0)
    e_lvl = [jnp.where((row & 1) == 1, g, 0.0),
             jnp.where(pos == 0, g_next, jnp.where(pos == 1, 0.0, jnp.where(pos == 2, g, g_prev + g)))]
    for l in range(2, GLA_LEVELS):
        m = 1 << l
        blocks = prefix.reshape(c // (2 * m), 2 * m, GLA_WIDTH)
        e_lvl.append((-jnp.abs(blocks - blocks[:, m - 1:m, :])).reshape(c, GLA_WIDTH))

    for h in range(GLA_HEADS):
        cols = slice(h * GROUP_DIM, (h + 1) * GROUP_DIM)
        q = q_ref[0, rows, cols] * (GROUP_DIM ** -0.5)
        k = k_ref[0, rows, cols]
        v = v_ref[0, rows, cols].astype(BF16)
        scores = msk_ref[GLA_LEVELS] * _nt_dot(q.astype(BF16), k.astype(BF16))
        for l in range(GLA_LEVELS):
            xl = jnp.exp2(e_lvl[l][:, cols])
            scores = scores + msk_ref[l] * _nt_dot((q * xl).astype(BF16), (k * xl).astype(BF16))
        o = jnp.dot(scores.astype(BF16), v, preferred_element_type=F32)
        x_in = jnp.exp2(e_in[:, cols])
        st = st_ref[h]
        o = o + _nt_dot((q * x_in).astype(BF16), st.astype(BF16))
        x_out = jnp.exp2(e_out[:, cols])
        x_all = jnp.exp2(total[:, cols])
        vt = v_ref[0, rows, cols].T.astype(BF16)
        st_ref[h] = st * x_all + jnp.dot(vt, (k * x_out).astype(BF16), preferred_element_type=F32)
        ms = jnp.mean(o * o, axis=-1, keepdims=True)
        o = o * lax.rsqrt(ms + RMS_EPS) * og_ref[...]
        o_ref[0, rows, cols] = (o * _silu(r_ref[0, rows, cols])).astype(o_ref.dtype)


def _gla(proj3, w_a2p, b_a2, out_gain, chunks_per_step):
    b, s, _ = proj3.shape
    c = GLA_CHUNK * chunks_per_step
    dst, masks = _gla_constants()
    wide = lambda blk: pl.BlockSpec((1, c, GLA_WIDTH), lambda bi, ci: (bi, ci, blk))
    const2 = lambda shape: pl.BlockSpec(shape, lambda bi, ci: (0, 0))
    a_blk = (POOL_WIDTH + 4 * GLA_WIDTH) // GROUP_DIM
    return pl.pallas_call(
        _gla_kernel,
        grid=(b, s // c),
        in_specs=[
            wide(1), wide(2), wide(3), wide(4),
            pl.BlockSpec((1, c, GROUP_DIM), lambda bi, ci: (bi, ci, a_blk)),
            const2((GROUP_DIM, GLA_WIDTH)),
            const2((1, GLA_WIDTH)),
            const2((1, GROUP_DIM)),
            const2(dst.shape),
            pl.BlockSpec(masks.shape, lambda bi, ci: (0, 0, 0)),
        ],
        out_specs=pl.BlockSpec((1, c, GLA_WIDTH), lambda bi, ci: (bi, ci, 0)),
        out_shape=jax.ShapeDtypeStruct((b, s, GLA_WIDTH), BF16),
        scratch_shapes=[pltpu.VMEM((GLA_HEADS, GROUP_DIM, GROUP_DIM), F32)],
        compiler_params=_cparams(("arbitrary", "arbitrary")),
        name="gla",
    )(proj3, proj3, proj3, proj3, proj3, w_a2p, b_a2, out_gain, dst, masks)


def _merge_kernel(h_ref, yp_ref, ys_ref, yg_ref, wgate_ref, wp_ref, ws_ref, wg_ref, o_ref):
    h = h_ref[...]
    d = o_ref.shape[1]
    branches = ((yp_ref[...], wp_ref), (ys_ref[...], ws_ref), (yg_ref[...], wg_ref))
    for c in range(d // MXU_WIDTH):
        cols = slice(c * MXU_WIDTH, (c + 1) * MXU_WIDTH)
        merged = None
        for br, (y, w_ref) in enumerate(branches):
            gate_cols = slice(br * d + c * MXU_WIDTH, br * d + (c + 1) * MXU_WIDTH)
            gate = jax.nn.sigmoid(jnp.dot(h, wgate_ref[:, gate_cols], preferred_element_type=F32))
            term = gate * jnp.dot(y, w_ref[:, cols], preferred_element_type=F32)
            merged = term if merged is None else merged + term
        o_ref[:, cols] = merged.astype(o_ref.dtype)


def _merge(h, y_pool, y_sb, y_gla, w_gates, w_br_pool, w_br_sb, w_br_gla, layer, *, tm):
    m, d = h.shape
    row = lambda width: pl.BlockSpec((tm, width), lambda i: (i, 0))
    resident = lambda w: pl.BlockSpec((None,) + w.shape[1:], lambda i: (layer, 0, 0),
                                      pipeline_mode=pl.Buffered(1))
    return pl.pallas_call(
        _merge_kernel,
        grid=(m // tm,),
        in_specs=[
            row(d), row(POOL_WIDTH), row(SB_WIDTH), row(GLA_WIDTH),
            resident(w_gates), resident(w_br_pool), resident(w_br_sb), resident(w_br_gla),
        ],
        out_specs=row(d),
        out_shape=jax.ShapeDtypeStruct((m, d), BF16),
        compiler_params=_cparams(("arbitrary",)),
        name="gated_merge",
    )(h, y_pool, y_sb, y_gla, w_gates, w_br_pool, w_br_sb, w_br_gla)


def _resproj_kernel(a_ref, w_ref, x_ref, ga_ref, g_ref, sc_ref, sh_ref, o_ref, h_ref):
    for r in range(a_ref.shape[0] // ROW_CHUNK):
        rows = slice(r * ROW_CHUNK, (r + 1) * ROW_CHUNK)
        y = jnp.dot(a_ref[rows, :], w_ref[...], preferred_element_type=F32)
        x_new = x_ref[rows, :] + ga_ref[0] * y
        o_ref[rows, :] = x_new
        h_ref[rows, :] = _modulated_norm(x_new, g_ref[...], sc_ref[0], sh_ref[0]).astype(BF16)


def _resproj(a, w, layer, x2, gate, gain, scale, shift, *, seq, tm):
    m, kdim = a.shape
    d = w.shape[2]
    tpb = seq // tm
    row = lambda width: pl.BlockSpec((tm, width), lambda i: (i, 0))
    mod = pl.BlockSpec((1, 1, d), lambda i: (i // tpb, 0, 0))
    return pl.pallas_call(
        _resproj_kernel,
        grid=(m // tm,),
        in_specs=[row(kdim), pl.BlockSpec((None, kdim, d), lambda i: (layer, 0, 0)), row(d), mod,
                  pl.BlockSpec((1, d), lambda i: (0, 0)), mod, mod],
        out_specs=[row(d), row(d)],
        out_shape=[jax.ShapeDtypeStruct((m, d), F32), jax.ShapeDtypeStruct((m, d), BF16)],
        compiler_params=_cparams(("arbitrary",)),
        name="residual_proj",
    )(a, w, x2, gate, gain, scale, shift)


def _swiglu_kernel(*refs, emit_next):
    if emit_next:
        h_ref, x_hbm, ga_ref, wg_ref, wu_ref, wd_ref, g_ref, sc_ref, sh_ref, o_ref, hn_ref, x_buf, x_sem = refs
    else:
        h_ref, x_hbm, ga_ref, wg_ref, wu_ref, wd_ref, o_ref, x_buf, x_sem = refs
    f = pl.program_id(1)
    last = pl.num_programs(1) - 1
    tm = h_ref.shape[0]

    def x_copy():
        row0 = pl.multiple_of(pl.program_id(0) * tm, tm)
        return pltpu.make_async_copy(x_hbm.at[pl.ds(row0, tm), :], x_buf, x_sem)

    def gated_partial(rows):
        h = h_ref[rows, :]
        gate = jnp.dot(h, wg_ref[...], preferred_element_type=F32)
        up = jnp.dot(h, wu_ref[...], preferred_element_type=F32)
        act = (_silu(gate) * up).astype(BF16)
        return ga_ref[0] * jnp.dot(act, wd_ref[...], preferred_element_type=F32)

    row_chunks = [slice(r * ROW_CHUNK, (r + 1) * ROW_CHUNK) for r in range(h_ref.shape[0] // ROW_CHUNK)]

    @pl.when(f == 0)
    def _():
        x_copy().start()
        for rows in row_chunks:
            o_ref[rows, :] = gated_partial(rows)

    @pl.when(jnp.logical_and(f > 0, f < last))
    def _():
        for rows in row_chunks:
            o_ref[rows, :] += gated_partial(rows)

    @pl.when(f == last)
    def _():
        x_copy().wait()
        for rows in row_chunks:
            x_new = x_buf[rows, :] + (o_ref[rows, :] + gated_partial(rows))
            o_ref[rows, :] = x_new
            if emit_next:
                hn_ref[rows, :] = _modulated_norm(x_new, g_ref[...], sc_ref[0], sh_ref[0]).astype(BF16)


def _swiglu(h, x2, gate, w_gate, w_up, w_down, layer, next_norm, *, seq, tm, tf):
    m, d = x2.shape
    dff = w_gate.shape[2]
    tpb = seq // tm
    n_steps = dff // tf
    assert n_steps >= 2
    emit_next = next_norm is not None
    row = pl.BlockSpec((tm, d), lambda i, f: (i, 0))
    mod = pl.BlockSpec((1, 1, d), lambda i, f: (i // tpb, 0, 0))
    in_specs = [row, pl.BlockSpec(memory_space=pl.ANY), mod,
                pl.BlockSpec((None, d, tf), lambda i, f: (layer, 0, f)),
                pl.BlockSpec((None, d, tf), lambda i, f: (layer, 0, f)),
                pl.BlockSpec((None, tf, d), lambda i, f: (layer, f, 0))]
    args = [h, x2, gate, w_gate, w_up, w_down]
    out_specs = [row]
    out_shape = [jax.ShapeDtypeStruct((m, d), F32)]
    if emit_next:
        in_specs += [pl.BlockSpec((1, d), lambda i, f: (0, 0)), mod, mod]
        args += list(next_norm)
        out_specs.append(row)
        out_shape.append(jax.ShapeDtypeStruct((m, d), BF16))
    outs = pl.pallas_call(
        functools.partial(_swiglu_kernel, emit_next=emit_next),
        grid=(m // tm, n_steps),
        in_specs=in_specs,
        out_specs=out_specs,
        out_shape=out_shape,
        scratch_shapes=[pltpu.VMEM((tm, d), F32), pltpu.SemaphoreType.DMA(())],
        compiler_params=_cparams(("arbitrary", "arbitrary")),
        name="swiglu",
    )(*args)
    return (outs[0], outs[1]) if emit_next else (outs[0], None)


class _Tiles(NamedTuple):
    rows: int
    rows_f32: int
    pool_rows: int
    ff_cols: int
    gla_chunks: int
    sb_heads: int


def _tile(n, pref):
    t = min(n, pref)
    assert n % t == 0, (n, t)
    return t


def _tiles(seq):
    return _Tiles(rows=_tile(seq, 1024), rows_f32=_tile(seq, 512), pool_rows=_tile(seq, 2048),
                  ff_cols=512,
                  gla_chunks=_tile(seq // GLA_CHUNK, 16), sb_heads=4)


def _mod_vectors(mod_l):
    d = D_MODEL
    return [mod_l[:, None, k * d:(k + 1) * d] for k in range(6)]


class _Weights(NamedTuple):
    f32cols: jax.Array
    sb: jax.Array
    gates: jax.Array
    br_pool: jax.Array
    br_sb: jax.Array
    br_gla: jax.Array
    out: jax.Array
    ff_gate: jax.Array
    ff_up: jax.Array
    ff_down: jax.Array


def _layer(x2, h1, mod_l, next_norm, batch, seq, layer, w, w_pool, pool_scale, sb_q_gain, sb_k_gain,
           gla_w_a2, gla_b_a2, gla_out_gain, g_norm2):
    d = D_MODEL
    t = _tiles(seq)
    _, _, ga1, sh2, sc2, ga2 = _mod_vectors(mod_l)

    proj = _proj(h1, w.f32cols, layer, jnp.ones((1, F32_COLS), F32), tm=t.rows, norm_cols=0, out_dtype=F32)
    q_scale = LOG2E / math.sqrt(GROUP_DIM)
    colgain = jnp.concatenate([jnp.tile(sb_q_gain * q_scale, SB_HEADS), jnp.tile(sb_k_gain, SB_HEADS),
                               jnp.ones((SB_WIDTH,), F32)]).reshape(1, 3 * SB_WIDTH)
    qkv = _proj(h1, w.sb, layer, colgain, tm=t.rows, norm_cols=2 * SB_WIDTH, out_dtype=BF16)

    proj3 = proj.reshape(batch, seq, F32_COLS)
    y_pool = _pool(proj3, w_pool.astype(BF16), pool_scale, ts=t.pool_rows)
    y_sb = _stick_breaking(qkv.reshape(batch, seq, 3 * SB_WIDTH), tq=t.rows, tk=GROUP_DIM,
                           heads_per_step=t.sb_heads)
    w_a2p = jnp.pad(gla_w_a2, ((0, GROUP_DIM - GLA_RANK), (0, 0))).astype(BF16)
    y_gla = _gla(proj3, w_a2p, gla_b_a2.reshape(1, GLA_WIDTH), gla_out_gain.reshape(1, GROUP_DIM),
                 chunks_per_step=t.gla_chunks)

    m = batch * seq
    merged = _merge(h1, y_pool.reshape(m, POOL_WIDTH), y_sb.reshape(m, SB_WIDTH), y_gla.reshape(m, GLA_WIDTH),
                    w.gates, w.br_pool, w.br_sb, w.br_gla, layer, tm=t.rows_f32)
    x2, h2 = _resproj(merged, w.out, layer, x2, ga1, g_norm2.reshape(1, d), sc2, sh2, seq=seq, tm=t.rows_f32)
    return _swiglu(h2, x2, ga2, w.ff_gate, w.ff_up, w.ff_down, layer, next_norm, seq=seq, tm=t.rows,
                   tf=t.ff_cols)


def kernel(x, c, w_ada, b_ada, g_norm1, w_in, w_pool, pool_scale, sb_q_gain, sb_k_gain, gla_w_a2, gla_b_a2, gla_out_gain, w_br_pool, w_br_sb, w_br_gla, w_out, g_norm2, w_ff_gate, w_ff_up, w_ff_down):
    batch, seq, d = x.shape
    depth = w_ada.shape[0]
    c_pad = jnp.pad(c, ((0, 8 - batch % 8 if batch % 8 else 0), (0, 0)))
    mod = _ada(c_pad, w_ada, b_ada)[:, :batch]
    x2 = x.reshape(batch * seq, d)
    w = _Weights(*_split_w_in(w_in.astype(BF16)), *[a.astype(BF16) for a in (w_br_pool, w_br_sb, w_br_gla, w_out,
                                                               w_ff_gate, w_ff_up, w_ff_down)])

    def first_norm(l):
        sh1, sc1 = _mod_vectors(mod[l])[:2]
        return g_norm1[l].reshape(1, d), sc1, sh1

    h1 = _norm(x2, *first_norm(0), seq=seq, tm=_tiles(seq).rows)
    for l in range(depth):
        next_norm = first_norm(l + 1) if l + 1 < depth else None
        x2, h1 = _layer(x2, h1, mod[l], next_norm, batch, seq, l, w, w_pool[l], pool_scale[l],
                        sb_q_gain[l], sb_k_gain[l], gla_w_a2[l], gla_b_a2[l], gla_out_gain[l], g_norm2[l])
    return x2.reshape(batch, seq, d)
```
